```python
import jax
import jax.numpy as jnp
from jax import lax
import numpy as np

D_MODEL = 1024
BATCH = 2
SEQ = 16384
DEPTH = 4

GRID_W = 64
CTX_LEN = 256
HEAD_DIM = 64
A_HEADS = 6
A_KV_HEADS = 2
WINDOW = 128
B_HEADS = 4
NA_ROWS = 8
NA_COLS = 16
C_HEADS = 6
C_KV_HEADS = 2
MIX_WIDTH = (A_HEADS + B_HEADS + C_HEADS) * HEAD_DIM
Q_BLOCK = 128
ROPE_BASE = 10000.0
N_EXPERTS = 16
N_EXPERT_GROUPS = 4
TOP_K = 2
D_EXPERT = 512
EXPERT_BLOCK = 512
N_MOD = 6
LN_EPS = 1e-6
QK_EPS = 1e-6
NEG_INF = -1e30
DEEPNORM_ALPHA = (2 * DEPTH) ** 0.25
DEEPNORM_BETA = (8 * DEPTH) ** -0.25
PROJ_SIZES = (A_HEADS * HEAD_DIM, A_KV_HEADS * HEAD_DIM, A_KV_HEADS * HEAD_DIM,
              B_HEADS * HEAD_DIM, B_HEADS * HEAD_DIM, B_HEADS * HEAD_DIM,
              C_HEADS * HEAD_DIM, C_KV_HEADS * HEAD_DIM, C_KV_HEADS * HEAD_DIM)
PROJ_WIDTH = sum(PROJ_SIZES)
VALUE_SLOTS = (2, 5, 8)

kernel_name = 'hybrid_dit_parallel_heads_grouped_moe'


def layer_norm(x, g, b):
    xf = x.astype(jnp.float32)
    mu = jnp.mean(xf, -1, keepdims=True)
    xc = xf - mu
    var = jnp.mean(xc * xc, -1, keepdims=True)
    return (xc * lax.rsqrt(var + LN_EPS) * g.astype(jnp.float32) + b.astype(jnp.float32)).astype(x.dtype)


def rms_norm(x, g):
    xf = x.astype(jnp.float32)
    y = xf * lax.rsqrt(jnp.mean(xf * xf, -1, keepdims=True) + QK_EPS) * g.astype(jnp.float32)
    return y.astype(x.dtype)


def axial_rope_angles(n_tok):
    t = jnp.arange(n_tok, dtype=jnp.int32)
    row = (t // GRID_W).astype(jnp.float32)
    col = (t % GRID_W).astype(jnp.float32)
    axis_dim = HEAD_DIM // 2
    inv_freq = ROPE_BASE ** (-jnp.arange(0, axis_dim, 2, dtype=jnp.float32) / axis_dim)
    ang = jnp.concatenate([row[:, None] * inv_freq, col[:, None] * inv_freq], -1)
    return jnp.cos(ang), jnp.sin(ang)


def apply_rope(x, cos, sin):
    xf = x.astype(jnp.float32).reshape(*x.shape[:-1], HEAD_DIM // 2, 2)
    x0, x1 = xf[..., 0], xf[..., 1]
    cs = cos[None, :, None, :]
    sn = sin[None, :, None, :]
    out = jnp.stack([x0 * cs - x1 * sn, x0 * sn + x1 * cs], -1)
    return out.reshape(x.shape).astype(x.dtype)


def split_heads(p):
    return p.reshape(*p.shape[:-1], p.shape[-1] // HEAD_DIM, HEAD_DIM)


def neighbourhood_indices(n_tok, rows):
    kh = min(NA_ROWS, rows)
    kw = NA_COLS
    t = jnp.arange(n_tok, dtype=jnp.int32)
    r = t // GRID_W
    cq = t % GRID_W
    r0 = jnp.clip(r - kh // 2, 0, rows - kh)
    c0 = jnp.clip(cq - kw // 2, 0, GRID_W - kw)
    kr = r0[:, None, None] + jnp.arange(kh, dtype=jnp.int32)[None, :, None]
    kc = c0[:, None, None] + jnp.arange(kw, dtype=jnp.int32)[None, None, :]
    idx = (kr * GRID_W + kc).reshape(n_tok, kh * kw)
    rel = ((kr - r[:, None, None] + NA_ROWS - 1) * (2 * NA_COLS - 1)
           + (kc - cq[:, None, None] + NA_COLS - 1)).reshape(n_tok, kh * kw)
    return idx, rel


def dense_attention(q, k, v, sink):
    bsz, n_q, n_heads, _ = q.shape
    n_kv = k.shape[2]
    grp = n_heads // n_kv
    n_k = k.shape[1]
    scale = HEAD_DIM ** -0.5
    qg = q.reshape(bsz, n_q, n_kv, grp, HEAD_DIM)
    s = jnp.einsum('bqhgd,bkhd->bhgqk', qg, k, preferred_element_type=jnp.float32) * scale
    if sink is not None:
        sink_col = jnp.broadcast_to(sink.astype(jnp.float32).reshape(1, n_kv, grp, 1, 1), (bsz, n_kv, grp, n_q, 1))
        s = jnp.concatenate([s, sink_col], -1)
    p = jax.nn.softmax(s, -1)[..., :n_k].astype(v.dtype)
    return jnp.einsum('bhgqk,bkhd->bqhgd', p, v).reshape(bsz, n_q, n_heads * HEAD_DIM)


def window_attention(q, k, v, k_ctx, v_ctx, sink):
    bsz, n, n_heads, _ = q.shape
    n_kv = k.shape[2]
    grp = n_heads // n_kv
    n_ctx = k_ctx.shape[1]
    n_blk = n // Q_BLOCK
    span = Q_BLOCK + 2 * WINDOW
    scale = HEAD_DIM ** -0.5
    pad = ((0, 0), (WINDOW, WINDOW), (0, 0), (0, 0))
    k_pad = jnp.pad(k, pad)
    v_pad = jnp.pad(v, pad)
    q_blk = q.reshape(bsz, n_blk, Q_BLOCK, n_kv, grp, HEAD_DIM).swapaxes(0, 1)
    sink_col = jnp.broadcast_to(sink.astype(jnp.float32).reshape(1, n_kv, grp, 1, 1), (bsz, n_kv, grp, Q_BLOCK, 1))

    def block(args):
        q_i, i = args
        start = i * Q_BLOCK
        k_i = lax.dynamic_slice_in_dim(k_pad, start, span, axis=1)
        v_i = lax.dynamic_slice_in_dim(v_pad, start, span, axis=1)
        q_pos = start + jnp.arange(Q_BLOCK, dtype=jnp.int32)
        k_pos = start - WINDOW + jnp.arange(span, dtype=jnp.int32)
        ok = (jnp.abs(q_pos[:, None] - k_pos[None, :]) <= WINDOW) & (k_pos >= 0)[None, :] & (k_pos < n)[None, :]
        s_loc = jnp.einsum('bqhgd,bkhd->bhgqk', q_i, k_i, preferred_element_type=jnp.float32) * scale
        s_loc = jnp.where(ok, s_loc, NEG_INF)
        s_ctx = jnp.einsum('bqhgd,bchd->bhgqc', q_i, k_ctx, preferred_element_type=jnp.float32) * scale
        p = jax.nn.softmax(jnp.concatenate([s_loc, s_ctx, sink_col], -1), -1).astype(v.dtype)
        o = (jnp.einsum('bhgqk,bkhd->bqhgd', p[..., :span], v_i)
             + jnp.einsum('bhgqc,bchd->bqhgd', p[..., span:span + n_ctx], v_ctx))
        return o.reshape(bsz, Q_BLOCK, n_heads * HEAD_DIM)

    o = lax.map(block, (q_blk, jnp.arange(n_blk, dtype=jnp.int32)))
    return o.swapaxes(0, 1).reshape(bsz, n, n_heads * HEAD_DIM)


def neighbourhood_attention(q, k, v, k_ctx, v_ctx, rpb, idx, rel):
    bsz, n, n_heads, _ = q.shape
    n_blk = n // Q_BLOCK
    n_nb = idx.shape[1]
    scale = HEAD_DIM ** -0.5
    q_blk = q.reshape(bsz, n_blk, Q_BLOCK, n_heads, HEAD_DIM).swapaxes(0, 1)
    idx_blk = idx.reshape(n_blk, Q_BLOCK, n_nb)
    rel_blk = rel.reshape(n_blk, Q_BLOCK, n_nb)
    bias_tab = rpb.astype(jnp.float32).reshape(n_heads, -1)

    def block(args):
        q_i, idx_i, rel_i = args
        k_i = k[:, idx_i]
        v_i = v[:, idx_i]
        s_nb = jnp.einsum('bqhd,bqkhd->bhqk', q_i, k_i, preferred_element_type=jnp.float32) * scale
        s_nb = s_nb + bias_tab[:, rel_i][None]
        s_ctx = jnp.einsum('bqhd,bchd->bhqc', q_i, k_ctx, preferred_element_type=jnp.float32) * scale
        p = jax.nn.softmax(jnp.concatenate([s_nb, s_ctx], -1), -1).astype(v.dtype)
        o = (jnp.einsum('bhqk,bqkhd->bqhd', p[..., :n_nb], v_i)
             + jnp.einsum('bhqc,bchd->bqhd', p[..., n_nb:], v_ctx))
        return o.reshape(bsz, Q_BLOCK, n_heads * HEAD_DIM)

    o = lax.map(block, (q_blk, idx_blk, rel_blk))
    return o.swapaxes(0, 1).reshape(bsz, n, n_heads * HEAD_DIM)


def global_attention(q, k, v, k_ctx, v_ctx):
    bsz, n, n_heads, _ = q.shape
    n_kv = k.shape[2]
    grp = n_heads // n_kv
    n_blk = n // Q_BLOCK
    scale = HEAD_DIM ** -0.5
    k_all = jnp.concatenate([k, k_ctx], 1)
    v_all = jnp.concatenate([v, v_ctx], 1)
    q_blk = q.reshape(bsz, n_blk, Q_BLOCK, n_kv, grp, HEAD_DIM).swapaxes(0, 1)

    def block(q_i):
        s = jnp.einsum('bqhgd,bkhd->bhgqk', q_i, k_all, preferred_element_type=jnp.float32) * scale
        p = jax.nn.softmax(s, -1).astype(v.dtype)
        return jnp.einsum('bhgqk,bkhd->bqhgd', p, v_all).reshape(bsz, Q_BLOCK, n_heads * HEAD_DIM)

    o = lax.map(block, q_blk)
    return o.swapaxes(0, 1).reshape(bsz, n, n_heads * HEAD_DIM)


def token_mixers(u, u_c, w_in, w_out, sink, rpb, q_gain, k_gain, cos, sin, na_idx, na_rel, need_ctx_out):
    cuts = [sum(PROJ_SIZES[:i + 1]) for i in range(len(PROJ_SIZES) - 1)]
    qw, kw, vw, qn, kn, vn, qg, kg, vg = [split_heads(p) for p in jnp.split(u @ w_in, cuts, axis=-1)]
    qw_c, kw_c, vw_c, qn_c, kn_c, vn_c, qg_c, kg_c, vg_c = [split_heads(p) for p in jnp.split(u_c @ w_in, cuts, axis=-1)]
    o_w = window_attention(apply_rope(qw, cos, sin), apply_rope(kw, cos, sin), vw, kw_c, vw_c, sink)
    o_n = neighbourhood_attention(qn, kn, vn, kn_c, vn_c, rpb, na_idx, na_rel)
    qg_c = rms_norm(qg_c, q_gain)
    kg_c = rms_norm(kg_c, k_gain)
    o_g = global_attention(apply_rope(rms_norm(qg, q_gain), cos, sin), apply_rope(rms_norm(kg, k_gain), cos, sin),
                           vg, kg_c, vg_c)
    out = jnp.concatenate([o_w, o_n, o_g], -1) @ w_out
    if not need_ctx_out:
        return out, None
    out_c = jnp.concatenate([dense_attention(qw_c, kw_c, vw_c, sink),
                             dense_attention(qn_c, kn_c, vn_c, None),
                             dense_attention(qg_c, kg_c, vg_c, None)], -1) @ w_out
    return out, out_c


def moe_ffn(h, w_router, router_bias, w1, w3, w2):
    n, d = h.shape
    per_group = N_EXPERTS // N_EXPERT_GROUPS
    affinity = jax.nn.sigmoid(jnp.dot(h, w_router, preferred_element_type=jnp.float32))
    select = (affinity + router_bias.astype(jnp.float32)).reshape(n, N_EXPERT_GROUPS, per_group)
    group = jnp.argmax(lax.top_k(select, TOP_K)[0].sum(-1), -1)
    tok = jnp.arange(n, dtype=jnp.int32)
    _, local = lax.top_k(select[tok, group], TOP_K)
    expert = group[:, None] * per_group + local
    gate = affinity[tok[:, None], expert]
    gate = gate / jnp.sum(gate, -1, keepdims=True)
    n_assign = n * TOP_K
    flat_e = expert.reshape(-1)
    order = jnp.argsort(flat_e)
    sorted_e = flat_e[order]
    counts = jnp.bincount(flat_e, length=N_EXPERTS)
    starts = jnp.cumsum(counts) - counts
    padded = (counts + EXPERT_BLOCK - 1) // EXPERT_BLOCK * EXPERT_BLOCK
    pad_ends = jnp.cumsum(padded)
    dest = (pad_ends - padded)[sorted_e] + jnp.arange(n_assign, dtype=jnp.int32) - starts[sorted_e]
    n_blocks = -(-n_assign // EXPERT_BLOCK) + N_EXPERTS
    n_slots = n_blocks * EXPERT_BLOCK
    slot_tok = jnp.full((n_slots,), n, jnp.int32).at[dest].set((order // TOP_K).astype(jnp.int32))
    slot_gate = jnp.zeros((n_slots,), jnp.float32).at[dest].set(gate.reshape(-1)[order])
    block_expert = jnp.minimum(
        jnp.searchsorted(pad_ends, jnp.arange(n_blocks, dtype=jnp.int32) * EXPERT_BLOCK, side='right'),
        N_EXPERTS - 1)
    h_pad = jnp.concatenate([h, jnp.zeros((1, d), h.dtype)], 0)

    def expert_block(args):
        tok_b, gate_b, e = args
        xb = h_pad[tok_b]
        hid = jax.nn.silu(xb @ w1[e]) * (xb @ w3[e])
        return (hid @ w2[e]) * gate_b[:, None].astype(h.dtype)

    y = lax.map(expert_block, (slot_tok.reshape(n_blocks, EXPERT_BLOCK),
                               slot_gate.reshape(n_blocks, EXPERT_BLOCK), block_expert))
    return jnp.zeros((n + 1, d), h.dtype).at[slot_tok].add(y.reshape(n_slots, d))[:n]


def setup_inputs(seed: int = 0) -> dict:
    key = jax.random.key(seed)
    ks = jax.random.split(key, 21)
    nrm = jax.random.normal
    f32 = jnp.float32
    col_scale = np.concatenate([np.full((s,), DEEPNORM_BETA if i in VALUE_SLOTS else 1.0, np.float32)
                                for i, s in enumerate(PROJ_SIZES)])
    return {
        'x': nrm(ks[0], (BATCH, SEQ, D_MODEL), f32),
        'c': nrm(ks[1], (BATCH, D_MODEL), f32),
        'ctx': nrm(ks[2], (BATCH, CTX_LEN, D_MODEL), f32),
        'c_ctx': nrm(ks[3], (D_MODEL,), f32),
        'w_in': nrm(ks[4], (DEPTH, D_MODEL, PROJ_WIDTH), f32) * (D_MODEL ** -0.5) * jnp.asarray(col_scale),
        'w_out': nrm(ks[5], (DEPTH, MIX_WIDTH, D_MODEL), f32) * (MIX_WIDTH ** -0.5 * DEEPNORM_BETA),
        'sink': 0.5 * nrm(ks[6], (DEPTH, A_HEADS), f32),
        'rpb': 0.1 * nrm(ks[7], (DEPTH, B_HEADS, 2 * NA_ROWS - 1, 2 * NA_COLS - 1), f32),
        'q_gain': 1.0 + 0.05 * nrm(ks[8], (DEPTH, HEAD_DIM), f32),
        'k_gain': 1.0 + 0.05 * nrm(ks[9], (DEPTH, HEAD_DIM), f32),
        'w_ada': nrm(ks[10], (DEPTH, D_MODEL, N_MOD * D_MODEL), f32) * (0.1 * D_MODEL ** -0.5),
        'b_ada': 0.01 * nrm(ks[11], (DEPTH, N_MOD * D_MODEL), f32),
        'ln1_g': 1.0 + 0.05 * nrm(ks[12], (DEPTH, D_MODEL), f32),
        'ln1_b': 0.01 * nrm(ks[13], (DEPTH, D_MODEL), f32),
        'ln2_g': 1.0 + 0.05 * nrm(ks[14], (DEPTH, D_MODEL), f32),
        'ln2_b': 0.01 * nrm(ks[15], (DEPTH, D_MODEL), f32),
        'w_router': nrm(ks[16], (D_MODEL, N_EXPERTS), f32) * (D_MODEL ** -0.5),
        'router_bias': 0.01 * nrm(ks[17], (N_EXPERTS,), f32),
        'w1': nrm(ks[18], (DEPTH, N_EXPERTS, D_MODEL, D_EXPERT), f32) * (D_MODEL ** -0.5),
        'w3': nrm(ks[19], (DEPTH, N_EXPERTS, D_MODEL, D_EXPERT), f32) * (D_MODEL ** -0.5),
        'w2': nrm(ks[20], (DEPTH, N_EXPERTS, D_EXPERT, D_MODEL), f32) * (D_EXPERT ** -0.5 * DEEPNORM_BETA),
    }


def reference(x, c, ctx, c_ctx, w_in, w_out, sink, rpb, q_gain, k_gain, w_ada, b_ada,
              ln1_g, ln1_b, ln2_g, ln2_b, w_router, router_bias, w1, w3, w2):
    bsz, n_tok, d = x.shape
    n_ctx = ctx.shape[1]
    rows = n_tok // GRID_W
    cos, sin = axial_rope_angles(n_tok)
    na_idx, na_rel = neighbourhood_indices(n_tok, rows)
    c_act = jax.nn.silu(c)
    cc_act = jax.nn.silu(c_ctx)
    h_ctx = ctx
    for layer in range(DEPTH):
        last = layer == DEPTH - 1
        mod = jnp.split(c_act @ w_ada[layer] + b_ada[layer], N_MOD, axis=-1)
        mod_c = jnp.split(cc_act @ w_ada[layer] + b_ada[layer], N_MOD, axis=-1)
        shift1, scale1, gate1, shift2, scale2, gate2 = [m[:, None, :] for m in mod]
        cshift1, cscale1, cgate1, cshift2, cscale2, cgate2 = mod_c
        u = x * (1 + scale1) + shift1
        u_c = h_ctx * (1 + cscale1) + cshift1
        o, o_c = token_mixers(u, u_c, w_in[layer], w_out[layer], sink[layer], rpb[layer],
                              q_gain[layer], k_gain[layer], cos, sin, na_idx, na_rel, not last)
        x = layer_norm(DEEPNORM_ALPHA * x + (1 + gate1) * o, ln1_g[layer], ln1_b[layer])
        u = x * (1 + scale2) + shift2
        if last:
            f = moe_ffn(u.reshape(-1, d), w_router, router_bias, w1[layer], w3[layer], w2[layer]).reshape(bsz, n_tok, d)
        else:
            h_ctx = layer_norm(DEEPNORM_ALPHA * h_ctx + (1 + cgate1) * o_c, ln1_g[layer], ln1_b[layer])
            u_c = h_ctx * (1 + cscale2) + cshift2
            f_all = moe_ffn(jnp.concatenate([u.reshape(-1, d), u_c.reshape(-1, d)], 0),
                            w_router, router_bias, w1[layer], w3[layer], w2[layer])
            f = f_all[:bsz * n_tok].reshape(bsz, n_tok, d)
            f_c = f_all[bsz * n_tok:].reshape(bsz, n_ctx, d)
            h_ctx = layer_norm(DEEPNORM_ALPHA * h_ctx + (1 + cgate2) * f_c, ln2_g[layer], ln2_b[layer])
        x = layer_norm(DEEPNORM_ALPHA * x + (1 + gate2) * f, ln2_g[layer], ln2_b[layer])
    return x
```

```python
import functools

import numpy as np
import jax
import jax.numpy as jnp
from jax import lax
from jax.experimental import pallas as pl
from jax.experimental.pallas import tpu as pltpu

F32 = jnp.float32
BF16 = jnp.bfloat16

D_MODEL = 1024
HEAD_DIM = 64
HALF = HEAD_DIM // 2
GRID_W = 64
A_HEADS, A_KV = 6, 2
WINDOW = 128
B_HEADS = 4
NA_ROWS, NA_COLS = 8, 16
C_HEADS, C_KV = 6, 2
GQA_GROUP = 3
ROPE_BASE = 10000.0
N_EXPERTS = 16
N_EXPERT_GROUPS = 4
PER_GROUP = N_EXPERTS // N_EXPERT_GROUPS
TOP_K = 2
D_EXPERT = 512
EXPERT_BLOCK = 512
N_MOD = 6
LN_EPS = 1e-6
QK_EPS = 1e-6
NEG_INF = -1e30

QA_W, QB_W, QC_W = A_HEADS * HEAD_DIM, B_HEADS * HEAD_DIM, C_HEADS * HEAD_DIM
KA_W, KB_W, KC_W = A_KV * HEAD_DIM, B_HEADS * HEAD_DIM, C_KV * HEAD_DIM
R_QA = 0
R_QB = R_QA + QA_W
R_QC = R_QB + QB_W
R_KA = R_QC + QC_W
R_KB = R_KA + KA_W
R_KC = R_KB + KB_W
R_VA = R_KC + KC_W
R_VB = R_VA + KA_W
R_VC = R_VB + KB_W
PROJ_WIDTH = R_VC + KC_W

V7X_VMEM_LIMIT = 56 * 1024 * 1024
TOK_TILE = 256
TQ_A = 256
TQ_B = 2 * GRID_W
TQ_C = 256
TK_C = 512
MOD_ROWS = 8


def _params(sem):
    return pltpu.CompilerParams(dimension_semantics=sem, vmem_limit_bytes=V7X_VMEM_LIMIT)


def _ada_kernel(c_ref, w_ref, b_ref, o_ref):
    c = c_ref[...]
    act = c * jax.nn.sigmoid(c)
    o_ref[...] = jnp.dot(act, w_ref[...], preferred_element_type=F32,
                         precision=lax.Precision.HIGHEST) + b_ref[...]


def _ada_table(cond, w_ada, b_ada):
    depth, d, n = w_ada.shape
    tn = n // 4
    return pl.pallas_call(
        _ada_kernel,
        grid=(depth, n // tn),
        in_specs=[pl.BlockSpec((MOD_ROWS, d), lambda l, j: (0, 0)),
                  pl.BlockSpec((None, d, tn), lambda l, j: (l, 0, j)),
                  pl.BlockSpec((None, 1, tn), lambda l, j: (l, 0, j))],
        out_specs=pl.BlockSpec((None, MOD_ROWS, tn), lambda l, j: (l, 0, j)),
        out_shape=jax.ShapeDtypeStruct((depth, MOD_ROWS, n), F32),
        compiler_params=_params(("parallel", "parallel")),
    )(cond, w_ada, b_ada.reshape(depth, 1, n))


def _proj_kernel(x_ref, mod_ref, w_ref, cos_ref, sin_ref, qg_ref, kg_ref,
                 qa_ref, qb_ref, qc_ref, ka_ref, kb_ref, kc_ref, va_ref, vb_ref, vc_ref):
    tm = x_ref.shape[0]
    u = (x_ref[...] * (1.0 + mod_ref[1:2, :]) + mod_ref[0:1, :]).astype(BF16)

    def proj(lo, width):
        return lax.dot_general(w_ref[lo:lo + width, :], u, (((1,), (1,)), ((), ())),
                               preferred_element_type=F32)

    cos = cos_ref[...]
    sin = sin_ref[...]

    def rope(blk):
        x0, x1 = blk[:HALF], blk[HALF:]
        return x0 * cos - x1 * sin, x0 * sin + x1 * cos

    def qk_norm(blk, gain):
        ms = jnp.mean(blk * blk, axis=0, keepdims=True)
        return blk * lax.rsqrt(ms + QK_EPS) * gain

    p = proj(R_QA, QA_W)
    for h in range(A_HEADS):
        o0, o1 = rope(p[h * HEAD_DIM:(h + 1) * HEAD_DIM])
        qa_ref[h * HEAD_DIM:h * HEAD_DIM + HALF, :] = o0.astype(BF16)
        qa_ref[h * HEAD_DIM + HALF:(h + 1) * HEAD_DIM, :] = o1.astype(BF16)
    qb_ref[...] = proj(R_QB, QB_W).astype(BF16)
    p = proj(R_QC, QC_W)
    qg = qg_ref[...]
    for h in range(C_HEADS):
        o0, o1 = rope(qk_norm(p[h * HEAD_DIM:(h + 1) * HEAD_DIM], qg))
        qc_ref[h * HEAD_DIM:h * HEAD_DIM + HALF, :] = o0.astype(BF16)
        qc_ref[h * HEAD_DIM + HALF:(h + 1) * HEAD_DIM, :] = o1.astype(BF16)
    p = proj(R_KA, KA_W)
    parts = []
    for h in range(A_KV):
        parts.extend(rope(p[h * HEAD_DIM:(h + 1) * HEAD_DIM]))
    ka_ref[...] = jnp.concatenate(parts, axis=0).T.astype(BF16)
    kb_ref[...] = proj(R_KB, KB_W).T.astype(BF16)
    p = proj(R_KC, KC_W)
    kg = kg_ref[...]
    parts = []
    for h in range(C_KV):
        parts.extend(rope(qk_norm(p[h * HEAD_DIM:(h + 1) * HEAD_DIM], kg)))
    kc_ref[...] = jnp.concatenate(parts, axis=0).T.astype(BF16)
    ones_row = (lax.broadcasted_iota(jnp.int32, (HEAD_DIM, tm), 0) == 0).astype(BF16)
    for lo, ref in ((R_VA, va_ref), (R_VC, vc_ref)):
        p = proj(lo, KA_W)
        for g in range(A_KV):
            ref[g, 0:HEAD_DIM, :] = p[g * HEAD_DIM:(g + 1) * HEAD_DIM].astype(BF16)
            ref[g, HEAD_DIM:2 * HEAD_DIM, :] = ones_row
    vb_ref[...] = proj(R_VB, KB_W).astype(BF16)


def _project(xa, mod, w_t, cos_t, sin_t, q_gain, k_gain, n_lat):
    bsz, s_tot, d = xa.shape
    tm = TOK_TILE
    n_lat_tiles = n_lat // tm
    sd = jax.ShapeDtypeStruct

    def dmajor(width):
        return pl.BlockSpec((None, width, tm), lambda b, i: (b, 0, i))

    def tmajor(width):
        return pl.BlockSpec((None, tm, width), lambda b, i: (b, i, 0))

    aug = pl.BlockSpec((None, A_KV, 2 * HEAD_DIM, tm), lambda b, i: (b, 0, 0, i))
    return pl.pallas_call(
        _proj_kernel,
        grid=(bsz, s_tot // tm),
        in_specs=[tmajor(d),
                  pl.BlockSpec((None, N_MOD, d), lambda b, i: (jnp.where(i >= n_lat_tiles, bsz, b), 0, 0)),
                  pl.BlockSpec((PROJ_WIDTH, d), lambda b, i: (0, 0)),
                  pl.BlockSpec((HALF, tm), lambda b, i: (0, i)),
                  pl.BlockSpec((HALF, tm), lambda b, i: (0, i)),
                  pl.BlockSpec((HEAD_DIM, 1), lambda b, i: (0, 0)),
                  pl.BlockSpec((HEAD_DIM, 1), lambda b, i: (0, 0))],
        out_specs=[dmajor(QA_W), dmajor(QB_W), dmajor(QC_W),
                   tmajor(KA_W), tmajor(KB_W), tmajor(KC_W),
                   aug, dmajor(KB_W), aug],
        out_shape=[sd((bsz, QA_W, s_tot), BF16), sd((bsz, QB_W, s_tot), BF16), sd((bsz, QC_W, s_tot), BF16),
                   sd((bsz, s_tot, KA_W), BF16), sd((bsz, s_tot, KB_W), BF16), sd((bsz, s_tot, KC_W), BF16),
                   sd((bsz, A_KV, 2 * HEAD_DIM, s_tot), BF16), sd((bsz, KB_W, s_tot), BF16),
                   sd((bsz, A_KV, 2 * HEAD_DIM, s_tot), BF16)],
        compiler_params=_params(("parallel", "parallel")),
    )(xa, mod, w_t, cos_t, sin_t, q_gain, k_gain)


def _padded_q(q_ref, h, g):
    blk = q_ref[h * HEAD_DIM:(h + 1) * HEAD_DIM, :]
    zeros = jnp.zeros_like(blk)
    return jnp.concatenate([blk, zeros] if g == 0 else [zeros, blk], axis=0)


def _attn_a_kernel(sink_ref, q_ref, k_ref, v_ref, o_ref, ot_ref, *, n_lat, n_ctx):
    tq = q_ref.shape[1]
    span = tq + 2 * WINDOW
    start = pl.program_id(1) * tq
    ks = pl.multiple_of(jnp.clip(start - WINDOW, 0, n_lat - span), 128)
    kpos = ks + lax.broadcasted_iota(jnp.int32, (span, tq), 0)
    qpos = start + lax.broadcasted_iota(jnp.int32, (span, tq), 1)
    ok = (jnp.abs(qpos - kpos) <= WINDOW) & (qpos < n_lat)
    k_loc = k_ref[pl.ds(ks, span), :]
    k_ctx = k_ref[n_lat:n_lat + n_ctx, :]
    for g in range(A_KV):
        v_loc = v_ref[g, :, pl.ds(ks, span)]
        v_ctx = v_ref[g, :, n_lat:n_lat + n_ctx]
        for hh in range(GQA_GROUP):
            h = g * GQA_GROUP + hh
            qp = _padded_q(q_ref, h, g)
            s_loc = jnp.where(ok, jnp.dot(k_loc, qp, preferred_element_type=F32), NEG_INF)
            s_ctx = jnp.dot(k_ctx, qp, preferred_element_type=F32)
            sink = sink_ref[h]
            m = jnp.maximum(jnp.maximum(jnp.max(s_loc, axis=0, keepdims=True),
                                        jnp.max(s_ctx, axis=0, keepdims=True)), sink)
            p_loc = jnp.exp(s_loc - m).astype(BF16)
            p_ctx = jnp.exp(s_ctx - m).astype(BF16)
            acc = (jnp.dot(v_loc, p_loc, preferred_element_type=F32)
                   + jnp.dot(v_ctx, p_ctx, preferred_element_type=F32))
            denom = acc[HEAD_DIM:HEAD_DIM + 1, :] + jnp.exp(sink - m)
            ot_ref[h * HEAD_DIM:(h + 1) * HEAD_DIM, :] = acc[0:HEAD_DIM, :] / denom
    o_ref[...] = ot_ref[...].T.astype(BF16)


def _attn_a(sink, q_t, k, v_t, n_lat):
    bsz, s_tot, _ = k.shape
    tq = TQ_A
    kern = functools.partial(_attn_a_kernel, n_lat=n_lat, n_ctx=s_tot - n_lat)
    return pl.pallas_call(
        kern,
        grid=(bsz, s_tot // tq),
        in_specs=[pl.BlockSpec(memory_space=pltpu.SMEM),
                  pl.BlockSpec((None, QA_W, tq), lambda b, i: (b, 0, i)),
                  pl.BlockSpec((None, s_tot, KA_W), lambda b, i: (b, 0, 0)),
                  pl.BlockSpec((None, A_KV, 2 * HEAD_DIM, s_tot), lambda b, i: (b, 0, 0, 0))],
        out_specs=pl.BlockSpec((None, tq, QA_W), lambda b, i: (b, i, 0)),
        out_shape=jax.ShapeDtypeStruct((bsz, s_tot, QA_W), BF16),
        scratch_shapes=[pltpu.VMEM((QA_W, tq), F32)],
        compiler_params=_params(("parallel", "arbitrary")),
    )(sink, q_t, k, v_t)


def _nb_plan(n_lat, s_tot):
    rows = n_lat // GRID_W
    kh = min(NA_ROWS, rows)
    q_rows = TQ_B // GRID_W
    w_rows = kh + q_rows
    n_lat_blocks = n_lat // TQ_B
    n_blocks = s_tot // TQ_B
    starts, variants, keys, reps = [], [], {}, []
    for i in range(n_lat_blocks):
        w0 = int(np.clip(q_rows * i - kh // 2, 0, rows - w_rows))
        r0s = tuple(int(np.clip(q_rows * i + j - kh // 2, 0, rows - kh)) - w0 for j in range(q_rows))
        key = (q_rows * i - w0,) + r0s
        if key not in keys:
            keys[key] = len(reps)
            reps.append((i, w0))
        starts.append(w0 * GRID_W)
        variants.append(keys[key])
    masked = len(reps)
    starts.extend([0] * (n_blocks - n_lat_blocks))
    variants.extend([masked] * (n_blocks - n_lat_blocks))
    return rows, kh, w_rows, reps, np.asarray(starts, np.int32), np.asarray(variants, np.int32)


def _nb_bias_tiles(rpb, n_lat, s_tot):
    rows, kh, w_rows, reps, starts, variants = _nb_plan(n_lat, s_tot)
    n_keys = w_rows * GRID_W
    kk = np.arange(n_keys)[:, None]
    qq = np.arange(TQ_B)[None, :]
    tiles = []
    for i, w0 in reps:
        kr, kc = w0 + kk // GRID_W, kk % GRID_W
        r, cq = (TQ_B // GRID_W) * i + qq // GRID_W, qq % GRID_W
        r0 = np.clip(r - kh // 2, 0, rows - kh)
        c0 = np.clip(cq - NA_COLS // 2, 0, GRID_W - NA_COLS)
        inside = (kr >= r0) & (kr < r0 + kh) & (kc >= c0) & (kc < c0 + NA_COLS)
        rel_r = np.clip(kr - r + NA_ROWS - 1, 0, 2 * NA_ROWS - 2)
        rel_c = np.clip(kc - cq + NA_COLS - 1, 0, 2 * NA_COLS - 2)
        vals = rpb.astype(F32)[:, rel_r, rel_c]
        tile = jnp.where(jnp.asarray(inside)[None], vals, NEG_INF)
        tiles.append(jnp.transpose(tile, (1, 0, 2)).reshape(n_keys, B_HEADS * TQ_B))
    tiles.append(jnp.full((n_keys, B_HEADS * TQ_B), NEG_INF, F32))
    return jnp.stack(tiles), jnp.asarray(starts), jnp.asarray(variants), n_keys


def _attn_b_kernel(start_ref, var_ref, q_ref, k_ref, v_ref, bias_ref, o_ref, qp_ref, ot_ref, *, n_lat, n_ctx, n_keys):
    tq = q_ref.shape[1]
    ws = pl.multiple_of(start_ref[pl.program_id(1)], 128)
    qp_ref[...] = jnp.zeros_like(qp_ref)
    for h in range(B_HEADS):
        qp_ref[h * HEAD_DIM:(h + 1) * HEAD_DIM, h * tq:(h + 1) * tq] = q_ref[h * HEAD_DIM:(h + 1) * HEAD_DIM, :]
    qp = qp_ref[...]
    s_loc = jnp.dot(k_ref[pl.ds(ws, n_keys), :], qp, preferred_element_type=F32) + bias_ref[...]
    s_ctx = jnp.dot(k_ref[n_lat:n_lat + n_ctx, :], qp, preferred_element_type=F32)
    m = jnp.maximum(jnp.max(s_loc, axis=0, keepdims=True), jnp.max(s_ctx, axis=0, keepdims=True))
    p_loc = jnp.exp(s_loc - m)
    p_ctx = jnp.exp(s_ctx - m)
    denom = jnp.sum(p_loc, axis=0, keepdims=True) + jnp.sum(p_ctx, axis=0, keepdims=True)
    acc = (jnp.dot(v_ref[:, pl.ds(ws, n_keys)], p_loc.astype(BF16), preferred_element_type=F32)
           + jnp.dot(v_ref[:, n_lat:n_lat + n_ctx], p_ctx.astype(BF16), preferred_element_type=F32))
    for h in range(B_HEADS):
        ot_ref[h * HEAD_DIM:(h + 1) * HEAD_DIM, :] = (
            acc[h * HEAD_DIM:(h + 1) * HEAD_DIM, h * tq:(h + 1) * tq] / denom[:, h * tq:(h + 1) * tq])
    o_ref[...] = ot_ref[...].T.astype(BF16)


def _attn_b(q_t, k, v_t, bias_tiles, starts, variants, n_keys, n_lat):
    bsz, s_tot, _ = k.shape
    tq = TQ_B
    kern = functools.partial(_attn_b_kernel, n_lat=n_lat, n_ctx=s_tot - n_lat, n_keys=n_keys)
    grid_spec = pltpu.PrefetchScalarGridSpec(
        num_scalar_prefetch=2,
        grid=(bsz, s_tot // tq),
        in_specs=[pl.BlockSpec((None, QB_W, tq), lambda b, i, st, va: (b, 0, i)),
                  pl.BlockSpec((None, s_tot, KB_W), lambda b, i, st, va: (b, 0, 0)),
                  pl.BlockSpec((None, KB_W, s_tot), lambda b, i, st, va: (b, 0, 0)),
                  pl.BlockSpec((None, n_keys, B_HEADS * tq), lambda b, i, st, va: (va[i], 0, 0))],
        out_specs=pl.BlockSpec((None, tq, QB_W), lambda b, i, st, va: (b, i, 0)),
        scratch_shapes=[pltpu.VMEM((QB_W, B_HEADS * tq), BF16), pltpu.VMEM((QB_W, tq), F32)],
    )
    return pl.pallas_call(
        kern,
        grid_spec=grid_spec,
        out_shape=jax.ShapeDtypeStruct((bsz, s_tot, QB_W), BF16),
        compiler_params=_params(("parallel", "arbitrary")),
    )(starts, variants, q_t, k, v_t, bias_tiles)


def _attn_c_kernel(q_ref, k_ref, v_ref, o_ref, qp_ref, m_ref, acc_ref, ot_ref, *, n_lat, n_ctx, tk):
    tq = q_ref.shape[1]
    is_ctx = pl.program_id(1) >= n_lat // tq
    n_chunks = jnp.where(is_ctx, 0, n_lat // tk)
    for g in range(C_KV):
        for hh in range(GQA_GROUP):
            qp_ref[:, hh * tq:(hh + 1) * tq] = _padded_q(q_ref, g * GQA_GROUP + hh, g)
        m_ref[...] = jnp.full_like(m_ref, NEG_INF)
        acc_ref[...] = jnp.zeros_like(acc_ref)

        def step(start, size):
            s = jnp.dot(k_ref[pl.ds(start, size), :], qp_ref[...], preferred_element_type=F32)
            m_prev = m_ref[...]
            m_new = jnp.maximum(m_prev, jnp.max(s, axis=0, keepdims=True))
            p = jnp.exp(s - m_new).astype(BF16)
            pv = jnp.dot(v_ref[g, :, pl.ds(start, size)], p, preferred_element_type=F32)
            acc_ref[...] = acc_ref[...] * jnp.exp(m_prev - m_new) + pv
            m_ref[...] = m_new

        def body(c, carry):
            step(pl.multiple_of(c * tk, tk), tk)
            return carry

        lax.fori_loop(0, n_chunks, body, 0)
        step(n_lat, n_ctx)
        acc = acc_ref[...]
        out = acc[0:HEAD_DIM, :] / acc[HEAD_DIM:HEAD_DIM + 1, :]
        for hh in range(GQA_GROUP):
            h = g * GQA_GROUP + hh
            ot_ref[h * HEAD_DIM:(h + 1) * HEAD_DIM, :] = out[:, hh * tq:(hh + 1) * tq]
    o_ref[...] = ot_ref[...].T.astype(BF16)


def _attn_c(q_t, k, v_t, n_lat):
    bsz, s_tot, _ = k.shape
    tq = TQ_C
    tk = min(TK_C, n_lat)
    kern = functools.partial(_attn_c_kernel, n_lat=n_lat, n_ctx=s_tot - n_lat, tk=tk)
    return pl.pallas_call(
        kern,
        grid=(bsz, s_tot // tq),
        in_specs=[pl.BlockSpec((None, QC_W, tq), lambda b, i: (b, 0, i)),
                  pl.BlockSpec((None, s_tot, KC_W), lambda b, i: (b, 0, 0)),
                  pl.BlockSpec((None, C_KV, 2 * HEAD_DIM, s_tot), lambda b, i: (b, 0, 0, 0))],
        out_specs=pl.BlockSpec((None, tq, QC_W), lambda b, i: (b, i, 0)),
        out_shape=jax.ShapeDtypeStruct((bsz, s_tot, QC_W), BF16),
        scratch_shapes=[pltpu.VMEM((2 * HEAD_DIM, GQA_GROUP * tq), BF16),
                        pltpu.VMEM((1, GQA_GROUP * tq), F32),
                        pltpu.VMEM((2 * HEAD_DIM, GQA_GROUP * tq), F32),
                        pltpu.VMEM((QC_W, tq), F32)],
        compiler_params=_params(("parallel", "arbitrary")),
    )(q_t, k, v_t)


def _layer_norm(h, g, b):
    mu = jnp.mean(h, axis=-1, keepdims=True)
    hc = h - mu
    var = jnp.mean(hc * hc, axis=-1, keepdims=True)
    return hc * lax.rsqrt(var + LN_EPS) * g + b


def _post_kernel(oa_ref, ob_ref, oc_ref, x_ref, mod_ref, w_ref, g_ref, b_ref, wr_ref, rb_ref,
                 x1_ref, u2_ref, eid_ref, gate_ref, *, alpha):
    o_cat = jnp.concatenate([oa_ref[...], ob_ref[...], oc_ref[...]], axis=1)
    o = jnp.dot(o_cat, w_ref[...], preferred_element_type=F32)
    x1 = _layer_norm(alpha * x_ref[...] + (1.0 + mod_ref[2:3, :]) * o, g_ref[...], b_ref[...])
    x1_ref[...] = x1
    u2 = x1 * (1.0 + mod_ref[4:5, :]) + mod_ref[3:4, :]
    u2_ref[...] = u2
    logits = lax.dot_general(wr_ref[...], u2, (((1,), (1,)), ((), ())), preferred_element_type=F32,
                             precision=lax.Precision.HIGHEST)
    aff_all = jax.nn.sigmoid(logits)
    sel_all = aff_all + rb_ref[...]
    aff = [aff_all[e:e + 1, :] for e in range(N_EXPERTS)]
    sel = [sel_all[e:e + 1, :] for e in range(N_EXPERTS)]
    gsum = []
    for g in range(N_EXPERT_GROUPS):
        a, b, c, d = sel[g * PER_GROUP:(g + 1) * PER_GROUP]
        hi1, lo1, hi2, lo2 = jnp.maximum(a, b), jnp.minimum(a, b), jnp.maximum(c, d), jnp.minimum(c, d)
        gsum.append(jnp.maximum(hi1, hi2) + jnp.maximum(jnp.minimum(hi1, hi2), jnp.maximum(lo1, lo2)))
    best_g = jnp.zeros_like(gsum[0], dtype=jnp.int32)
    best_v = gsum[0]
    for g in range(1, N_EXPERT_GROUPS):
        better = gsum[g] > best_v
        best_g = jnp.where(better, g, best_g)
        best_v = jnp.where(better, gsum[g], best_v)
    s_loc, a_loc = [], []
    for j in range(PER_GROUP):
        sj, aj = sel[j], aff[j]
        for g in range(1, N_EXPERT_GROUPS):
            pick = best_g == g
            sj = jnp.where(pick, sel[g * PER_GROUP + j], sj)
            aj = jnp.where(pick, aff[g * PER_GROUP + j], aj)
        s_loc.append(sj)
        a_loc.append(aj)
    i1 = jnp.zeros_like(best_g)
    v1, g1 = s_loc[0], a_loc[0]
    for j in range(1, PER_GROUP):
        better = s_loc[j] > v1
        i1 = jnp.where(better, j, i1)
        v1 = jnp.where(better, s_loc[j], v1)
        g1 = jnp.where(better, a_loc[j], g1)
    i2 = jnp.full_like(best_g, -1)
    v2 = jnp.full_like(v1, -jnp.inf)
    g2 = jnp.zeros_like(g1)
    for j in range(PER_GROUP):
        better = (i1 != j) & ((s_loc[j] > v2) | (i2 < 0))
        i2 = jnp.where(better, j, i2)
        v2 = jnp.where(better, s_loc[j], v2)
        g2 = jnp.where(better, a_loc[j], g2)
    total = g1 + g2
    tm = x_ref.shape[0]
    eid_ref[...] = jnp.concatenate([best_g * PER_GROUP + i1, best_g * PER_GROUP + i2,
                                    jnp.zeros((MOD_ROWS - TOP_K, tm), jnp.int32)], axis=0)
    gate_ref[...] = jnp.concatenate([g1 / total, g2 / total, jnp.zeros((MOD_ROWS - TOP_K, tm), F32)], axis=0)


def _post(oa, ob, oc, xa, mod, w_out, ln_g, ln_b, wr_t, rb, n_lat, alpha):
    bsz, s_tot, d = xa.shape
    tm = TOK_TILE
    n_lat_tiles = n_lat // tm
    n_tiles = s_tot // tm
    sd = jax.ShapeDtypeStruct

    def tmajor(width):
        return pl.BlockSpec((None, tm, width), lambda b, i: (b, i, 0))

    def const(shape):
        return pl.BlockSpec(shape, lambda b, i: (0,) * len(shape))

    lanes = pl.BlockSpec((MOD_ROWS, tm), lambda b, i: (0, b * n_tiles + i))
    return pl.pallas_call(
        functools.partial(_post_kernel, alpha=alpha),
        grid=(bsz, n_tiles),
        in_specs=[tmajor(QA_W), tmajor(QB_W), tmajor(QC_W), tmajor(d),
                  pl.BlockSpec((None, N_MOD, d), lambda b, i: (jnp.where(i >= n_lat_tiles, bsz, b), 0, 0)),
                  const((d, d)), const((1, d)), const((1, d)), const((N_EXPERTS, d)), const((N_EXPERTS, 1))],
        out_specs=[tmajor(d), tmajor(d), lanes, lanes],
        out_shape=[sd((bsz, s_tot, d), F32), sd((bsz, s_tot, d), F32),
                   sd((MOD_ROWS, bsz * s_tot), jnp.int32), sd((MOD_ROWS, bsz * s_tot), F32)],
        compiler_params=_params(("parallel", "parallel")),
    )(oa, ob, oc, xa, mod, w_out, ln_g, ln_b, wr_t, rb)


def _ffn_kernel(bexp_ref, nused_ref, tok_ref, u_hbm, w1_ref, w3_ref, w2_ref, y_ref, xbuf, sem):
    del bexp_ref
    n_rows = xbuf.shape[0]

    def row_copy(j):
        return pltpu.make_async_copy(u_hbm.at[pl.ds(tok_ref[0, 0, j], 1), :], xbuf.at[pl.ds(j, 1), :], sem)

    @pl.when(pl.program_id(0) < nused_ref[0])
    def _():
        def issue(j, carry):
            row_copy(j).start()
            return carry

        def drain(j, carry):
            row_copy(j).wait()
            return carry

        lax.fori_loop(0, n_rows, issue, 0)
        lax.fori_loop(0, n_rows, drain, 0)
        xb = xbuf[...].astype(BF16)
        h1 = jnp.dot(xb, w1_ref[...], preferred_element_type=F32)
        h3 = jnp.dot(xb, w3_ref[...], preferred_element_type=F32)
        hid = (h1 * jax.nn.sigmoid(h1) * h3).astype(BF16)
        y_ref[...] = jnp.dot(hid, w2_ref[...], preferred_element_type=F32)

    @pl.when(pl.program_id(0) >= nused_ref[0])
    def _():
        y_ref[...] = jnp.zeros_like(y_ref)


def _expert_ffn(block_expert, n_used, slot_tok, u2, w1, w3, w2):
    n_tok, d = u2.shape
    n_blocks = block_expert.shape[0]
    blk = EXPERT_BLOCK
    grid_spec = pltpu.PrefetchScalarGridSpec(
        num_scalar_prefetch=2,
        grid=(n_blocks,),
        in_specs=[pl.BlockSpec((1, 1, blk), lambda i, be, nu: (i, 0, 0), memory_space=pltpu.SMEM),
                  pl.BlockSpec(memory_space=pl.ANY),
                  pl.BlockSpec((None, d, D_EXPERT), lambda i, be, nu: (be[i], 0, 0)),
                  pl.BlockSpec((None, d, D_EXPERT), lambda i, be, nu: (be[i], 0, 0)),
                  pl.BlockSpec((None, D_EXPERT, d), lambda i, be, nu: (be[i], 0, 0))],
        out_specs=pl.BlockSpec((blk, d), lambda i, be, nu: (i, 0)),
        scratch_shapes=[pltpu.VMEM((blk, d), F32), pltpu.SemaphoreType.DMA(())],
    )
    return pl.pallas_call(
        _ffn_kernel,
        grid_spec=grid_spec,
        out_shape=jax.ShapeDtypeStruct((n_blocks * blk, d), F32),
        compiler_params=_params(("arbitrary",)),
    )(block_expert, n_used, slot_tok.reshape(n_blocks, 1, blk), u2, w1, w3, w2)


def _comb_kernel(pos_ref, y_hbm, x1_ref, gate_ref, mod_ref, g_ref, b_ref, o_ref, ybuf, sem, *, alpha):
    tm = x1_ref.shape[0]

    def row_copy(j):
        return pltpu.make_async_copy(y_hbm.at[pl.ds(pos_ref[0, 0, j], 1), :], ybuf.at[pl.ds(j, 1), :], sem)

    def issue(j, carry):
        row_copy(j).start()
        return carry

    def drain(j, carry):
        row_copy(j).wait()
        return carry

    lax.fori_loop(0, TOP_K * tm, issue, 0)
    lax.fori_loop(0, TOP_K * tm, drain, 0)
    f = gate_ref[:, 0:1] * ybuf[0:tm, :] + gate_ref[:, 1:2] * ybuf[tm:2 * tm, :]
    o_ref[...] = _layer_norm(alpha * x1_ref[...] + (1.0 + mod_ref[5:6, :]) * f, g_ref[...], b_ref[...])


def _combine(pos, y, x1, gates, mod, ln_g, ln_b, n_lat, alpha):
    bsz, s_tot, d = x1.shape
    tm = TOK_TILE
    n_lat_tiles = n_lat // tm
    n_tiles = s_tot // tm
    return pl.pallas_call(
        functools.partial(_comb_kernel, alpha=alpha),
        grid=(bsz, n_tiles),
        in_specs=[pl.BlockSpec((1, 1, TOP_K * tm), lambda b, i: (b * n_tiles + i, 0, 0), memory_space=pltpu.SMEM),
                  pl.BlockSpec(memory_space=pl.ANY),
                  pl.BlockSpec((None, tm, d), lambda b, i: (b, i, 0)),
                  pl.BlockSpec((tm, TOP_K), lambda b, i: (b * n_tiles + i, 0)),
                  pl.BlockSpec((None, N_MOD, d), lambda b, i: (jnp.where(i >= n_lat_tiles, bsz, b), 0, 0)),
                  pl.BlockSpec((1, d), lambda b, i: (0, 0)),
                  pl.BlockSpec((1, d), lambda b, i: (0, 0))],
        out_specs=pl.BlockSpec((None, tm, d), lambda b, i: (b, i, 0)),
        out_shape=jax.ShapeDtypeStruct((bsz, s_tot, d), F32),
        scratch_shapes=[pltpu.VMEM((TOP_K * tm, d), F32), pltpu.SemaphoreType.DMA(())],
        compiler_params=_params(("arbitrary", "arbitrary")),
    )(pos, y, x1, gates, mod, ln_g, ln_b)


def _deinterleave(n_heads):
    one = np.concatenate([np.arange(0, HEAD_DIM, 2), np.arange(1, HEAD_DIM, 2)])
    return np.concatenate([h * HEAD_DIM + one for h in range(n_heads)])


def _proj_weight(w_in_l):
    sizes = (QA_W, KA_W, KA_W, QB_W, KB_W, KB_W, QC_W, KC_W, KC_W)
    offs = np.concatenate([[0], np.cumsum(sizes)])
    qa, ka, va, qb, kb, vb, qc, kc, vc = [w_in_l[:, offs[i]:offs[i + 1]] for i in range(len(sizes))]
    scale = HEAD_DIM ** -0.5
    qa = qa[:, _deinterleave(A_HEADS)] * scale
    ka = ka[:, _deinterleave(A_KV)]
    qb = qb * scale
    qc = qc[:, _deinterleave(C_HEADS)]
    kc = kc[:, _deinterleave(C_KV)]
    return jnp.concatenate([qa, qb, qc, ka, kb, kc, va, vb, vc], axis=1).T.astype(BF16)


def _rope_tables(n_lat, n_ctx):
    t = np.arange(n_lat)
    inv_freq = ROPE_BASE ** (-np.arange(0, HALF, 2, dtype=np.float32) / HALF)
    ang = jnp.concatenate([jnp.asarray(t // GRID_W, F32)[None, :] * jnp.asarray(inv_freq)[:, None],
                           jnp.asarray(t % GRID_W, F32)[None, :] * jnp.asarray(inv_freq)[:, None]], axis=0)
    cos = jnp.concatenate([jnp.cos(ang), jnp.ones((HALF, n_ctx), F32)], axis=1)
    sin = jnp.concatenate([jnp.sin(ang), jnp.zeros((HALF, n_ctx), F32)], axis=1)
    return cos, sin


def _dispatch(eid, n_tok):
    n_assign = n_tok * TOP_K
    flat_e = eid.T.reshape(-1)
    onehot = (flat_e[:, None] == jnp.arange(N_EXPERTS, dtype=jnp.int32)[None, :]).astype(jnp.int32)
    csum = jnp.cumsum(onehot, axis=0)
    rank = jnp.take_along_axis(csum, flat_e[:, None], axis=1)[:, 0] - 1
    counts = csum[-1]
    padded = (counts + EXPERT_BLOCK - 1) // EXPERT_BLOCK * EXPERT_BLOCK
    pad_ends = jnp.cumsum(padded)
    dest = (pad_ends - padded)[flat_e] + rank
    n_blocks = -(-n_assign // EXPERT_BLOCK) + N_EXPERTS
    tok = jnp.arange(n_assign, dtype=jnp.int32) // TOP_K
    slot_tok = jnp.zeros((n_blocks * EXPERT_BLOCK,), jnp.int32).at[dest].set(tok)
    block_expert = jnp.minimum(
        jnp.searchsorted(pad_ends, jnp.arange(n_blocks, dtype=jnp.int32) * EXPERT_BLOCK, side='right'),
        N_EXPERTS - 1).astype(jnp.int32)
    n_used = (pad_ends[-1] // EXPERT_BLOCK).astype(jnp.int32).reshape(1)
    return slot_tok, block_expert, dest.reshape(n_tok, TOP_K).astype(jnp.int32), n_used


def kernel(x, c, ctx, c_ctx, w_in, w_out, sink, rpb, q_gain, k_gain, w_ada, b_ada,
           ln1_g, ln1_b, ln2_g, ln2_b, w_router, router_bias, w1, w3, w2):
    bsz, n_lat, d = x.shape
    n_ctx = ctx.shape[1]
    s_tot = n_lat + n_ctx
    depth = w_in.shape[0]
    assert d == D_MODEL and n_ctx == TOK_TILE and n_lat % min(TK_C, n_lat) == 0
    assert n_lat % (2 * TOK_TILE) == 0 and bsz + 1 <= MOD_ROWS
    alpha = float((2 * depth) ** 0.25)
    n_tok = bsz * s_tot
    tm = TOK_TILE

    cond = jnp.zeros((MOD_ROWS, d), F32).at[:bsz].set(c).at[bsz].set(c_ctx)
    mods = _ada_table(cond, w_ada, b_ada).reshape(depth, MOD_ROWS, N_MOD, d)
    cos_t, sin_t = _rope_tables(n_lat, n_ctx)
    gain_perm = _deinterleave(1)
    wr_t = w_router.T
    rb = router_bias.reshape(N_EXPERTS, 1).astype(F32)

    xa = jnp.concatenate([x, ctx], axis=1)
    for l in range(depth):
        mod = mods[l]
        qa, qb, qc, ka, kb, kc, va, vb, vc = _project(
            xa, mod, _proj_weight(w_in[l]), cos_t, sin_t,
            q_gain[l][gain_perm].reshape(HEAD_DIM, 1) * (HEAD_DIM ** -0.5),
            k_gain[l][gain_perm].reshape(HEAD_DIM, 1), n_lat)
        oa = _attn_a(sink[l], qa, ka, va, n_lat)
        bias_tiles, starts, variants, n_keys = _nb_bias_tiles(rpb[l], n_lat, s_tot)
        ob = _attn_b(qb, kb, vb, bias_tiles, starts, variants, n_keys, n_lat)
        oc = _attn_c(qc, kc, vc, n_lat)
        x1, u2, eid8, gate8 = _post(oa, ob, oc, xa, mod, w_out[l].astype(BF16),
                                    ln1_g[l].reshape(1, d), ln1_b[l].reshape(1, d), wr_t, rb, n_lat, alpha)
        slot_tok, block_expert, dest, n_used = _dispatch(eid8[:TOP_K], n_tok)
        y = _expert_ffn(block_expert, n_used, slot_tok, u2.reshape(n_tok, d),
                        w1[l].astype(BF16), w3[l].astype(BF16), w2[l].astype(BF16))
        pos = dest.reshape(n_tok // tm, tm, TOP_K).transpose(0, 2, 1).reshape(n_tok // tm, 1, TOP_K * tm)
        xa = _combine(pos, y, x1, gate8[:TOP_K].T, mod, ln2_g[l].reshape(1, d), ln2_b[l].reshape(1, d),
                      n_lat, alpha)
    return xa[:, :n_lat]
```

```python
import functools

import numpy as np
import jax
import jax.numpy as jnp
from jax import lax
from jax.experimental import pallas as pl
from jax.experimental.pallas import tpu as pltpu

F32 = jnp.float32
BF16 = jnp.bfloat16

D_MODEL = 1024
HEAD_DIM = 64
HALF = HEAD_DIM // 2
GRID_W = 64
A_HEADS, A_KV = 6, 2
WINDOW = 128
B_HEADS = 4
NA_ROWS, NA_COLS = 8, 16
C_HEADS, C_KV = 6, 2
GQA_GROUP = 3
ROPE_BASE = 10000.0
N_EXPERTS = 16
N_EXPERT_GROUPS = 4
PER_GROUP = N_EXPERTS // N_EXPERT_GROUPS
TOP_K = 2
D_EXPERT = 512
EXPERT_BLOCK = 512
N_MOD = 6
LN_EPS = 1e-6
QK_EPS = 1e-6
NEG_INF = -1e30

QA_W, QB_W, QC_W = A_HEADS * HEAD_DIM, B_HEADS * HEAD_DIM, C_HEADS * HEAD_DIM
KA_W, KB_W, KC_W = A_KV * HEAD_DIM, B_HEADS * HEAD_DIM, C_KV * HEAD_DIM
R_QA = 0
R_QB = R_QA + QA_W
R_QC = R_QB + QB_W
R_KA = R_QC + QC_W
R_KB = R_KA + KA_W
R_KC = R_KB + KB_W
R_VA = R_KC + KC_W
R_VB = R_VA + KA_W
R_VC = R_VB + KB_W
PROJ_WIDTH = R_VC + KC_W

V7X_VMEM_LIMIT = 56 * 1024 * 1024
TOK_TILE = 256
TQ_A = 256
TQ_B = 2 * GRID_W
TQ_C = 256
TK_C = 2048
MOD_ROWS = 8
LOG2_E = 1.4426950408889634
MAX_SCORE_BOUND = 60.0
BOUND_SLACK = 1.004
TILE_ROWS = 8


def _params(sem):
    return pltpu.CompilerParams(dimension_semantics=sem, vmem_limit_bytes=V7X_VMEM_LIMIT)


def _ada_kernel(c_ref, w_ref, b_ref, o_ref):
    c = c_ref[...]
    act = c * jax.nn.sigmoid(c)
    o_ref[...] = jnp.dot(act, w_ref[...], preferred_element_type=F32,
                         precision=lax.Precision.HIGHEST) + b_ref[...]


def _ada_table(cond, w_ada, b_ada):
    depth, d, n = w_ada.shape
    tn = n // 4
    return pl.pallas_call(
        _ada_kernel,
        grid=(depth, n // tn),
        in_specs=[pl.BlockSpec((MOD_ROWS, d), lambda l, j: (0, 0)),
                  pl.BlockSpec((None, d, tn), lambda l, j: (l, 0, j)),
                  pl.BlockSpec((None, 1, tn), lambda l, j: (l, 0, j))],
        out_specs=pl.BlockSpec((None, MOD_ROWS, tn), lambda l, j: (l, 0, j)),
        out_shape=jax.ShapeDtypeStruct((depth, MOD_ROWS, n), F32),
        compiler_params=_params(("parallel", "parallel")),
    )(cond, w_ada, b_ada.reshape(depth, 1, n))


def _sq_norm(parts):
    return sum(jnp.sum(jnp.square(p.astype(F32)), axis=0, keepdims=True) for p in parts)


def _proj_kernel(x_ref, mod_ref, w_ref, cos_ref, sin_ref, qg_ref, kg_ref,
                 qa_ref, qb_ref, qc_ref, ka_ref, kb_ref, kc_ref, va_ref, vb_ref, vc_ref, qn_ref, kn_ref):
    tm = x_ref.shape[0]
    u = (x_ref[...] * (1.0 + mod_ref[1:2, :]) + mod_ref[0:1, :]).astype(BF16)

    def proj(lo, width):
        return lax.dot_general(w_ref[lo:lo + width, :], u, (((1,), (1,)), ((), ())),
                               preferred_element_type=F32)

    cos = cos_ref[...]
    sin = sin_ref[...]

    def rope(blk):
        x0, x1 = blk[:HALF], blk[HALF:]
        return x0 * cos - x1 * sin, x0 * sin + x1 * cos

    def qk_norm(blk, gain):
        ms = jnp.mean(blk * blk, axis=0, keepdims=True)
        return blk * lax.rsqrt(ms + QK_EPS) * gain

    p = proj(R_QA, QA_W)
    for h in range(A_HEADS):
        o0, o1 = rope(p[h * HEAD_DIM:(h + 1) * HEAD_DIM])
        qa_ref[h * HEAD_DIM:h * HEAD_DIM + HALF, :] = o0.astype(BF16)
        qa_ref[h * HEAD_DIM + HALF:(h + 1) * HEAD_DIM, :] = o1.astype(BF16)
    qb_ref[...] = proj(R_QB, QB_W).astype(BF16)
    p = proj(R_QC, QC_W)
    qg = qg_ref[...]
    for h in range(C_HEADS):
        o0, o1 = [o.astype(BF16) for o in rope(qk_norm(p[h * HEAD_DIM:(h + 1) * HEAD_DIM], qg))]
        qc_ref[h * HEAD_DIM:h * HEAD_DIM + HALF, :] = o0
        qc_ref[h * HEAD_DIM + HALF:(h + 1) * HEAD_DIM, :] = o1
        qn_ref[h:h + 1, :] = _sq_norm([o0, o1])
    qn_ref[C_HEADS:MOD_ROWS, :] = jnp.zeros((MOD_ROWS - C_HEADS, tm), F32)
    p = proj(R_KA, KA_W)
    parts = []
    for h in range(A_KV):
        parts.extend(rope(p[h * HEAD_DIM:(h + 1) * HEAD_DIM]))
    ka_ref[...] = jnp.concatenate(parts, axis=0).T.astype(BF16)
    kb_ref[...] = proj(R_KB, KB_W).T.astype(BF16)
    p = proj(R_KC, KC_W)
    kg = kg_ref[...]
    parts = []
    for h in range(C_KV):
        o0, o1 = [o.astype(BF16) for o in rope(qk_norm(p[h * HEAD_DIM:(h + 1) * HEAD_DIM], kg))]
        kn_ref[h:h + 1, :] = _sq_norm([o0, o1])
        parts.extend([o0.astype(F32), o1.astype(F32)])
    kn_ref[C_KV:MOD_ROWS, :] = jnp.zeros((MOD_ROWS - C_KV, tm), F32)
    kc_ref[:, 0:KC_W] = jnp.concatenate(parts, axis=0).T.astype(BF16)
    kc_ref[:, KC_W:2 * KC_W] = (lax.broadcasted_iota(jnp.int32, (tm, KC_W), 1) == 0).astype(BF16)
    ones_row = (lax.broadcasted_iota(jnp.int32, (HEAD_DIM, tm), 0) == 0).astype(BF16)
    for lo, ref in ((R_VA, va_ref), (R_VC, vc_ref)):
        p = proj(lo, KA_W)
        for g in range(A_KV):
            ref[g, 0:HEAD_DIM, :] = p[g * HEAD_DIM:(g + 1) * HEAD_DIM].astype(BF16)
            ref[g, HEAD_DIM:2 * HEAD_DIM, :] = ones_row
    vb_ref[...] = proj(R_VB, KB_W).astype(BF16)


def _project(xa, mod, w_t, cos_t, sin_t, q_gain, k_gain, n_lat):
    bsz, s_tot, d = xa.shape
    tm = TOK_TILE
    n_lat_tiles = n_lat // tm
    sd = jax.ShapeDtypeStruct

    def dmajor(width):
        return pl.BlockSpec((None, width, tm), lambda b, i: (b, 0, i))

    def tmajor(width):
        return pl.BlockSpec((None, tm, width), lambda b, i: (b, i, 0))

    aug = pl.BlockSpec((None, A_KV, 2 * HEAD_DIM, tm), lambda b, i: (b, 0, 0, i))
    return pl.pallas_call(
        _proj_kernel,
        grid=(bsz, s_tot // tm),
        in_specs=[tmajor(d),
                  pl.BlockSpec((None, N_MOD, d), lambda b, i: (jnp.where(i >= n_lat_tiles, bsz, b), 0, 0)),
                  pl.BlockSpec((PROJ_WIDTH, d), lambda b, i: (0, 0)),
                  pl.BlockSpec((HALF, tm), lambda b, i: (0, i)),
                  pl.BlockSpec((HALF, tm), lambda b, i: (0, i)),
                  pl.BlockSpec((HEAD_DIM, 1), lambda b, i: (0, 0)),
                  pl.BlockSpec((HEAD_DIM, 1), lambda b, i: (0, 0))],
        out_specs=[dmajor(QA_W), dmajor(QB_W), dmajor(QC_W),
                   tmajor(KA_W), tmajor(KB_W), tmajor(2 * KC_W),
                   aug, dmajor(KB_W), aug, dmajor(MOD_ROWS), dmajor(MOD_ROWS)],
        out_shape=[sd((bsz, QA_W, s_tot), BF16), sd((bsz, QB_W, s_tot), BF16), sd((bsz, QC_W, s_tot), BF16),
                   sd((bsz, s_tot, KA_W), BF16), sd((bsz, s_tot, KB_W), BF16), sd((bsz, s_tot, 2 * KC_W), BF16),
                   sd((bsz, A_KV, 2 * HEAD_DIM, s_tot), BF16), sd((bsz, KB_W, s_tot), BF16),
                   sd((bsz, A_KV, 2 * HEAD_DIM, s_tot), BF16),
                   sd((bsz, MOD_ROWS, s_tot), F32), sd((bsz, MOD_ROWS, s_tot), F32)],
        compiler_params=_params(("parallel", "parallel")),
    )(xa, mod, w_t, cos_t, sin_t, q_gain, k_gain)


def _padded_q(q_ref, h, g):
    blk = q_ref[h * HEAD_DIM:(h + 1) * HEAD_DIM, :]
    zeros = jnp.zeros_like(blk)
    return jnp.concatenate([blk, zeros] if g == 0 else [zeros, blk], axis=0)


def _attn_a_kernel(sink_ref, q_ref, k_ref, v_ref, o_ref, ot_ref, *, n_lat, n_ctx):
    tq = q_ref.shape[1]
    span = tq + 2 * WINDOW
    start = pl.program_id(1) * tq
    ks = pl.multiple_of(jnp.clip(start - WINDOW, 0, n_lat - span), 128)
    kpos = ks + lax.broadcasted_iota(jnp.int32, (span, tq), 0)
    qpos = start + lax.broadcasted_iota(jnp.int32, (span, tq), 1)
    ok = (jnp.abs(qpos - kpos) <= WINDOW) & (qpos < n_lat)
    k_loc = k_ref[pl.ds(ks, span), :]
    k_ctx = k_ref[n_lat:n_lat + n_ctx, :]
    for g in range(A_KV):
        v_loc = v_ref[g, :, pl.ds(ks, span)]
        v_ctx = v_ref[g, :, n_lat:n_lat + n_ctx]
        for hh in range(GQA_GROUP):
            h = g * GQA_GROUP + hh
            qp = _padded_q(q_ref, h, g)
            s_loc = jnp.where(ok, jnp.dot(k_loc, qp, preferred_element_type=F32), NEG_INF)
            s_ctx = jnp.dot(k_ctx, qp, preferred_element_type=F32)
            sink = sink_ref[h]
            m = jnp.maximum(jnp.maximum(jnp.max(s_loc, axis=0, keepdims=True),
                                        jnp.max(s_ctx, axis=0, keepdims=True)), sink)
            p_loc = jnp.exp(s_loc - m).astype(BF16)
            p_ctx = jnp.exp(s_ctx - m).astype(BF16)
            acc = (jnp.dot(v_loc, p_loc, preferred_element_type=F32)
                   + jnp.dot(v_ctx, p_ctx, preferred_element_type=F32))
            denom = acc[HEAD_DIM:HEAD_DIM + 1, :] + jnp.exp(sink - m)
            ot_ref[h * HEAD_DIM:(h + 1) * HEAD_DIM, :] = acc[0:HEAD_DIM, :] / denom
    o_ref[...] = ot_ref[...].T.astype(BF16)


def _attn_a(sink, q_t, k, v_t, n_lat):
    bsz, s_tot, _ = k.shape
    tq = TQ_A
    kern = functools.partial(_attn_a_kernel, n_lat=n_lat, n_ctx=s_tot - n_lat)
    return pl.pallas_call(
        kern,
        grid=(bsz, s_tot // tq),
        in_specs=[pl.BlockSpec(memory_space=pltpu.SMEM),
                  pl.BlockSpec((None, QA_W, tq), lambda b, i: (b, 0, i)),
                  pl.BlockSpec((None, s_tot, KA_W), lambda b, i: (b, 0, 0)),
                  pl.BlockSpec((None, A_KV, 2 * HEAD_DIM, s_tot), lambda b, i: (b, 0, 0, 0))],
        out_specs=pl.BlockSpec((None, tq, QA_W), lambda b, i: (b, i, 0)),
        out_shape=jax.ShapeDtypeStruct((bsz, s_tot, QA_W), BF16),
        scratch_shapes=[pltpu.VMEM((QA_W, tq), F32)],
        compiler_params=_params(("parallel", "arbitrary")),
    )(sink, q_t, k, v_t)


def _nb_plan(n_lat, s_tot):
    rows = n_lat // GRID_W
    kh = min(NA_ROWS, rows)
    q_rows = TQ_B // GRID_W
    w_rows = kh + q_rows
    n_lat_blocks = n_lat // TQ_B
    n_blocks = s_tot // TQ_B
    starts, variants, keys, reps = [], [], {}, []
    for i in range(n_lat_blocks):
        w0 = int(np.clip(q_rows * i - kh // 2, 0, rows - w_rows))
        r0s = tuple(int(np.clip(q_rows * i + j - kh // 2, 0, rows - kh)) - w0 for j in range(q_rows))
        key = (q_rows * i - w0,) + r0s
        if key not in keys:
            keys[key] = len(reps)
            reps.append((i, w0))
        starts.append(w0 * GRID_W)
        variants.append(keys[key])
    masked = len(reps)
    starts.extend([0] * (n_blocks - n_lat_blocks))
    variants.extend([masked] * (n_blocks - n_lat_blocks))
    return rows, kh, w_rows, reps, np.asarray(starts, np.int32), np.asarray(variants, np.int32)


def _rel_bias_kernel(r_ref, e_ref, m_ref, o_ref):
    o_ref[...] = jnp.dot(r_ref[...], e_ref[...], preferred_element_type=F32,
                         precision=lax.Precision.HIGHEST) + m_ref[...]


def _rel_bias_blocks(rpb):
    depth, heads, n_dr, n_dc = rpb.shape
    kc = np.arange(GRID_W)[:, None]
    cq = np.arange(GRID_W)[None, :]
    dc = (kc - cq + NA_COLS - 1).reshape(-1)
    n_dc_pad = 32
    onehot = (np.arange(n_dc_pad)[:, None] == dc[None, :]).astype(np.float32)
    c0 = np.clip(cq - NA_COLS // 2, 0, GRID_W - NA_COLS)
    outside = ~((kc >= c0) & (kc < c0 + NA_COLS))
    col_mask = np.where(outside, NEG_INF, 0.0).astype(np.float32).reshape(1, -1)
    table = jnp.pad(rpb.astype(F32).reshape(depth * heads * n_dr, n_dc), ((0, 0), (0, n_dc_pad - n_dc)))
    blocks = pl.pallas_call(
        _rel_bias_kernel,
        out_shape=jax.ShapeDtypeStruct((depth * heads * n_dr, GRID_W * GRID_W), F32),
    )(table, jnp.asarray(onehot), jnp.asarray(col_mask))
    blocks = blocks.reshape(depth, heads, n_dr, GRID_W, GRID_W)
    return jnp.concatenate([blocks, jnp.full((depth, heads, 1, GRID_W, GRID_W), NEG_INF, F32)], axis=2)


def _nb_bias_tiles(rpb, n_lat, s_tot):
    rows, kh, w_rows, reps, starts, variants = _nb_plan(n_lat, s_tot)
    q_rows = TQ_B // GRID_W
    masked_block = 2 * NA_ROWS - 1
    dr_map = np.full((len(reps) + 1, w_rows, q_rows), masked_block, np.int32)
    for v, (i, w0) in enumerate(reps):
        for a in range(w_rows):
            for b in range(q_rows):
                r = q_rows * i + b
                r0 = int(np.clip(r - kh // 2, 0, rows - kh))
                if r0 <= w0 + a < r0 + kh:
                    dr_map[v, a, b] = w0 + a - r + NA_ROWS - 1
    blocks = _rel_bias_blocks(rpb)[:, :, dr_map]
    depth = rpb.shape[0]
    tiles = jnp.transpose(blocks, (0, 2, 3, 5, 1, 4, 6)).reshape(
        depth, len(reps) + 1, w_rows * GRID_W, B_HEADS * TQ_B)
    return tiles, jnp.asarray(starts), jnp.asarray(variants), w_rows * GRID_W


def _attn_b_kernel(start_ref, var_ref, q_ref, k_ref, v_ref, bias_ref, o_ref, qp_ref, ot_ref, *, n_lat, n_ctx, n_keys):
    tq = q_ref.shape[1]
    ws = pl.multiple_of(start_ref[pl.program_id(1)], 128)
    qp_ref[...] = jnp.zeros_like(qp_ref)
    for h in range(B_HEADS):
        qp_ref[h * HEAD_DIM:(h + 1) * HEAD_DIM, h * tq:(h + 1) * tq] = q_ref[h * HEAD_DIM:(h + 1) * HEAD_DIM, :]
    qp = qp_ref[...]
    s_loc = jnp.dot(k_ref[pl.ds(ws, n_keys), :], qp, preferred_element_type=F32) + bias_ref[...]
    s_ctx = jnp.dot(k_ref[n_lat:n_lat + n_ctx, :], qp, preferred_element_type=F32)
    m = jnp.maximum(jnp.max(s_loc, axis=0, keepdims=True), jnp.max(s_ctx, axis=0, keepdims=True))
    p_loc = jnp.exp(s_loc - m)
    p_ctx = jnp.exp(s_ctx - m)
    denom = jnp.sum(p_loc, axis=0, keepdims=True) + jnp.sum(p_ctx, axis=0, keepdims=True)
    acc = (jnp.dot(v_ref[:, pl.ds(ws, n_keys)], p_loc.astype(BF16), preferred_element_type=F32)
           + jnp.dot(v_ref[:, n_lat:n_lat + n_ctx], p_ctx.astype(BF16), preferred_element_type=F32))
    for h in range(B_HEADS):
        ot_ref[h * HEAD_DIM:(h + 1) * HEAD_DIM, :] = (
            acc[h * HEAD_DIM:(h + 1) * HEAD_DIM, h * tq:(h + 1) * tq] / denom[:, h * tq:(h + 1) * tq])
    o_ref[...] = ot_ref[...].T.astype(BF16)


def _attn_b(q_t, k, v_t, bias_tiles, starts, variants, n_keys, n_lat):
    bsz, s_tot, _ = k.shape
    tq = TQ_B
    kern = functools.partial(_attn_b_kernel, n_lat=n_lat, n_ctx=s_tot - n_lat, n_keys=n_keys)
    grid_spec = pltpu.PrefetchScalarGridSpec(
        num_scalar_prefetch=2,
        grid=(bsz, s_tot // tq),
        in_specs=[pl.BlockSpec((None, QB_W, tq), lambda b, i, st, va: (b, 0, i)),
                  pl.BlockSpec((None, s_tot, KB_W), lambda b, i, st, va: (b, 0, 0)),
                  pl.BlockSpec((None, KB_W, s_tot), lambda b, i, st, va: (b, 0, 0)),
                  pl.BlockSpec((None, n_keys, B_HEADS * tq), lambda b, i, st, va: (va[i], 0, 0))],
        out_specs=pl.BlockSpec((None, tq, QB_W), lambda b, i, st, va: (b, i, 0)),
        scratch_shapes=[pltpu.VMEM((QB_W, B_HEADS * tq), BF16), pltpu.VMEM((QB_W, tq), F32)],
    )
    return pl.pallas_call(
        kern,
        grid_spec=grid_spec,
        out_shape=jax.ShapeDtypeStruct((bsz, s_tot, QB_W), BF16),
        compiler_params=_params(("parallel", "arbitrary")),
    )(starts, variants, q_t, k, v_t, bias_tiles)


def _attn_c_kernel(fast_ref, ksc_ref, q_ref, qn_ref, k_ref, v_ref, o_ref, qp_ref, m_ref, acc_ref, ot_ref,
                   *, n_lat, n_ctx, tk):
    tq = q_ref.shape[1]
    is_ctx = pl.program_id(1) >= n_lat // tq
    n_chunks = jnp.where(is_ctx, 0, n_lat // tk)
    fast = fast_ref[0] == 1
    first_row = lax.broadcasted_iota(jnp.int32, (KC_W, tq), 0) == 0
    for g in range(C_KV):
        k_scale = ksc_ref[pl.program_id(0) * C_KV + g]
        for hh in range(GQA_GROUP):
            h = g * GQA_GROUP + hh
            qp_ref[0:KC_W, hh * tq:(hh + 1) * tq] = _padded_q(q_ref, h, g)
            offset = jnp.where(fast, -jnp.sqrt(qn_ref[h:h + 1, :]) * k_scale, 0.0)
            qp_ref[KC_W:2 * KC_W, hh * tq:(hh + 1) * tq] = jnp.where(first_row, offset, 0.0).astype(BF16)
        acc_ref[...] = jnp.zeros_like(acc_ref)

        def bounded_step(start, size):
            s = jnp.dot(k_ref[pl.ds(start, size), :], qp_ref[...], preferred_element_type=F32)
            p = jnp.exp2(s).astype(BF16)
            acc_ref[...] += jnp.dot(v_ref[g, :, pl.ds(start, size)], p, preferred_element_type=F32)

        def online_step(start, size):
            s = jnp.dot(k_ref[pl.ds(start, size), :], qp_ref[...], preferred_element_type=F32)
            m_prev = m_ref[...]
            m_new = jnp.maximum(m_prev, jnp.max(s, axis=0, keepdims=True))
            p = jnp.exp2(s - m_new).astype(BF16)
            pv = jnp.dot(v_ref[g, :, pl.ds(start, size)], p, preferred_element_type=F32)
            acc_ref[...] = acc_ref[...] * jnp.exp2(m_prev - m_new) + pv
            m_ref[...] = m_new

        def sweep(step):
            def body(c, carry):
                step(pl.multiple_of(c * tk, tk), tk)
                return carry

            lax.fori_loop(0, n_chunks, body, 0)
            step(n_lat, n_ctx)

        @pl.when(fast)
        def _():
            sweep(bounded_step)

        @pl.when(jnp.logical_not(fast))
        def _():
            m_ref[...] = jnp.full_like(m_ref, NEG_INF)
            sweep(online_step)

        acc = acc_ref[...]
        out = acc[0:HEAD_DIM, :] / acc[HEAD_DIM:HEAD_DIM + 1, :]
        for hh in range(GQA_GROUP):
            h = g * GQA_GROUP + hh
            ot_ref[h * HEAD_DIM:(h + 1) * HEAD_DIM, :] = out[:, hh * tq:(hh + 1) * tq]
    o_ref[...] = ot_ref[...].T.astype(BF16)


def _attn_c(q_t, q_norm2, k, k_norm2, v_t, n_lat):
    bsz, s_tot, _ = k.shape
    tq = TQ_C
    tk = min(TK_C, n_lat)
    k_scale = jnp.sqrt(jnp.max(k_norm2[:, :C_KV, :], axis=-1)) * BOUND_SLACK
    bound = jnp.sqrt(jnp.max(q_norm2)) * jnp.max(k_scale)
    fast = (bound <= MAX_SCORE_BOUND).astype(jnp.int32).reshape(1)
    kern = functools.partial(_attn_c_kernel, n_lat=n_lat, n_ctx=s_tot - n_lat, tk=tk)
    smem = pl.BlockSpec(memory_space=pltpu.SMEM)
    return pl.pallas_call(
        kern,
        grid=(bsz, s_tot // tq),
        in_specs=[smem, smem,
                  pl.BlockSpec((None, QC_W, tq), lambda b, i: (b, 0, i)),
                  pl.BlockSpec((None, MOD_ROWS, tq), lambda b, i: (b, 0, i)),
                  pl.BlockSpec((None, s_tot, 2 * KC_W), lambda b, i: (b, 0, 0)),
                  pl.BlockSpec((None, C_KV, 2 * HEAD_DIM, s_tot), lambda b, i: (b, 0, 0, 0))],
        out_specs=pl.BlockSpec((None, tq, QC_W), lambda b, i: (b, i, 0)),
        out_shape=jax.ShapeDtypeStruct((bsz, s_tot, QC_W), BF16),
        scratch_shapes=[pltpu.VMEM((2 * KC_W, GQA_GROUP * tq), BF16),
                        pltpu.VMEM((1, GQA_GROUP * tq), F32),
                        pltpu.VMEM((2 * HEAD_DIM, GQA_GROUP * tq), F32),
                        pltpu.VMEM((QC_W, tq), F32)],
        compiler_params=_params(("parallel", "arbitrary")),
    )(fast, k_scale.reshape(-1), q_t, q_norm2, k, v_t)


def _store_token_tiles(ref, val):
    n = val.shape[0]
    for c in range(TILE_ROWS):
        ref[pl.ds(c, n, stride=TILE_ROWS), :] = val[:, c * 128:(c + 1) * 128]


def _load_token_tiles(ref, n):
    return jnp.concatenate([ref[pl.ds(c, n, stride=TILE_ROWS), :] for c in range(TILE_ROWS)], axis=1)


def _tile_copy(src_hbm, dst, idx_ref, j, sem):
    src = pl.multiple_of(idx_ref[0, 0, j] * TILE_ROWS, TILE_ROWS)
    return pltpu.make_async_copy(src_hbm.at[pl.ds(src, TILE_ROWS), :],
                                 dst.at[pl.ds(pl.multiple_of(j * TILE_ROWS, TILE_ROWS), TILE_ROWS), :], sem)


def _start_gather(src_hbm, dst, idx_ref, n, sem):
    def issue(j, carry):
        _tile_copy(src_hbm, dst, idx_ref, j, sem).start()
        return carry

    lax.fori_loop(0, n, issue, 0, unroll=8)


def _wait_gather(src_hbm, dst, n, sem):
    pltpu.make_async_copy(src_hbm.at[pl.ds(0, n * TILE_ROWS), :], dst, sem).wait()


def _layer_norm(h, g, b):
    mu = jnp.mean(h, axis=-1, keepdims=True)
    hc = h - mu
    var = jnp.mean(hc * hc, axis=-1, keepdims=True)
    return hc * lax.rsqrt(var + LN_EPS) * g + b


def _post_kernel(oa_ref, ob_ref, oc_ref, x_ref, mod_ref, w_ref, g_ref, b_ref, wr_ref, rb_ref,
                 x1_ref, u2_ref, eid_ref, gate_ref, *, alpha):
    o_cat = jnp.concatenate([oa_ref[...], ob_ref[...], oc_ref[...]], axis=1)
    o = jnp.dot(o_cat, w_ref[...], preferred_element_type=F32)
    x1 = _layer_norm(alpha * x_ref[...] + (1.0 + mod_ref[2:3, :]) * o, g_ref[...], b_ref[...])
    x1_ref[...] = x1
    u2 = x1 * (1.0 + mod_ref[4:5, :]) + mod_ref[3:4, :]
    _store_token_tiles(u2_ref, u2)
    logits = lax.dot_general(wr_ref[...], u2, (((1,), (1,)), ((), ())), preferred_element_type=F32,
                             precision=lax.Precision.HIGHEST)
    aff_all = jax.nn.sigmoid(logits)
    sel_all = aff_all + rb_ref[...]
    aff = [aff_all[e:e + 1, :] for e in range(N_EXPERTS)]
    sel = [sel_all[e:e + 1, :] for e in range(N_EXPERTS)]
    gsum = []
    for g in range(N_EXPERT_GROUPS):
        a, b, c, d = sel[g * PER_GROUP:(g + 1) * PER_GROUP]
        hi1, lo1, hi2, lo2 = jnp.maximum(a, b), jnp.minimum(a, b), jnp.maximum(c, d), jnp.minimum(c, d)
        gsum.append(jnp.maximum(hi1, hi2) + jnp.maximum(jnp.minimum(hi1, hi2), jnp.maximum(lo1, lo2)))
    best_g = jnp.zeros_like(gsum[0], dtype=jnp.int32)
    best_v = gsum[0]
    for g in range(1, N_EXPERT_GROUPS):
        better = gsum[g] > best_v
        best_g = jnp.where(better, g, best_g)
        best_v = jnp.where(better, gsum[g], best_v)
    s_loc, a_loc = [], []
    for j in range(PER_GROUP):
        sj, aj = sel[j], aff[j]
        for g in range(1, N_EXPERT_GROUPS):
            pick = best_g == g
            sj = jnp.where(pick, sel[g * PER_GROUP + j], sj)
            aj = jnp.where(pick, aff[g * PER_GROUP + j], aj)
        s_loc.append(sj)
        a_loc.append(aj)
    i1 = jnp.zeros_like(best_g)
    v1, g1 = s_loc[0], a_loc[0]
    for j in range(1, PER_GROUP):
        better = s_loc[j] > v1
        i1 = jnp.where(better, j, i1)
        v1 = jnp.where(better, s_loc[j], v1)
        g1 = jnp.where(better, a_loc[j], g1)
    i2 = jnp.full_like(best_g, -1)
    v2 = jnp.full_like(v1, -jnp.inf)
    g2 = jnp.zeros_like(g1)
    for j in range(PER_GROUP):
        better = (i1 != j) & ((s_loc[j] > v2) | (i2 < 0))
        i2 = jnp.where(better, j, i2)
        v2 = jnp.where(better, s_loc[j], v2)
        g2 = jnp.where(better, a_loc[j], g2)
    total = g1 + g2
    tm = x_ref.shape[0]
    eid_ref[...] = jnp.concatenate([best_g * PER_GROUP + i1, best_g * PER_GROUP + i2,
                                    jnp.zeros((MOD_ROWS - TOP_K, tm), jnp.int32)], axis=0)
    gate_ref[...] = jnp.concatenate([g1 / total, g2 / total, jnp.zeros((MOD_ROWS - TOP_K, tm), F32)], axis=0)


def _post(oa, ob, oc, xa, mod, w_out, ln_g, ln_b, wr_t, rb, n_lat, alpha):
    bsz, s_tot, d = xa.shape
    tm = TOK_TILE
    n_lat_tiles = n_lat // tm
    n_tiles = s_tot // tm
    sd = jax.ShapeDtypeStruct

    def tmajor(width):
        return pl.BlockSpec((None, tm, width), lambda b, i: (b, i, 0))

    def const(shape):
        return pl.BlockSpec(shape, lambda b, i: (0,) * len(shape))

    lanes = pl.BlockSpec((MOD_ROWS, tm), lambda b, i: (0, b * n_tiles + i))
    return pl.pallas_call(
        functools.partial(_post_kernel, alpha=alpha),
        grid=(bsz, n_tiles),
        in_specs=[tmajor(QA_W), tmajor(QB_W), tmajor(QC_W), tmajor(d),
                  pl.BlockSpec((None, N_MOD, d), lambda b, i: (jnp.where(i >= n_lat_tiles, bsz, b), 0, 0)),
                  const((d, d)), const((1, d)), const((1, d)), const((N_EXPERTS, d)), const((N_EXPERTS, 1))],
        out_specs=[tmajor(d), pl.BlockSpec((tm * TILE_ROWS, 128), lambda b, i: (b * n_tiles + i, 0)), lanes, lanes],
        out_shape=[sd((bsz, s_tot, d), F32), sd((bsz * s_tot * TILE_ROWS, 128), F32),
                   sd((MOD_ROWS, bsz * s_tot), jnp.int32), sd((MOD_ROWS, bsz * s_tot), F32)],
        compiler_params=_params(("parallel", "parallel")),
    )(oa, ob, oc, xa, mod, w_out, ln_g, ln_b, wr_t, rb)


def _ffn_kernel(bexp_ref, nused_ref, tok_ref, tok_next_ref, u_hbm, w1_ref, w3_ref, w2_ref, y_ref, xbuf, sem):
    del bexp_ref
    blk = y_ref.shape[0] // TILE_ROWS
    step = pl.program_id(0)
    n_used = nused_ref[0]
    slot = step % 2

    @pl.when(step < n_used)
    def _():
        @pl.when(step == 0)
        def _():
            _start_gather(u_hbm, xbuf.at[0], tok_ref, blk, sem.at[0])

        @pl.when(step + 1 < n_used)
        def _():
            _start_gather(u_hbm, xbuf.at[1 - slot], tok_next_ref, blk, sem.at[1 - slot])

        _wait_gather(u_hbm, xbuf.at[slot], blk, sem.at[slot])
        xb = _load_token_tiles(xbuf.at[slot], blk).astype(BF16)
        h1 = jnp.dot(xb, w1_ref[...], preferred_element_type=F32)
        h3 = jnp.dot(xb, w3_ref[...], preferred_element_type=F32)
        hid = (h1 * jax.nn.sigmoid(h1) * h3).astype(BF16)
        _store_token_tiles(y_ref, jnp.dot(hid, w2_ref[...], preferred_element_type=F32))

    @pl.when(step >= n_used)
    def _():
        y_ref[...] = jnp.zeros_like(y_ref)


def _expert_ffn(block_expert, n_used, slot_tok, u2_tiles, w1, w3, w2):
    d = w1.shape[1]
    n_blocks = block_expert.shape[0]
    blk = EXPERT_BLOCK
    slot_tok = slot_tok.reshape(n_blocks, 1, blk)
    grid_spec = pltpu.PrefetchScalarGridSpec(
        num_scalar_prefetch=2,
        grid=(n_blocks,),
        in_specs=[pl.BlockSpec((1, 1, blk), lambda i, be, nu: (i, 0, 0), memory_space=pltpu.SMEM),
                  pl.BlockSpec((1, 1, blk), lambda i, be, nu: (jnp.minimum(i + 1, n_blocks - 1), 0, 0),
                               memory_space=pltpu.SMEM),
                  pl.BlockSpec(memory_space=pl.ANY),
                  pl.BlockSpec((None, d, D_EXPERT), lambda i, be, nu: (be[i], 0, 0)),
                  pl.BlockSpec((None, d, D_EXPERT), lambda i, be, nu: (be[i], 0, 0)),
                  pl.BlockSpec((None, D_EXPERT, d), lambda i, be, nu: (be[i], 0, 0))],
        out_specs=pl.BlockSpec((blk * TILE_ROWS, 128), lambda i, be, nu: (i, 0)),
        scratch_shapes=[pltpu.VMEM((2, blk * TILE_ROWS, 128), F32), pltpu.SemaphoreType.DMA((2,))],
    )
    return pl.pallas_call(
        _ffn_kernel,
        grid_spec=grid_spec,
        out_shape=jax.ShapeDtypeStruct((n_blocks * blk * TILE_ROWS, 128), F32),
        compiler_params=_params(("arbitrary",)),
    )(block_expert, n_used, slot_tok, slot_tok, u2_tiles, w1, w3, w2)


def _comb_kernel(pos_ref, pos_next_ref, y_hbm, x1_ref, gate_ref, mod_ref, g_ref, b_ref, o_ref, ybuf, sem, *, alpha):
    tm = x1_ref.shape[0]
    n = TOP_K * tm
    step = pl.program_id(0)
    slot = step % 2

    @pl.when(step == 0)
    def _():
        _start_gather(y_hbm, ybuf.at[0], pos_ref, n, sem.at[0])

    @pl.when(step + 1 < pl.num_programs(0))
    def _():
        _start_gather(y_hbm, ybuf.at[1 - slot], pos_next_ref, n, sem.at[1 - slot])

    _wait_gather(y_hbm, ybuf.at[slot], n, sem.at[slot])
    rows = _load_token_tiles(ybuf.at[slot], n)
    f = gate_ref[:, 0:1] * rows[0:tm, :] + gate_ref[:, 1:2] * rows[tm:n, :]
    o_ref[...] = _layer_norm(alpha * x1_ref[...] + (1.0 + mod_ref[5:6, :]) * f, g_ref[...], b_ref[...])


def _combine(pos, y_tiles, x1, gates, mod, ln_g, ln_b, n_lat, alpha):
    bsz, s_tot, d = x1.shape
    tm = TOK_TILE
    n_lat_tiles = n_lat // tm
    n_tiles = s_tot // tm
    n_steps = bsz * n_tiles

    def mod_row(t):
        return jnp.where(t % n_tiles >= n_lat_tiles, bsz, t // n_tiles)

    out = pl.pallas_call(
        functools.partial(_comb_kernel, alpha=alpha),
        grid=(n_steps,),
        in_specs=[pl.BlockSpec((1, 1, TOP_K * tm), lambda t: (t, 0, 0), memory_space=pltpu.SMEM),
                  pl.BlockSpec((1, 1, TOP_K * tm), lambda t: (jnp.minimum(t + 1, n_steps - 1), 0, 0),
                               memory_space=pltpu.SMEM),
                  pl.BlockSpec(memory_space=pl.ANY),
                  pl.BlockSpec((tm, d), lambda t: (t, 0)),
                  pl.BlockSpec((tm, TOP_K), lambda t: (t, 0)),
                  pl.BlockSpec((None, N_MOD, d), lambda t: (mod_row(t), 0, 0)),
                  pl.BlockSpec((1, d), lambda t: (0, 0)),
                  pl.BlockSpec((1, d), lambda t: (0, 0))],
        out_specs=pl.BlockSpec((tm, d), lambda t: (t, 0)),
        out_shape=jax.ShapeDtypeStruct((bsz * s_tot, d), F32),
        scratch_shapes=[pltpu.VMEM((2, TOP_K * tm * TILE_ROWS, 128), F32), pltpu.SemaphoreType.DMA((2,))],
        compiler_params=_params(("arbitrary",)),
    )(pos, pos, y_tiles, x1.reshape(bsz * s_tot, d), gates, mod, ln_g, ln_b)
    return out.reshape(bsz, s_tot, d)


def _deinterleave(n_heads):
    one = np.concatenate([np.arange(0, HEAD_DIM, 2), np.arange(1, HEAD_DIM, 2)])
    return np.concatenate([h * HEAD_DIM + one for h in range(n_heads)])


def _proj_weight(w_in_l):
    sizes = (QA_W, KA_W, KA_W, QB_W, KB_W, KB_W, QC_W, KC_W, KC_W)
    offs = np.concatenate([[0], np.cumsum(sizes)])
    qa, ka, va, qb, kb, vb, qc, kc, vc = [w_in_l[:, offs[i]:offs[i + 1]] for i in range(len(sizes))]
    scale = HEAD_DIM ** -0.5
    qa = qa[:, _deinterleave(A_HEADS)] * scale
    ka = ka[:, _deinterleave(A_KV)]
    qb = qb * scale
    qc = qc[:, _deinterleave(C_HEADS)]
    kc = kc[:, _deinterleave(C_KV)]
    return jnp.concatenate([qa, qb, qc, ka, kb, kc, va, vb, vc], axis=1).T.astype(BF16)


def _rope_tables(n_lat, n_ctx):
    t = np.arange(n_lat)
    inv_freq = ROPE_BASE ** (-np.arange(0, HALF, 2, dtype=np.float32) / HALF)
    ang = jnp.concatenate([jnp.asarray(t // GRID_W, F32)[None, :] * jnp.asarray(inv_freq)[:, None],
                           jnp.asarray(t % GRID_W, F32)[None, :] * jnp.asarray(inv_freq)[:, None]], axis=0)
    cos = jnp.concatenate([jnp.cos(ang), jnp.ones((HALF, n_ctx), F32)], axis=1)
    sin = jnp.concatenate([jnp.sin(ang), jnp.zeros((HALF, n_ctx), F32)], axis=1)
    return cos, sin


def _dispatch(eid, n_tok):
    n_assign = n_tok * TOP_K
    flat_e = eid.T.reshape(-1)
    onehot = (flat_e[:, None] == jnp.arange(N_EXPERTS, dtype=jnp.int32)[None, :]).astype(jnp.int32)
    csum = jnp.cumsum(onehot, axis=0)
    rank = jnp.take_along_axis(csum, flat_e[:, None], axis=1)[:, 0] - 1
    counts = csum[-1]
    padded = (counts + EXPERT_BLOCK - 1) // EXPERT_BLOCK * EXPERT_BLOCK
    pad_ends = jnp.cumsum(padded)
    dest = (pad_ends - padded)[flat_e] + rank
    n_blocks = -(-n_assign // EXPERT_BLOCK) + N_EXPERTS
    tok = jnp.arange(n_assign, dtype=jnp.int32) // TOP_K
    slot_tok = jnp.zeros((n_blocks * EXPERT_BLOCK,), jnp.int32).at[dest].set(tok)
    block_expert = jnp.minimum(
        jnp.searchsorted(pad_ends, jnp.arange(n_blocks, dtype=jnp.int32) * EXPERT_BLOCK, side='right'),
        N_EXPERTS - 1).astype(jnp.int32)
    n_used = (pad_ends[-1] // EXPERT_BLOCK).astype(jnp.int32).reshape(1)
    return slot_tok, block_expert, dest.reshape(n_tok, TOP_K).astype(jnp.int32), n_used


def kernel(x, c, ctx, c_ctx, w_in, w_out, sink, rpb, q_gain, k_gain, w_ada, b_ada,
           ln1_g, ln1_b, ln2_g, ln2_b, w_router, router_bias, w1, w3, w2):
    bsz, n_lat, d = x.shape
    n_ctx = ctx.shape[1]
    s_tot = n_lat + n_ctx
    depth = w_in.shape[0]
    assert d == D_MODEL and n_ctx == TOK_TILE and n_lat % min(TK_C, n_lat) == 0
    assert n_lat % (2 * TOK_TILE) == 0 and bsz + 1 <= MOD_ROWS
    alpha = float((2 * depth) ** 0.25)
    n_tok = bsz * s_tot
    tm = TOK_TILE

    cond = jnp.zeros((MOD_ROWS, d), F32).at[:bsz].set(c).at[bsz].set(c_ctx)
    mods = _ada_table(cond, w_ada, b_ada).reshape(depth, MOD_ROWS, N_MOD, d)
    cos_t, sin_t = _rope_tables(n_lat, n_ctx)
    gain_perm = _deinterleave(1)
    wr_t = w_router.T
    rb = router_bias.reshape(N_EXPERTS, 1).astype(F32)

    bias_tiles, starts, variants, n_keys = _nb_bias_tiles(rpb, n_lat, s_tot)

    xa = jnp.concatenate([x, ctx], axis=1)
    for l in range(depth):
        mod = mods[l]
        qa, qb, qc, ka, kb, kc, va, vb, vc, qn, kn = _project(
            xa, mod, _proj_weight(w_in[l]), cos_t, sin_t,
            q_gain[l][gain_perm].reshape(HEAD_DIM, 1) * (HEAD_DIM ** -0.5 * LOG2_E),
            k_gain[l][gain_perm].reshape(HEAD_DIM, 1), n_lat)
        oa = _attn_a(sink[l], qa, ka, va, n_lat)
        ob = _attn_b(qb, kb, vb, bias_tiles[l], starts, variants, n_keys, n_lat)
        oc = _attn_c(qc, qn, kc, kn, vc, n_lat)
        x1, u2, eid8, gate8 = _post(oa, ob, oc, xa, mod, w_out[l].astype(BF16),
                                    ln1_g[l].reshape(1, d), ln1_b[l].reshape(1, d), wr_t, rb, n_lat, alpha)
        slot_tok, block_expert, dest, n_used = _dispatch(eid8[:TOP_K], n_tok)
        y = _expert_ffn(block_expert, n_used, slot_tok, u2,
                        w1[l].astype(BF16), w3[l].astype(BF16), w2[l].astype(BF16))
        pos = dest.reshape(n_tok // tm, tm, TOP_K).transpose(0, 2, 1).reshape(n_tok // tm, 1, TOP_K * tm)
        xa = _combine(pos, y, x1, gate8[:TOP_K].T, mod, ln2_g[l].reshape(1, d), ln2_b[l].reshape(1, d),
                      n_lat, alpha)
    return xa[:, :n_lat]
```

```python
import functools

import numpy as np
import jax
import jax.numpy as jnp
from jax import lax
from jax.experimental import pallas as pl
from jax.experimental.pallas import tpu as pltpu

F32 = jnp.float32
BF16 = jnp.bfloat16

D_MODEL = 1024
HEAD_DIM = 64
HALF = HEAD_DIM // 2
GRID_W = 64
A_HEADS, A_KV = 6, 2
WINDOW = 128
B_HEADS = 4
NA_ROWS, NA_COLS = 8, 16
C_HEADS, C_KV = 6, 2
GQA_GROUP = 3
ROPE_BASE = 10000.0
N_EXPERTS = 16
N_EXPERT_GROUPS = 4
PER_GROUP = N_EXPERTS // N_EXPERT_GROUPS
TOP_K = 2
D_EXPERT = 512
EXPERT_BLOCK = 512
N_MOD = 6
LN_EPS = 1e-6
QK_EPS = 1e-6
NEG_INF = -1e30

QA_W, QB_W, QC_W = A_HEADS * HEAD_DIM, B_HEADS * HEAD_DIM, C_HEADS * HEAD_DIM
KA_W, KB_W, KC_W = A_KV * HEAD_DIM, B_HEADS * HEAD_DIM, C_KV * HEAD_DIM
R_QA = 0
R_QB = R_QA + QA_W
R_QC = R_QB + QB_W
R_KA = R_QC + QC_W
R_KB = R_KA + KA_W
R_KC = R_KB + KB_W
R_VA = R_KC + KC_W
R_VB = R_VA + KA_W
R_VC = R_VB + KB_W
PROJ_WIDTH = R_VC + KC_W

V7X_VMEM_LIMIT = 56 * 1024 * 1024
TOK_TILE = 256
TQ_A = 256
TQ_B = 2 * GRID_W
TQ_C = 256
TK_C = 2048
MOD_ROWS = 8
LOG2_E = 1.4426950408889634
MAX_SCORE_BOUND = 60.0
BOUND_SLACK = 1.004
TILE_ROWS = 8
BF16_ROWS = 16
V_ROWS = HEAD_DIM + BF16_ROWS


def _params(sem):
    return pltpu.CompilerParams(dimension_semantics=sem, vmem_limit_bytes=V7X_VMEM_LIMIT)


def _ada_kernel(c_ref, w_ref, b_ref, o_ref):
    c = c_ref[...]
    act = c * jax.nn.sigmoid(c)
    o_ref[...] = jnp.dot(act, w_ref[...], preferred_element_type=F32,
                         precision=lax.Precision.HIGHEST) + b_ref[...]


def _ada_table(cond, w_ada, b_ada):
    depth, d, n = w_ada.shape
    tn = n // 4
    return pl.pallas_call(
        _ada_kernel,
        grid=(depth, n // tn),
        in_specs=[pl.BlockSpec((MOD_ROWS, d), lambda l, j: (0, 0)),
                  pl.BlockSpec((None, d, tn), lambda l, j: (l, 0, j)),
                  pl.BlockSpec((None, 1, tn), lambda l, j: (l, 0, j))],
        out_specs=pl.BlockSpec((None, MOD_ROWS, tn), lambda l, j: (l, 0, j)),
        out_shape=jax.ShapeDtypeStruct((depth, MOD_ROWS, n), F32),
        compiler_params=_params(("parallel", "parallel")),
    )(cond, w_ada, b_ada.reshape(depth, 1, n))


def _sq_norm(parts):
    return sum(jnp.sum(jnp.square(p.astype(F32)), axis=0, keepdims=True) for p in parts)


def _proj_kernel(x_ref, mod_ref, w_ref, cos_ref, sin_ref, qg_ref, kg_ref,
                 qa_ref, qb_ref, qc_ref, ka_ref, kb_ref, kc_ref, va_ref, vb_ref, vc_ref, qn_ref, kn_ref):
    tm = x_ref.shape[0]
    u = (x_ref[...] * (1.0 + mod_ref[1:2, :]) + mod_ref[0:1, :]).astype(BF16)

    def proj(lo, width):
        return lax.dot_general(w_ref[lo:lo + width, :], u, (((1,), (1,)), ((), ())),
                               preferred_element_type=F32)

    cos = cos_ref[...]
    sin = sin_ref[...]

    def rope(blk):
        x0, x1 = blk[:HALF], blk[HALF:]
        return x0 * cos - x1 * sin, x0 * sin + x1 * cos

    def qk_norm(blk, gain):
        ms = jnp.mean(blk * blk, axis=0, keepdims=True)
        return blk * lax.rsqrt(ms + QK_EPS) * gain

    p = proj(R_QA, QA_W)
    for h in range(A_HEADS):
        o0, o1 = rope(p[h * HEAD_DIM:(h + 1) * HEAD_DIM])
        qa_ref[h * HEAD_DIM:h * HEAD_DIM + HALF, :] = o0.astype(BF16)
        qa_ref[h * HEAD_DIM + HALF:(h + 1) * HEAD_DIM, :] = o1.astype(BF16)
    qb_ref[...] = proj(R_QB, QB_W).astype(BF16)
    p = proj(R_QC, QC_W)
    qg = qg_ref[...]
    for h in range(C_HEADS):
        o0, o1 = [o.astype(BF16) for o in rope(qk_norm(p[h * HEAD_DIM:(h + 1) * HEAD_DIM], qg))]
        qc_ref[h * HEAD_DIM:h * HEAD_DIM + HALF, :] = o0
        qc_ref[h * HEAD_DIM + HALF:(h + 1) * HEAD_DIM, :] = o1
        qn_ref[h:h + 1, :] = _sq_norm([o0, o1])
    qn_ref[C_HEADS:MOD_ROWS, :] = jnp.zeros((MOD_ROWS - C_HEADS, tm), F32)
    p = proj(R_KA, KA_W)
    parts = []
    for h in range(A_KV):
        parts.extend(rope(p[h * HEAD_DIM:(h + 1) * HEAD_DIM]))
    ka_ref[...] = jnp.concatenate(parts, axis=0).T.astype(BF16)
    kb_ref[...] = proj(R_KB, KB_W).T.astype(BF16)
    p = proj(R_KC, KC_W)
    kg = kg_ref[...]
    parts = []
    for h in range(C_KV):
        o0, o1 = [o.astype(BF16) for o in rope(qk_norm(p[h * HEAD_DIM:(h + 1) * HEAD_DIM], kg))]
        kn_ref[h:h + 1, :] = _sq_norm([o0, o1])
        parts.extend([o0.astype(F32), o1.astype(F32)])
    kn_ref[C_KV:MOD_ROWS, :] = jnp.zeros((MOD_ROWS - C_KV, tm), F32)
    kc_ref[:, 0:KC_W] = jnp.concatenate(parts, axis=0).T.astype(BF16)
    kc_ref[:, KC_W:2 * KC_W] = (lax.broadcasted_iota(jnp.int32, (tm, KC_W), 1) == 0).astype(BF16)
    ones_row = (lax.broadcasted_iota(jnp.int32, (BF16_ROWS, tm), 0) == 0).astype(BF16)
    for lo, ref in ((R_VA, va_ref), (R_VC, vc_ref)):
        p = proj(lo, KA_W)
        for g in range(A_KV):
            ref[g, 0:HEAD_DIM, :] = p[g * HEAD_DIM:(g + 1) * HEAD_DIM].astype(BF16)
            ref[g, HEAD_DIM:V_ROWS, :] = ones_row
    vb_ref[...] = proj(R_VB, KB_W).astype(BF16)


def _project(xa, mod, w_t, cos_t, sin_t, q_gain, k_gain, n_lat):
    bsz, s_tot, d = xa.shape
    tm = TOK_TILE
    n_lat_tiles = n_lat // tm
    sd = jax.ShapeDtypeStruct

    def dmajor(width):
        return pl.BlockSpec((None, width, tm), lambda b, i: (b, 0, i))

    def tmajor(width):
        return pl.BlockSpec((None, tm, width), lambda b, i: (b, i, 0))

    aug = pl.BlockSpec((None, A_KV, V_ROWS,tm), lambda b, i: (b, 0, 0, i))
    return pl.pallas_call(
        _proj_kernel,
        grid=(bsz, s_tot // tm),
        in_specs=[tmajor(d),
                  pl.BlockSpec((None, N_MOD, d), lambda b, i: (jnp.where(i >= n_lat_tiles, bsz, b), 0, 0)),
                  pl.BlockSpec((PROJ_WIDTH, d), lambda b, i: (0, 0)),
                  pl.BlockSpec((HALF, tm), lambda b, i: (0, i)),
                  pl.BlockSpec((HALF, tm), lambda b, i: (0, i)),
                  pl.BlockSpec((HEAD_DIM, 1), lambda b, i: (0, 0)),
                  pl.BlockSpec((HEAD_DIM, 1), lambda b, i: (0, 0))],
        out_specs=[dmajor(QA_W), dmajor(QB_W), dmajor(QC_W),
                   tmajor(KA_W), tmajor(KB_W), tmajor(2 * KC_W),
                   aug, dmajor(KB_W), aug, dmajor(MOD_ROWS), dmajor(MOD_ROWS)],
        out_shape=[sd((bsz, QA_W, s_tot), BF16), sd((bsz, QB_W, s_tot), BF16), sd((bsz, QC_W, s_tot), BF16),
                   sd((bsz, s_tot, KA_W), BF16), sd((bsz, s_tot, KB_W), BF16), sd((bsz, s_tot, 2 * KC_W), BF16),
                   sd((bsz, A_KV, V_ROWS,s_tot), BF16), sd((bsz, KB_W, s_tot), BF16),
                   sd((bsz, A_KV, V_ROWS,s_tot), BF16),
                   sd((bsz, MOD_ROWS, s_tot), F32), sd((bsz, MOD_ROWS, s_tot), F32)],
        compiler_params=_params(("parallel", "parallel")),
    )(xa, mod, w_t, cos_t, sin_t, q_gain, k_gain)


def _padded_q(q_ref, h, g):
    blk = q_ref[h * HEAD_DIM:(h + 1) * HEAD_DIM, :]
    zeros = jnp.zeros_like(blk)
    return jnp.concatenate([blk, zeros] if g == 0 else [zeros, blk], axis=0)


def _attn_a_kernel(sink_ref, q_ref, k_ref, v_ref, o_ref, ot_ref, *, n_lat, n_ctx):
    tq = q_ref.shape[1]
    span = tq + 2 * WINDOW
    start = pl.program_id(1) * tq
    ks = pl.multiple_of(jnp.clip(start - WINDOW, 0, n_lat - span), 128)
    kpos = ks + lax.broadcasted_iota(jnp.int32, (span, tq), 0)
    qpos = start + lax.broadcasted_iota(jnp.int32, (span, tq), 1)
    ok = (jnp.abs(qpos - kpos) <= WINDOW) & (qpos < n_lat)
    k_loc = k_ref[pl.ds(ks, span), :]
    k_ctx = k_ref[n_lat:n_lat + n_ctx, :]
    for g in range(A_KV):
        v_loc = v_ref[g, :, pl.ds(ks, span)]
        v_ctx = v_ref[g, :, n_lat:n_lat + n_ctx]
        for hh in range(GQA_GROUP):
            h = g * GQA_GROUP + hh
            qp = _padded_q(q_ref, h, g)
            s_loc = jnp.where(ok, jnp.dot(k_loc, qp, preferred_element_type=F32), NEG_INF)
            s_ctx = jnp.dot(k_ctx, qp, preferred_element_type=F32)
            sink = sink_ref[h]
            m = jnp.maximum(jnp.maximum(jnp.max(s_loc, axis=0, keepdims=True),
                                        jnp.max(s_ctx, axis=0, keepdims=True)), sink)
            p_loc = jnp.exp(s_loc - m).astype(BF16)
            p_ctx = jnp.exp(s_ctx - m).astype(BF16)
            acc = (jnp.dot(v_loc, p_loc, preferred_element_type=F32)
                   + jnp.dot(v_ctx, p_ctx, preferred_element_type=F32))
            denom = acc[HEAD_DIM:HEAD_DIM + 1, :] + jnp.exp(sink - m)
            ot_ref[h * HEAD_DIM:(h + 1) * HEAD_DIM, :] = acc[0:HEAD_DIM, :] / denom
    o_ref[...] = ot_ref[...].T.astype(BF16)


def _attn_a(sink, q_t, k, v_t, n_lat):
    bsz, s_tot, _ = k.shape
    tq = TQ_A
    kern = functools.partial(_attn_a_kernel, n_lat=n_lat, n_ctx=s_tot - n_lat)
    return pl.pallas_call(
        kern,
        grid=(bsz, s_tot // tq),
        in_specs=[pl.BlockSpec(memory_space=pltpu.SMEM),
                  pl.BlockSpec((None, QA_W, tq), lambda b, i: (b, 0, i)),
                  pl.BlockSpec((None, s_tot, KA_W), lambda b, i: (b, 0, 0)),
                  pl.BlockSpec((None, A_KV, V_ROWS,s_tot), lambda b, i: (b, 0, 0, 0))],
        out_specs=pl.BlockSpec((None, tq, QA_W), lambda b, i: (b, i, 0)),
        out_shape=jax.ShapeDtypeStruct((bsz, s_tot, QA_W), BF16),
        scratch_shapes=[pltpu.VMEM((QA_W, tq), F32)],
        compiler_params=_params(("parallel", "arbitrary")),
    )(sink, q_t, k, v_t)


def _nb_plan(n_lat, s_tot):
    rows = n_lat // GRID_W
    kh = min(NA_ROWS, rows)
    q_rows = TQ_B // GRID_W
    w_rows = kh + q_rows
    n_lat_blocks = n_lat // TQ_B
    n_blocks = s_tot // TQ_B
    starts, variants, keys, reps = [], [], {}, []
    for i in range(n_lat_blocks):
        w0 = int(np.clip(q_rows * i - kh // 2, 0, rows - w_rows))
        r0s = tuple(int(np.clip(q_rows * i + j - kh // 2, 0, rows - kh)) - w0 for j in range(q_rows))
        key = (q_rows * i - w0,) + r0s
        if key not in keys:
            keys[key] = len(reps)
            reps.append((i, w0))
        starts.append(w0 * GRID_W)
        variants.append(keys[key])
    masked = len(reps)
    starts.extend([0] * (n_blocks - n_lat_blocks))
    variants.extend([masked] * (n_blocks - n_lat_blocks))
    return rows, kh, w_rows, reps, np.asarray(starts, np.int32), np.asarray(variants, np.int32)


def _rel_bias_kernel(r_ref, e_ref, m_ref, o_ref):
    o_ref[...] = jnp.dot(r_ref[...], e_ref[...], preferred_element_type=F32,
                         precision=lax.Precision.HIGHEST) + m_ref[...]


def _rel_bias_blocks(rpb):
    depth, heads, n_dr, n_dc = rpb.shape
    kc = np.arange(GRID_W)[:, None]
    cq = np.arange(GRID_W)[None, :]
    dc = (kc - cq + NA_COLS - 1).reshape(-1)
    n_dc_pad = 32
    onehot = (np.arange(n_dc_pad)[:, None] == dc[None, :]).astype(np.float32)
    c0 = np.clip(cq - NA_COLS // 2, 0, GRID_W - NA_COLS)
    outside = ~((kc >= c0) & (kc < c0 + NA_COLS))
    col_mask = np.where(outside, NEG_INF, 0.0).astype(np.float32).reshape(1, -1)
    table = jnp.pad(rpb.astype(F32).reshape(depth * heads * n_dr, n_dc), ((0, 0), (0, n_dc_pad - n_dc)))
    blocks = pl.pallas_call(
        _rel_bias_kernel,
        out_shape=jax.ShapeDtypeStruct((depth * heads * n_dr, GRID_W * GRID_W), F32),
    )(table, jnp.asarray(onehot), jnp.asarray(col_mask))
    blocks = blocks.reshape(depth, heads, n_dr, GRID_W, GRID_W)
    return jnp.concatenate([blocks, jnp.full((depth, heads, 1, GRID_W, GRID_W), NEG_INF, F32)], axis=2)


def _nb_bias_tiles(rpb, n_lat, s_tot):
    rows, kh, w_rows, reps, starts, variants = _nb_plan(n_lat, s_tot)
    q_rows = TQ_B // GRID_W
    masked_block = 2 * NA_ROWS - 1
    dr_map = np.full((len(reps) + 1, w_rows, q_rows), masked_block, np.int32)
    for v, (i, w0) in enumerate(reps):
        for a in range(w_rows):
            for b in range(q_rows):
                r = q_rows * i + b
                r0 = int(np.clip(r - kh // 2, 0, rows - kh))
                if r0 <= w0 + a < r0 + kh:
                    dr_map[v, a, b] = w0 + a - r + NA_ROWS - 1
    blocks = _rel_bias_blocks(rpb)[:, :, dr_map]
    depth = rpb.shape[0]
    tiles = jnp.transpose(blocks, (0, 2, 3, 5, 1, 4, 6)).reshape(
        depth, len(reps) + 1, w_rows * GRID_W, B_HEADS * TQ_B)
    return tiles, jnp.asarray(starts), jnp.asarray(variants), w_rows * GRID_W


def _attn_b_kernel(start_ref, var_ref, q_ref, k_ref, v_ref, bias_ref, o_ref, qp_ref, ot_ref, *, n_lat, n_ctx, n_keys):
    tq = q_ref.shape[1]
    ws = pl.multiple_of(start_ref[pl.program_id(1)], 128)
    qp_ref[...] = jnp.zeros_like(qp_ref)
    for h in range(B_HEADS):
        qp_ref[h * HEAD_DIM:(h + 1) * HEAD_DIM, h * tq:(h + 1) * tq] = q_ref[h * HEAD_DIM:(h + 1) * HEAD_DIM, :]
    qp = qp_ref[...]
    s_loc = jnp.dot(k_ref[pl.ds(ws, n_keys), :], qp, preferred_element_type=F32) + bias_ref[...]
    s_ctx = jnp.dot(k_ref[n_lat:n_lat + n_ctx, :], qp, preferred_element_type=F32)
    m = jnp.maximum(jnp.max(s_loc, axis=0, keepdims=True), jnp.max(s_ctx, axis=0, keepdims=True))
    p_loc = jnp.exp(s_loc - m)
    p_ctx = jnp.exp(s_ctx - m)
    denom = jnp.sum(p_loc, axis=0, keepdims=True) + jnp.sum(p_ctx, axis=0, keepdims=True)
    acc = (jnp.dot(v_ref[:, pl.ds(ws, n_keys)], p_loc.astype(BF16), preferred_element_type=F32)
           + jnp.dot(v_ref[:, n_lat:n_lat + n_ctx], p_ctx.astype(BF16), preferred_element_type=F32))
    for h in range(B_HEADS):
        ot_ref[h * HEAD_DIM:(h + 1) * HEAD_DIM, :] = (
            acc[h * HEAD_DIM:(h + 1) * HEAD_DIM, h * tq:(h + 1) * tq] / denom[:, h * tq:(h + 1) * tq])
    o_ref[...] = ot_ref[...].T.astype(BF16)


def _attn_b(q_t, k, v_t, bias_tiles, starts, variants, n_keys, n_lat):
    bsz, s_tot, _ = k.shape
    tq = TQ_B
    kern = functools.partial(_attn_b_kernel, n_lat=n_lat, n_ctx=s_tot - n_lat, n_keys=n_keys)
    grid_spec = pltpu.PrefetchScalarGridSpec(
        num_scalar_prefetch=2,
        grid=(bsz, s_tot // tq),
        in_specs=[pl.BlockSpec((None, QB_W, tq), lambda b, i, st, va: (b, 0, i)),
                  pl.BlockSpec((None, s_tot, KB_W), lambda b, i, st, va: (b, 0, 0)),
                  pl.BlockSpec((None, KB_W, s_tot), lambda b, i, st, va: (b, 0, 0)),
                  pl.BlockSpec((None, n_keys, B_HEADS * tq), lambda b, i, st, va: (va[i], 0, 0))],
        out_specs=pl.BlockSpec((None, tq, QB_W), lambda b, i, st, va: (b, i, 0)),
        scratch_shapes=[pltpu.VMEM((QB_W, B_HEADS * tq), BF16), pltpu.VMEM((QB_W, tq), F32)],
    )
    return pl.pallas_call(
        kern,
        grid_spec=grid_spec,
        out_shape=jax.ShapeDtypeStruct((bsz, s_tot, QB_W), BF16),
        compiler_params=_params(("parallel", "arbitrary")),
    )(starts, variants, q_t, k, v_t, bias_tiles)


def _attn_c_kernel(fast_ref, ksc_ref, q_ref, qn_ref, k_ref, v_ref, o_ref, qp_ref, m_ref, acc_ref, ot_ref,
                   *, n_lat, n_ctx, tk):
    tq = q_ref.shape[1]
    is_ctx = pl.program_id(1) >= n_lat // tq
    n_chunks = jnp.where(is_ctx, 0, n_lat // tk)
    fast = fast_ref[0] == 1
    first_row = lax.broadcasted_iota(jnp.int32, (KC_W, tq), 0) == 0
    for g in range(C_KV):
        k_scale = ksc_ref[pl.program_id(0) * C_KV + g]
        for hh in range(GQA_GROUP):
            h = g * GQA_GROUP + hh
            qp_ref[g, 0:KC_W, hh * tq:(hh + 1) * tq] = _padded_q(q_ref, h, g)
            offset = jnp.where(fast, -jnp.sqrt(qn_ref[h:h + 1, :]) * k_scale, 0.0)
            qp_ref[g, KC_W:2 * KC_W, hh * tq:(hh + 1) * tq] = jnp.where(first_row, offset, 0.0).astype(BF16)
    acc_ref[...] = jnp.zeros_like(acc_ref)

    def bounded_step(g, start, size):
        s = jnp.dot(k_ref[pl.ds(start, size), :], qp_ref[g], preferred_element_type=F32)
        p = jnp.exp2(s).astype(BF16)
        acc_ref[g] += jnp.dot(v_ref[g, :, pl.ds(start, size)], p, preferred_element_type=F32)

    def online_step(g, start, size):
        s = jnp.dot(k_ref[pl.ds(start, size), :], qp_ref[g], preferred_element_type=F32)
        m_prev = m_ref[g]
        m_new = jnp.maximum(m_prev, jnp.max(s, axis=0, keepdims=True))
        p = jnp.exp2(s - m_new).astype(BF16)
        pv = jnp.dot(v_ref[g, :, pl.ds(start, size)], p, preferred_element_type=F32)
        acc_ref[g] = acc_ref[g] * jnp.exp2(m_prev - m_new) + pv
        m_ref[g] = m_new

    def sweep(step):
        def body(c, carry):
            for g in range(C_KV):
                step(g, pl.multiple_of(c * tk, tk), tk)
            return carry

        lax.fori_loop(0, n_chunks, body, 0)
        for g in range(C_KV):
            step(g, n_lat, n_ctx)

    @pl.when(fast)
    def _():
        sweep(bounded_step)

    @pl.when(jnp.logical_not(fast))
    def _():
        m_ref[...] = jnp.full_like(m_ref, NEG_INF)
        sweep(online_step)

    for g in range(C_KV):
        acc = acc_ref[g]
        out = acc[0:HEAD_DIM, :] / acc[HEAD_DIM:HEAD_DIM + 1, :]
        for hh in range(GQA_GROUP):
            h = g * GQA_GROUP + hh
            ot_ref[h * HEAD_DIM:(h + 1) * HEAD_DIM, :] = out[:, hh * tq:(hh + 1) * tq]
    o_ref[...] = ot_ref[...].T.astype(BF16)


def _attn_c(q_t, q_norm2, k, k_norm2, v_t, n_lat):
    bsz, s_tot, _ = k.shape
    tq = TQ_C
    tk = min(TK_C, n_lat)
    k_scale = jnp.sqrt(jnp.max(k_norm2[:, :C_KV, :], axis=-1)) * BOUND_SLACK
    bound = jnp.sqrt(jnp.max(q_norm2)) * jnp.max(k_scale)
    fast = (bound <= MAX_SCORE_BOUND).astype(jnp.int32).reshape(1)
    kern = functools.partial(_attn_c_kernel, n_lat=n_lat, n_ctx=s_tot - n_lat, tk=tk)
    smem = pl.BlockSpec(memory_space=pltpu.SMEM)
    return pl.pallas_call(
        kern,
        grid=(bsz, s_tot // tq),
        in_specs=[smem, smem,
                  pl.BlockSpec((None, QC_W, tq), lambda b, i: (b, 0, i)),
                  pl.BlockSpec((None, MOD_ROWS, tq), lambda b, i: (b, 0, i)),
                  pl.BlockSpec((None, s_tot, 2 * KC_W), lambda b, i: (b, 0, 0)),
                  pl.BlockSpec((None, C_KV, V_ROWS, s_tot), lambda b, i: (b, 0, 0, 0))],
        out_specs=pl.BlockSpec((None, tq, QC_W), lambda b, i: (b, i, 0)),
        out_shape=jax.ShapeDtypeStruct((bsz, s_tot, QC_W), BF16),
        scratch_shapes=[pltpu.VMEM((C_KV, 2 * KC_W, GQA_GROUP * tq), BF16),
                        pltpu.VMEM((C_KV, 1, GQA_GROUP * tq), F32),
                        pltpu.VMEM((C_KV, V_ROWS, GQA_GROUP * tq), F32),
                        pltpu.VMEM((QC_W, tq), F32)],
        compiler_params=_params(("parallel", "arbitrary")),
    )(fast, k_scale.reshape(-1), q_t, q_norm2, k, v_t)


def _store_token_tiles(ref, val):
    n = val.shape[0]
    for c in range(TILE_ROWS):
        ref[pl.ds(c, n, stride=TILE_ROWS), :] = val[:, c * 128:(c + 1) * 128]


def _load_token_tiles(ref, n):
    return jnp.concatenate([ref[pl.ds(c, n, stride=TILE_ROWS), :] for c in range(TILE_ROWS)], axis=1)


def _tile_copy(src_hbm, dst, idx_ref, j, sem):
    src = pl.multiple_of(idx_ref[0, 0, j] * TILE_ROWS, TILE_ROWS)
    return pltpu.make_async_copy(src_hbm.at[pl.ds(src, TILE_ROWS), :],
                                 dst.at[pl.ds(pl.multiple_of(j * TILE_ROWS, TILE_ROWS), TILE_ROWS), :], sem)


def _start_gather(src_hbm, dst, idx_ref, n, sem):
    def issue(j, carry):
        _tile_copy(src_hbm, dst, idx_ref, j, sem).start()
        return carry

    lax.fori_loop(0, n, issue, 0, unroll=8)


def _wait_gather(src_hbm, dst, n, sem):
    pltpu.make_async_copy(src_hbm.at[pl.ds(0, n * TILE_ROWS), :], dst, sem).wait()


def _layer_norm(h, g, b):
    mu = jnp.mean(h, axis=-1, keepdims=True)
    hc = h - mu
    var = jnp.mean(hc * hc, axis=-1, keepdims=True)
    return hc * lax.rsqrt(var + LN_EPS) * g + b


def _post_kernel(oa_ref, ob_ref, oc_ref, x_ref, mod_ref, w_ref, g_ref, b_ref, wr_ref, rb_ref,
                 x1_ref, u2_ref, eid_ref, gate_ref, o_scr, ub_scr, *, alpha):
    tm = x_ref.shape[0]
    o_cat = jnp.concatenate([oa_ref[...], ob_ref[...], oc_ref[...]], axis=1)
    o_scr[...] = jnp.dot(o_cat, w_ref[...], preferred_element_type=F32)
    gate1 = 1.0 + mod_ref[2:3, :]
    scale2 = 1.0 + mod_ref[4:5, :]
    shift2 = mod_ref[3:4, :]
    ln_g = g_ref[...]
    ln_b = b_ref[...]

    def row_group(r, carry):
        r0 = pl.multiple_of(r * BF16_ROWS, BF16_ROWS)
        rows = pl.ds(r0, BF16_ROWS)
        x1 = _layer_norm(alpha * x_ref[rows, :] + gate1 * o_scr[rows, :], ln_g, ln_b)
        x1_ref[rows, :] = x1
        u2 = x1 * scale2 + shift2
        ub_scr[rows, :] = u2.astype(BF16)
        for c in range(TILE_ROWS):
            u2_ref[pl.ds(r0 * TILE_ROWS + c, BF16_ROWS, stride=TILE_ROWS), :] = u2[:, c * 128:(c + 1) * 128]
        return carry

    lax.fori_loop(0, tm // BF16_ROWS, row_group, 0)
    logits = lax.dot_general(wr_ref[...], ub_scr[...], (((1,), (1,)), ((), ())), preferred_element_type=F32)
    aff_all = jax.nn.sigmoid(logits)
    sel_all = aff_all + rb_ref[...]
    aff = [aff_all[e:e + 1, :] for e in range(N_EXPERTS)]
    sel = [sel_all[e:e + 1, :] for e in range(N_EXPERTS)]
    gsum = []
    for g in range(N_EXPERT_GROUPS):
        a, b, c, d = sel[g * PER_GROUP:(g + 1) * PER_GROUP]
        hi1, lo1, hi2, lo2 = jnp.maximum(a, b), jnp.minimum(a, b), jnp.maximum(c, d), jnp.minimum(c, d)
        gsum.append(jnp.maximum(hi1, hi2) + jnp.maximum(jnp.minimum(hi1, hi2), jnp.maximum(lo1, lo2)))
    best_g = jnp.zeros_like(gsum[0], dtype=jnp.int32)
    best_v = gsum[0]
    for g in range(1, N_EXPERT_GROUPS):
        better = gsum[g] > best_v
        best_g = jnp.where(better, g, best_g)
        best_v = jnp.where(better, gsum[g], best_v)
    s_loc, a_loc = [], []
    for j in range(PER_GROUP):
        sj, aj = sel[j], aff[j]
        for g in range(1, N_EXPERT_GROUPS):
            pick = best_g == g
            sj = jnp.where(pick, sel[g * PER_GROUP + j], sj)
            aj = jnp.where(pick, aff[g * PER_GROUP + j], aj)
        s_loc.append(sj)
        a_loc.append(aj)
    i1 = jnp.zeros_like(best_g)
    v1, g1 = s_loc[0], a_loc[0]
    for j in range(1, PER_GROUP):
        better = s_loc[j] > v1
        i1 = jnp.where(better, j, i1)
        v1 = jnp.where(better, s_loc[j], v1)
        g1 = jnp.where(better, a_loc[j], g1)
    i2 = jnp.full_like(best_g, -1)
    v2 = jnp.full_like(v1, -jnp.inf)
    g2 = jnp.zeros_like(g1)
    for j in range(PER_GROUP):
        better = (i1 != j) & ((s_loc[j] > v2) | (i2 < 0))
        i2 = jnp.where(better, j, i2)
        v2 = jnp.where(better, s_loc[j], v2)
        g2 = jnp.where(better, a_loc[j], g2)
    total = g1 + g2
    tm = x_ref.shape[0]
    eid_ref[...] = jnp.concatenate([best_g * PER_GROUP + i1, best_g * PER_GROUP + i2,
                                    jnp.zeros((MOD_ROWS - TOP_K, tm), jnp.int32)], axis=0)
    gate_ref[...] = jnp.concatenate([g1 / total, g2 / total, jnp.zeros((MOD_ROWS - TOP_K, tm), F32)], axis=0)


def _post(oa, ob, oc, xa, mod, w_out, ln_g, ln_b, wr_t, rb, n_lat, alpha):
    bsz, s_tot, d = xa.shape
    tm = TOK_TILE
    n_lat_tiles = n_lat // tm
    n_tiles = s_tot // tm
    sd = jax.ShapeDtypeStruct

    def tmajor(width):
        return pl.BlockSpec((None, tm, width), lambda b, i: (b, i, 0))

    def const(shape):
        return pl.BlockSpec(shape, lambda b, i: (0,) * len(shape))

    lanes = pl.BlockSpec((MOD_ROWS, tm), lambda b, i: (0, b * n_tiles + i))
    return pl.pallas_call(
        functools.partial(_post_kernel, alpha=alpha),
        grid=(bsz, n_tiles),
        in_specs=[tmajor(QA_W), tmajor(QB_W), tmajor(QC_W), tmajor(d),
                  pl.BlockSpec((None, N_MOD, d), lambda b, i: (jnp.where(i >= n_lat_tiles, bsz, b), 0, 0)),
                  const((d, d)), const((1, d)), const((1, d)), const((N_EXPERTS, d)), const((N_EXPERTS, 1))],
        out_specs=[tmajor(d), pl.BlockSpec((tm * TILE_ROWS, 128), lambda b, i: (b * n_tiles + i, 0)), lanes, lanes],
        out_shape=[sd((bsz, s_tot, d), F32), sd((bsz * s_tot * TILE_ROWS, 128), F32),
                   sd((MOD_ROWS, bsz * s_tot), jnp.int32), sd((MOD_ROWS, bsz * s_tot), F32)],
        scratch_shapes=[pltpu.VMEM((tm, d), F32), pltpu.VMEM((tm, d), BF16)],
        compiler_params=_params(("parallel", "parallel")),
    )(oa, ob, oc, xa, mod, w_out, ln_g, ln_b, wr_t, rb)


def _ffn_kernel(bexp_ref, nused_ref, tok_ref, tok_next_ref, u_hbm, w1_ref, w3_ref, w2_ref, y_ref,
                xbuf, w1b, w3b, w2b, sem):
    blk = y_ref.shape[0] // TILE_ROWS
    step = pl.program_id(0)
    n_used = nused_ref[0]
    slot = step % 2

    @pl.when(step < n_used)
    def _():
        @pl.when(step == 0)
        def _():
            _start_gather(u_hbm, xbuf.at[0], tok_ref, blk, sem.at[0])

        @pl.when(step + 1 < n_used)
        def _():
            _start_gather(u_hbm, xbuf.at[1 - slot], tok_next_ref, blk, sem.at[1 - slot])

        @pl.when((step == 0) | (bexp_ref[step] != bexp_ref[jnp.maximum(step - 1, 0)]))
        def _():
            w1b[...] = w1_ref[...].astype(BF16)
            w3b[...] = w3_ref[...].astype(BF16)
            w2b[...] = w2_ref[...].astype(BF16)

        _wait_gather(u_hbm, xbuf.at[slot], blk, sem.at[slot])
        xb = _load_token_tiles(xbuf.at[slot], blk).astype(BF16)
        h1 = jnp.dot(xb, w1b[...], preferred_element_type=F32)
        h3 = jnp.dot(xb, w3b[...], preferred_element_type=F32)
        hid = (h1 * jax.nn.sigmoid(h1) * h3).astype(BF16)
        _store_token_tiles(y_ref, jnp.dot(hid, w2b[...], preferred_element_type=F32))

    @pl.when(step >= n_used)
    def _():
        y_ref[...] = jnp.zeros_like(y_ref)


def _expert_ffn(block_expert, n_used, slot_tok, u2_tiles, w1, w3, w2, layer):
    d = w1.shape[2]
    n_blocks = block_expert.shape[0]
    blk = EXPERT_BLOCK
    slot_tok = slot_tok.reshape(n_blocks, 1, blk)
    grid_spec = pltpu.PrefetchScalarGridSpec(
        num_scalar_prefetch=2,
        grid=(n_blocks,),
        in_specs=[pl.BlockSpec((1, 1, blk), lambda i, be, nu: (i, 0, 0), memory_space=pltpu.SMEM),
                  pl.BlockSpec((1, 1, blk), lambda i, be, nu: (jnp.minimum(i + 1, n_blocks - 1), 0, 0),
                               memory_space=pltpu.SMEM),
                  pl.BlockSpec(memory_space=pl.ANY),
                  pl.BlockSpec((None, None, d, D_EXPERT), lambda i, be, nu: (layer, be[i], 0, 0)),
                  pl.BlockSpec((None, None, d, D_EXPERT), lambda i, be, nu: (layer, be[i], 0, 0)),
                  pl.BlockSpec((None, None, D_EXPERT, d), lambda i, be, nu: (layer, be[i], 0, 0))],
        out_specs=pl.BlockSpec((blk * TILE_ROWS, 128), lambda i, be, nu: (i, 0)),
        scratch_shapes=[pltpu.VMEM((2, blk * TILE_ROWS, 128), F32),
                        pltpu.VMEM((d, D_EXPERT), BF16), pltpu.VMEM((d, D_EXPERT), BF16),
                        pltpu.VMEM((D_EXPERT, d), BF16), pltpu.SemaphoreType.DMA((2,))],
    )
    return pl.pallas_call(
        _ffn_kernel,
        grid_spec=grid_spec,
        out_shape=jax.ShapeDtypeStruct((n_blocks * blk * TILE_ROWS, 128), F32),
        compiler_params=_params(("arbitrary",)),
    )(block_expert, n_used, slot_tok, slot_tok, u2_tiles, w1, w3, w2)


def _comb_kernel(pos_ref, pos_next_ref, y_hbm, x1_ref, gate_ref, mod_ref, g_ref, b_ref, o_ref, ybuf, sem, *, alpha):
    tm = x1_ref.shape[0]
    n = TOP_K * tm
    step = pl.program_id(0)
    slot = step % 2

    @pl.when(step == 0)
    def _():
        _start_gather(y_hbm, ybuf.at[0], pos_ref, n, sem.at[0])

    @pl.when(step + 1 < pl.num_programs(0))
    def _():
        _start_gather(y_hbm, ybuf.at[1 - slot], pos_next_ref, n, sem.at[1 - slot])

    _wait_gather(y_hbm, ybuf.at[slot], n, sem.at[slot])
    rows = _load_token_tiles(ybuf.at[slot], n)
    f = gate_ref[:, 0:1] * rows[0:tm, :] + gate_ref[:, 1:2] * rows[tm:n, :]
    o_ref[...] = _layer_norm(alpha * x1_ref[...] + (1.0 + mod_ref[5:6, :]) * f, g_ref[...], b_ref[...])


def _combine(pos, y_tiles, x1, gates, mod, ln_g, ln_b, n_lat, alpha):
    bsz, s_tot, d = x1.shape
    tm = TOK_TILE
    n_lat_tiles = n_lat // tm
    n_tiles = s_tot // tm
    n_steps = bsz * n_tiles

    def mod_row(t):
        return jnp.where(t % n_tiles >= n_lat_tiles, bsz, t // n_tiles)

    out = pl.pallas_call(
        functools.partial(_comb_kernel, alpha=alpha),
        grid=(n_steps,),
        in_specs=[pl.BlockSpec((1, 1, TOP_K * tm), lambda t: (t, 0, 0), memory_space=pltpu.SMEM),
                  pl.BlockSpec((1, 1, TOP_K * tm), lambda t: (jnp.minimum(t + 1, n_steps - 1), 0, 0),
                               memory_space=pltpu.SMEM),
                  pl.BlockSpec(memory_space=pl.ANY),
                  pl.BlockSpec((tm, d), lambda t: (t, 0)),
                  pl.BlockSpec((tm, TOP_K), lambda t: (t, 0)),
                  pl.BlockSpec((None, N_MOD, d), lambda t: (mod_row(t), 0, 0)),
                  pl.BlockSpec((1, d), lambda t: (0, 0)),
                  pl.BlockSpec((1, d), lambda t: (0, 0))],
        out_specs=pl.BlockSpec((tm, d), lambda t: (t, 0)),
        out_shape=jax.ShapeDtypeStruct((bsz * s_tot, d), F32),
        scratch_shapes=[pltpu.VMEM((2, TOP_K * tm * TILE_ROWS, 128), F32), pltpu.SemaphoreType.DMA((2,))],
        compiler_params=_params(("arbitrary",)),
    )(pos, pos, y_tiles, x1.reshape(bsz * s_tot, d), gates, mod, ln_g, ln_b)
    return out.reshape(bsz, s_tot, d)


def _deinterleave(n_heads):
    one = np.concatenate([np.arange(0, HEAD_DIM, 2), np.arange(1, HEAD_DIM, 2)])
    return np.concatenate([h * HEAD_DIM + one for h in range(n_heads)])


def _proj_weight(w_in_l):
    sizes = (QA_W, KA_W, KA_W, QB_W, KB_W, KB_W, QC_W, KC_W, KC_W)
    offs = np.concatenate([[0], np.cumsum(sizes)])
    qa, ka, va, qb, kb, vb, qc, kc, vc = [w_in_l[:, offs[i]:offs[i + 1]] for i in range(len(sizes))]
    scale = HEAD_DIM ** -0.5
    qa = qa[:, _deinterleave(A_HEADS)] * scale
    ka = ka[:, _deinterleave(A_KV)]
    qb = qb * scale
    qc = qc[:, _deinterleave(C_HEADS)]
    kc = kc[:, _deinterleave(C_KV)]
    return jnp.concatenate([qa, qb, qc, ka, kb, kc, va, vb, vc], axis=1).T.astype(BF16)


def _rope_tables(n_lat, n_ctx):
    t = np.arange(n_lat)
    inv_freq = ROPE_BASE ** (-np.arange(0, HALF, 2, dtype=np.float32) / HALF)
    ang = jnp.concatenate([jnp.asarray(t // GRID_W, F32)[None, :] * jnp.asarray(inv_freq)[:, None],
                           jnp.asarray(t % GRID_W, F32)[None, :] * jnp.asarray(inv_freq)[:, None]], axis=0)
    cos = jnp.concatenate([jnp.cos(ang), jnp.ones((HALF, n_ctx), F32)], axis=1)
    sin = jnp.concatenate([jnp.sin(ang), jnp.zeros((HALF, n_ctx), F32)], axis=1)
    return cos, sin


def _dispatch(eid, n_tok):
    n_assign = n_tok * TOP_K
    flat_e = eid.T.reshape(-1)
    onehot = (flat_e[:, None] == jnp.arange(N_EXPERTS, dtype=jnp.int32)[None, :]).astype(jnp.int32)
    csum = jnp.cumsum(onehot, axis=0)
    rank = jnp.take_along_axis(csum, flat_e[:, None], axis=1)[:, 0] - 1
    counts = csum[-1]
    padded = (counts + EXPERT_BLOCK - 1) // EXPERT_BLOCK * EXPERT_BLOCK
    pad_ends = jnp.cumsum(padded)
    dest = (pad_ends - padded)[flat_e] + rank
    n_blocks = -(-n_assign // EXPERT_BLOCK) + N_EXPERTS
    tok = jnp.arange(n_assign, dtype=jnp.int32) // TOP_K
    slot_tok = jnp.zeros((n_blocks * EXPERT_BLOCK,), jnp.int32).at[dest].set(tok)
    block_expert = jnp.minimum(
        jnp.searchsorted(pad_ends, jnp.arange(n_blocks, dtype=jnp.int32) * EXPERT_BLOCK, side='right'),
        N_EXPERTS - 1).astype(jnp.int32)
    n_used = (pad_ends[-1] // EXPERT_BLOCK).astype(jnp.int32).reshape(1)
    return slot_tok, block_expert, dest.reshape(n_tok, TOP_K).astype(jnp.int32), n_used


def kernel(x, c, ctx, c_ctx, w_in, w_out, sink, rpb, q_gain, k_gain, w_ada, b_ada,
           ln1_g, ln1_b, ln2_g, ln2_b, w_router, router_bias, w1, w3, w2):
    bsz, n_lat, d = x.shape
    n_ctx = ctx.shape[1]
    s_tot = n_lat + n_ctx
    depth = w_in.shape[0]
    assert d == D_MODEL and n_ctx == TOK_TILE and n_lat % min(TK_C, n_lat) == 0
    assert n_lat % (2 * TOK_TILE) == 0 and bsz + 1 <= MOD_ROWS
    alpha = float((2 * depth) ** 0.25)
    n_tok = bsz * s_tot
    tm = TOK_TILE

    cond = jnp.zeros((MOD_ROWS, d), F32).at[:bsz].set(c).at[bsz].set(c_ctx)
    mods = _ada_table(cond, w_ada, b_ada).reshape(depth, MOD_ROWS, N_MOD, d)
    cos_t, sin_t = _rope_tables(n_lat, n_ctx)
    gain_perm = _deinterleave(1)
    wr_t = w_router.T.astype(BF16)
    rb = router_bias.reshape(N_EXPERTS, 1).astype(F32)

    bias_tiles, starts, variants, n_keys = _nb_bias_tiles(rpb, n_lat, s_tot)

    xa = jnp.concatenate([x, ctx], axis=1)
    for l in range(depth):
        mod = mods[l]
        qa, qb, qc, ka, kb, kc, va, vb, vc, qn, kn = _project(
            xa, mod, _proj_weight(w_in[l]), cos_t, sin_t,
            q_gain[l][gain_perm].reshape(HEAD_DIM, 1) * (HEAD_DIM ** -0.5 * LOG2_E),
            k_gain[l][gain_perm].reshape(HEAD_DIM, 1), n_lat)
        oa = _attn_a(sink[l], qa, ka, va, n_lat)
        ob = _attn_b(qb, kb, vb, bias_tiles[l], starts, variants, n_keys, n_lat)
        oc = _attn_c(qc, qn, kc, kn, vc, n_lat)
        x1, u2, eid8, gate8 = _post(oa, ob, oc, xa, mod, w_out[l].astype(BF16),
                                    ln1_g[l].reshape(1, d), ln1_b[l].reshape(1, d), wr_t, rb, n_lat, alpha)
        slot_tok, block_expert, dest, n_used = _dispatch(eid8[:TOP_K], n_tok)
        y = _expert_ffn(block_expert, n_used, slot_tok, u2, w1, w3, w2, l)
        pos = dest.reshape(n_tok // tm, tm, TOP_K).transpose(0, 2, 1).reshape(n_tok // tm, 1, TOP_K * tm)
        xa = _combine(pos, y, x1, gate8[:TOP_K].T, mod, ln2_g[l].reshape(1, d), ln2_b[l].reshape(1, d),
                      n_lat, alpha)
    return xa[:, :n_lat]
```

```python
import functools

import numpy as np
import jax
import jax.numpy as jnp
from jax import lax
from jax.experimental import pallas as pl
from jax.experimental.pallas import tpu as pltpu

F32 = jnp.float32
BF16 = jnp.bfloat16

D_MODEL = 1024
HEAD_DIM = 64
HALF = HEAD_DIM // 2
GRID_W = 64
A_HEADS, A_KV = 6, 2
WINDOW = 128
B_HEADS = 4
NA_ROWS, NA_COLS = 8, 16
C_HEADS, C_KV = 6, 2
GQA_GROUP = 3
ROPE_BASE = 10000.0
N_EXPERTS = 16
N_EXPERT_GROUPS = 4
PER_GROUP = N_EXPERTS // N_EXPERT_GROUPS
TOP_K = 2
D_EXPERT = 512
EXPERT_BLOCK = 512
N_MOD = 6
LN_EPS = 1e-6
QK_EPS = 1e-6
NEG_INF = -1e30

QA_W, QB_W, QC_W = A_HEADS * HEAD_DIM, B_HEADS * HEAD_DIM, C_HEADS * HEAD_DIM
KA_W, KB_W, KC_W = A_KV * HEAD_DIM, B_HEADS * HEAD_DIM, C_KV * HEAD_DIM
R_QA = 0
R_QB = R_QA + QA_W
R_QC = R_QB + QB_W
R_KA = R_QC + QC_W
R_KB = R_KA + KA_W
R_KC = R_KB + KB_W
R_VA = R_KC + KC_W
R_VB = R_VA + KA_W
R_VC = R_VB + KB_W
PROJ_WIDTH = R_VC + KC_W

V7X_VMEM_LIMIT = 56 * 1024 * 1024
TOK_TILE = 256
TQ_A = 256
TQ_B = 2 * GRID_W
TQ_C = 256
TK_C = 2048
UNROLL_C = 1
MOD_ROWS = 8
LOG2_E = 1.4426950408889634
MAX_SCORE_BOUND = 60.0
BOUND_SLACK = 1.004
TILE_ROWS = 8
BF16_ROWS = 16
LN_ROWS = 64
LN_UNROLL = 4
V_ROWS = HEAD_DIM + BF16_ROWS


def _params(sem):
    return pltpu.CompilerParams(dimension_semantics=sem, vmem_limit_bytes=V7X_VMEM_LIMIT)


def _ada_kernel(c_ref, w_ref, b_ref, o_ref):
    c = c_ref[...]
    act = c * jax.nn.sigmoid(c)
    o_ref[...] = jnp.dot(act, w_ref[...], preferred_element_type=F32,
                         precision=lax.Precision.HIGHEST) + b_ref[...]


def _ada_table(cond, w_ada, b_ada):
    depth, d, n = w_ada.shape
    tn = n // 4
    return pl.pallas_call(
        _ada_kernel,
        grid=(depth, n // tn),
        in_specs=[pl.BlockSpec((MOD_ROWS, d), lambda l, j: (0, 0)),
                  pl.BlockSpec((None, d, tn), lambda l, j: (l, 0, j)),
                  pl.BlockSpec((None, 1, tn), lambda l, j: (l, 0, j))],
        out_specs=pl.BlockSpec((None, MOD_ROWS, tn), lambda l, j: (l, 0, j)),
        out_shape=jax.ShapeDtypeStruct((depth, MOD_ROWS, n), F32),
        compiler_params=_params(("parallel", "parallel")),
    )(cond, w_ada, b_ada.reshape(depth, 1, n))


def _sq_norm(parts):
    return sum(jnp.sum(jnp.square(p.astype(F32)), axis=0, keepdims=True) for p in parts)


def _proj_kernel(x_ref, mod_ref, w_ref, cos_ref, sin_ref, qg_ref, kg_ref,
                 qa_ref, qb_ref, qc_ref, ka_ref, kb_ref, kc_ref, va_ref, vb_ref, vc_ref, qn_ref, kn_ref):
    tm = x_ref.shape[0]
    u = (x_ref[...] * (1.0 + mod_ref[1:2, :]) + mod_ref[0:1, :]).astype(BF16)

    def proj(lo, width):
        return lax.dot_general(w_ref[lo:lo + width, :], u, (((1,), (1,)), ((), ())),
                               preferred_element_type=F32)

    cos = cos_ref[...]
    sin = sin_ref[...]

    def rope(blk):
        x0, x1 = blk[:HALF], blk[HALF:]
        return x0 * cos - x1 * sin, x0 * sin + x1 * cos

    def qk_norm(blk, gain):
        ms = jnp.mean(blk * blk, axis=0, keepdims=True)
        return blk * lax.rsqrt(ms + QK_EPS) * gain

    p = proj(R_QA, QA_W)
    for h in range(A_HEADS):
        o0, o1 = rope(p[h * HEAD_DIM:(h + 1) * HEAD_DIM])
        qa_ref[h * HEAD_DIM:h * HEAD_DIM + HALF, :] = o0.astype(BF16)
        qa_ref[h * HEAD_DIM + HALF:(h + 1) * HEAD_DIM, :] = o1.astype(BF16)
    qb_ref[...] = proj(R_QB, QB_W).astype(BF16)
    p = proj(R_QC, QC_W)
    qg = qg_ref[...]
    for h in range(C_HEADS):
        o0, o1 = [o.astype(BF16) for o in rope(qk_norm(p[h * HEAD_DIM:(h + 1) * HEAD_DIM], qg))]
        qc_ref[h * HEAD_DIM:h * HEAD_DIM + HALF, :] = o0
        qc_ref[h * HEAD_DIM + HALF:(h + 1) * HEAD_DIM, :] = o1
        qn_ref[h:h + 1, :] = _sq_norm([o0, o1])
    qn_ref[C_HEADS:MOD_ROWS, :] = jnp.zeros((MOD_ROWS - C_HEADS, tm), F32)
    p = proj(R_KA, KA_W)
    parts = []
    for h in range(A_KV):
        parts.extend(rope(p[h * HEAD_DIM:(h + 1) * HEAD_DIM]))
    ka_ref[...] = jnp.concatenate(parts, axis=0).T.astype(BF16)
    kb_ref[...] = proj(R_KB, KB_W).T.astype(BF16)
    p = proj(R_KC, KC_W)
    kg = kg_ref[...]
    parts = []
    for h in range(C_KV):
        o0, o1 = [o.astype(BF16) for o in rope(qk_norm(p[h * HEAD_DIM:(h + 1) * HEAD_DIM], kg))]
        kn_ref[h:h + 1, :] = _sq_norm([o0, o1])
        parts.extend([o0.astype(F32), o1.astype(F32)])
    kn_ref[C_KV:MOD_ROWS, :] = jnp.zeros((MOD_ROWS - C_KV, tm), F32)
    kc_ref[:, 0:KC_W] = jnp.concatenate(parts, axis=0).T.astype(BF16)
    kc_ref[:, KC_W:2 * KC_W] = (lax.broadcasted_iota(jnp.int32, (tm, KC_W), 1) == 0).astype(BF16)
    ones_row = (lax.broadcasted_iota(jnp.int32, (BF16_ROWS, tm), 0) == 0).astype(BF16)
    for lo, ref in ((R_VA, va_ref), (R_VC, vc_ref)):
        p = proj(lo, KA_W)
        for g in range(A_KV):
            ref[g, 0:HEAD_DIM, :] = p[g * HEAD_DIM:(g + 1) * HEAD_DIM].astype(BF16)
            ref[g, HEAD_DIM:V_ROWS, :] = ones_row
    vb_ref[...] = proj(R_VB, KB_W).astype(BF16)


def _project(xa, mod, w_t, cos_t, sin_t, q_gain, k_gain, n_lat):
    bsz, s_tot, d = xa.shape
    tm = TOK_TILE
    n_lat_tiles = n_lat // tm
    sd = jax.ShapeDtypeStruct

    def dmajor(width):
        return pl.BlockSpec((None, width, tm), lambda b, i: (b, 0, i))

    def tmajor(width):
        return pl.BlockSpec((None, tm, width), lambda b, i: (b, i, 0))

    aug = pl.BlockSpec((None, A_KV, V_ROWS,tm), lambda b, i: (b, 0, 0, i))
    return pl.pallas_call(
        _proj_kernel,
        grid=(bsz, s_tot // tm),
        in_specs=[tmajor(d),
                  pl.BlockSpec((None, N_MOD, d), lambda b, i: (jnp.where(i >= n_lat_tiles, bsz, b), 0, 0)),
                  pl.BlockSpec((PROJ_WIDTH, d), lambda b, i: (0, 0)),
                  pl.BlockSpec((HALF, tm), lambda b, i: (0, i)),
                  pl.BlockSpec((HALF, tm), lambda b, i: (0, i)),
                  pl.BlockSpec((HEAD_DIM, 1), lambda b, i: (0, 0)),
                  pl.BlockSpec((HEAD_DIM, 1), lambda b, i: (0, 0))],
        out_specs=[dmajor(QA_W), dmajor(QB_W), dmajor(QC_W),
                   tmajor(KA_W), tmajor(KB_W), tmajor(2 * KC_W),
                   aug, dmajor(KB_W), aug, dmajor(MOD_ROWS), dmajor(MOD_ROWS)],
        out_shape=[sd((bsz, QA_W, s_tot), BF16), sd((bsz, QB_W, s_tot), BF16), sd((bsz, QC_W, s_tot), BF16),
                   sd((bsz, s_tot, KA_W), BF16), sd((bsz, s_tot, KB_W), BF16), sd((bsz, s_tot, 2 * KC_W), BF16),
                   sd((bsz, A_KV, V_ROWS,s_tot), BF16), sd((bsz, KB_W, s_tot), BF16),
                   sd((bsz, A_KV, V_ROWS,s_tot), BF16),
                   sd((bsz, MOD_ROWS, s_tot), F32), sd((bsz, MOD_ROWS, s_tot), F32)],
        compiler_params=_params(("parallel", "parallel")),
    )(xa, mod, w_t, cos_t, sin_t, q_gain, k_gain)


def _padded_q(q_ref, h, g):
    blk = q_ref[h * HEAD_DIM:(h + 1) * HEAD_DIM, :]
    zeros = jnp.zeros_like(blk)
    return jnp.concatenate([blk, zeros] if g == 0 else [zeros, blk], axis=0)


def _attn_a_kernel(sink_ref, q_ref, k_ref, v_ref, o_ref, ot_ref, *, n_lat, n_ctx):
    tq = q_ref.shape[1]
    span = tq + 2 * WINDOW
    start = pl.program_id(1) * tq
    ks = pl.multiple_of(jnp.clip(start - WINDOW, 0, n_lat - span), 128)
    kpos = ks + lax.broadcasted_iota(jnp.int32, (span, tq), 0)
    qpos = start + lax.broadcasted_iota(jnp.int32, (span, tq), 1)
    ok = (jnp.abs(qpos - kpos) <= WINDOW) & (qpos < n_lat)
    k_loc = k_ref[pl.ds(ks, span), :]
    k_ctx = k_ref[n_lat:n_lat + n_ctx, :]
    for g in range(A_KV):
        v_loc = v_ref[g, :, pl.ds(ks, span)]
        v_ctx = v_ref[g, :, n_lat:n_lat + n_ctx]
        for hh in range(GQA_GROUP):
            h = g * GQA_GROUP + hh
            qp = _padded_q(q_ref, h, g)
            s_loc = jnp.where(ok, jnp.dot(k_loc, qp, preferred_element_type=F32), NEG_INF)
            s_ctx = jnp.dot(k_ctx, qp, preferred_element_type=F32)
            sink = sink_ref[h]
            m = jnp.maximum(jnp.maximum(jnp.max(s_loc, axis=0, keepdims=True),
                                        jnp.max(s_ctx, axis=0, keepdims=True)), sink)
            p_loc = jnp.exp(s_loc - m).astype(BF16)
            p_ctx = jnp.exp(s_ctx - m).astype(BF16)
            acc = (jnp.dot(v_loc, p_loc, preferred_element_type=F32)
                   + jnp.dot(v_ctx, p_ctx, preferred_element_type=F32))
            denom = acc[HEAD_DIM:HEAD_DIM + 1, :] + jnp.exp(sink - m)
            ot_ref[h * HEAD_DIM:(h + 1) * HEAD_DIM, :] = acc[0:HEAD_DIM, :] / denom
    o_ref[...] = ot_ref[...].T.astype(BF16)


def _attn_a(sink, q_t, k, v_t, n_lat):
    bsz, s_tot, _ = k.shape
    tq = TQ_A
    kern = functools.partial(_attn_a_kernel, n_lat=n_lat, n_ctx=s_tot - n_lat)
    return pl.pallas_call(
        kern,
        grid=(bsz, s_tot // tq),
        in_specs=[pl.BlockSpec(memory_space=pltpu.SMEM),
                  pl.BlockSpec((None, QA_W, tq), lambda b, i: (b, 0, i)),
                  pl.BlockSpec((None, s_tot, KA_W), lambda b, i: (b, 0, 0)),
                  pl.BlockSpec((None, A_KV, V_ROWS,s_tot), lambda b, i: (b, 0, 0, 0))],
        out_specs=pl.BlockSpec((None, tq, QA_W), lambda b, i: (b, i, 0)),
        out_shape=jax.ShapeDtypeStruct((bsz, s_tot, QA_W), BF16),
        scratch_shapes=[pltpu.VMEM((QA_W, tq), F32)],
        compiler_params=_params(("parallel", "arbitrary")),
    )(sink, q_t, k, v_t)


def _nb_plan(n_lat, s_tot):
    rows = n_lat // GRID_W
    kh = min(NA_ROWS, rows)
    q_rows = TQ_B // GRID_W
    w_rows = kh + q_rows
    n_lat_blocks = n_lat // TQ_B
    n_blocks = s_tot // TQ_B
    starts, variants, keys, reps = [], [], {}, []
    for i in range(n_lat_blocks):
        w0 = int(np.clip(q_rows * i - kh // 2, 0, rows - w_rows))
        r0s = tuple(int(np.clip(q_rows * i + j - kh // 2, 0, rows - kh)) - w0 for j in range(q_rows))
        key = (q_rows * i - w0,) + r0s
        if key not in keys:
            keys[key] = len(reps)
            reps.append((i, w0))
        starts.append(w0 * GRID_W)
        variants.append(keys[key])
    masked = len(reps)
    starts.extend([0] * (n_blocks - n_lat_blocks))
    variants.extend([masked] * (n_blocks - n_lat_blocks))
    return rows, kh, w_rows, reps, np.asarray(starts, np.int32), np.asarray(variants, np.int32)


def _rel_bias_kernel(r_ref, e_ref, m_ref, o_ref):
    o_ref[...] = jnp.dot(r_ref[...], e_ref[...], preferred_element_type=F32,
                         precision=lax.Precision.HIGHEST) + m_ref[...]


def _rel_bias_blocks(rpb):
    depth, heads, n_dr, n_dc = rpb.shape
    kc = np.arange(GRID_W)[:, None]
    cq = np.arange(GRID_W)[None, :]
    dc = (kc - cq + NA_COLS - 1).reshape(-1)
    n_dc_pad = 32
    onehot = (np.arange(n_dc_pad)[:, None] == dc[None, :]).astype(np.float32)
    c0 = np.clip(cq - NA_COLS // 2, 0, GRID_W - NA_COLS)
    outside = ~((kc >= c0) & (kc < c0 + NA_COLS))
    col_mask = np.where(outside, NEG_INF, 0.0).astype(np.float32).reshape(1, -1)
    table = jnp.pad(rpb.astype(F32).reshape(depth * heads * n_dr, n_dc), ((0, 0), (0, n_dc_pad - n_dc)))
    blocks = pl.pallas_call(
        _rel_bias_kernel,
        out_shape=jax.ShapeDtypeStruct((depth * heads * n_dr, GRID_W * GRID_W), F32),
    )(table, jnp.asarray(onehot), jnp.asarray(col_mask))
    blocks = blocks.reshape(depth, heads, n_dr, GRID_W, GRID_W)
    return jnp.concatenate([blocks, jnp.full((depth, heads, 1, GRID_W, GRID_W), NEG_INF, F32)], axis=2)


def _nb_bias_tiles(rpb, n_lat, s_tot):
    rows, kh, w_rows, reps, starts, variants = _nb_plan(n_lat, s_tot)
    q_rows = TQ_B // GRID_W
    masked_block = 2 * NA_ROWS - 1
    dr_map = np.full((len(reps) + 1, w_rows, q_rows), masked_block, np.int32)
    for v, (i, w0) in enumerate(reps):
        for a in range(w_rows):
            for b in range(q_rows):
                r = q_rows * i + b
                r0 = int(np.clip(r - kh // 2, 0, rows - kh))
                if r0 <= w0 + a < r0 + kh:
                    dr_map[v, a, b] = w0 + a - r + NA_ROWS - 1
    blocks = _rel_bias_blocks(rpb)[:, :, dr_map]
    depth = rpb.shape[0]
    tiles = jnp.transpose(blocks, (0, 2, 3, 5, 1, 4, 6)).reshape(
        depth, len(reps) + 1, w_rows * GRID_W, B_HEADS * TQ_B)
    return tiles, jnp.asarray(starts), jnp.asarray(variants), w_rows * GRID_W


def _attn_b_kernel(start_ref, var_ref, q_ref, k_ref, v_ref, bias_ref, o_ref, qp_ref, ot_ref, *, n_lat, n_ctx, n_keys):
    tq = q_ref.shape[1]
    ws = pl.multiple_of(start_ref[pl.program_id(1)], 128)
    qp_ref[...] = jnp.zeros_like(qp_ref)
    for h in range(B_HEADS):
        qp_ref[h * HEAD_DIM:(h + 1) * HEAD_DIM, h * tq:(h + 1) * tq] = q_ref[h * HEAD_DIM:(h + 1) * HEAD_DIM, :]
    qp = qp_ref[...]
    s_loc = jnp.dot(k_ref[pl.ds(ws, n_keys), :], qp, preferred_element_type=F32) + bias_ref[...]
    s_ctx = jnp.dot(k_ref[n_lat:n_lat + n_ctx, :], qp, preferred_element_type=F32)
    m = jnp.maximum(jnp.max(s_loc, axis=0, keepdims=True), jnp.max(s_ctx, axis=0, keepdims=True))
    p_loc = jnp.exp(s_loc - m)
    p_ctx = jnp.exp(s_ctx - m)
    denom = jnp.sum(p_loc, axis=0, keepdims=True) + jnp.sum(p_ctx, axis=0, keepdims=True)
    acc = (jnp.dot(v_ref[:, pl.ds(ws, n_keys)], p_loc.astype(BF16), preferred_element_type=F32)
           + jnp.dot(v_ref[:, n_lat:n_lat + n_ctx], p_ctx.astype(BF16), preferred_element_type=F32))
    for h in range(B_HEADS):
        ot_ref[h * HEAD_DIM:(h + 1) * HEAD_DIM, :] = (
            acc[h * HEAD_DIM:(h + 1) * HEAD_DIM, h * tq:(h + 1) * tq] / denom[:, h * tq:(h + 1) * tq])
    o_ref[...] = ot_ref[...].T.astype(BF16)


def _attn_b(q_t, k, v_t, bias_tiles, starts, variants, n_keys, n_lat):
    bsz, s_tot, _ = k.shape
    tq = TQ_B
    kern = functools.partial(_attn_b_kernel, n_lat=n_lat, n_ctx=s_tot - n_lat, n_keys=n_keys)
    grid_spec = pltpu.PrefetchScalarGridSpec(
        num_scalar_prefetch=2,
        grid=(bsz, s_tot // tq),
        in_specs=[pl.BlockSpec((None, QB_W, tq), lambda b, i, st, va: (b, 0, i)),
                  pl.BlockSpec((None, s_tot, KB_W), lambda b, i, st, va: (b, 0, 0)),
                  pl.BlockSpec((None, KB_W, s_tot), lambda b, i, st, va: (b, 0, 0)),
                  pl.BlockSpec((None, n_keys, B_HEADS * tq), lambda b, i, st, va: (va[i], 0, 0))],
        out_specs=pl.BlockSpec((None, tq, QB_W), lambda b, i, st, va: (b, i, 0)),
        scratch_shapes=[pltpu.VMEM((QB_W, B_HEADS * tq), BF16), pltpu.VMEM((QB_W, tq), F32)],
    )
    return pl.pallas_call(
        kern,
        grid_spec=grid_spec,
        out_shape=jax.ShapeDtypeStruct((bsz, s_tot, QB_W), BF16),
        compiler_params=_params(("parallel", "arbitrary")),
    )(starts, variants, q_t, k, v_t, bias_tiles)


def _attn_c_kernel(fast_ref, ksc_ref, q_ref, qn_ref, k_ref, v_ref, o_ref, qp_ref, m_ref, acc_ref, ot_ref,
                   *, n_lat, n_ctx, tk):
    tq = q_ref.shape[1]
    is_ctx = pl.program_id(1) >= n_lat // tq
    n_chunks = jnp.where(is_ctx, 0, n_lat // (tk * UNROLL_C))
    fast = fast_ref[0] == 1
    first_row = lax.broadcasted_iota(jnp.int32, (KC_W, tq), 0) == 0
    for g in range(C_KV):
        k_scale = ksc_ref[pl.program_id(0) * C_KV + g]
        for hh in range(GQA_GROUP):
            h = g * GQA_GROUP + hh
            qp_ref[g, 0:KC_W, hh * tq:(hh + 1) * tq] = _padded_q(q_ref, h, g)
            offset = jnp.where(fast, -jnp.sqrt(qn_ref[h:h + 1, :]) * k_scale, 0.0)
            qp_ref[g, KC_W:2 * KC_W, hh * tq:(hh + 1) * tq] = jnp.where(first_row, offset, 0.0).astype(BF16)
    acc_ref[...] = jnp.zeros_like(acc_ref)

    def bounded_step(g, start, size):
        s = jnp.dot(k_ref[pl.ds(start, size), :], qp_ref[g], preferred_element_type=F32)
        p = jnp.exp2(s).astype(BF16)
        acc_ref[g] += jnp.dot(v_ref[g, :, pl.ds(start, size)], p, preferred_element_type=F32)

    def online_step(g, start, size):
        s = jnp.dot(k_ref[pl.ds(start, size), :], qp_ref[g], preferred_element_type=F32)
        m_prev = m_ref[g]
        m_new = jnp.maximum(m_prev, jnp.max(s, axis=0, keepdims=True))
        p = jnp.exp2(s - m_new).astype(BF16)
        pv = jnp.dot(v_ref[g, :, pl.ds(start, size)], p, preferred_element_type=F32)
        acc_ref[g] = acc_ref[g] * jnp.exp2(m_prev - m_new) + pv
        m_ref[g] = m_new

    def sweep(step):
        def body(c, carry):
            for u in range(UNROLL_C):
                for g in range(C_KV):
                    step(g, pl.multiple_of((c * UNROLL_C + u) * tk, tk), tk)
            return carry

        lax.fori_loop(0, n_chunks, body, 0)
        for g in range(C_KV):
            step(g, n_lat, n_ctx)

    @pl.when(fast)
    def _():
        sweep(bounded_step)

    @pl.when(jnp.logical_not(fast))
    def _():
        m_ref[...] = jnp.full_like(m_ref, NEG_INF)
        sweep(online_step)

    for g in range(C_KV):
        acc = acc_ref[g]
        out = acc[0:HEAD_DIM, :] / acc[HEAD_DIM:HEAD_DIM + 1, :]
        for hh in range(GQA_GROUP):
            h = g * GQA_GROUP + hh
            ot_ref[h * HEAD_DIM:(h + 1) * HEAD_DIM, :] = out[:, hh * tq:(hh + 1) * tq]
    o_ref[...] = ot_ref[...].T.astype(BF16)


def _attn_c(q_t, q_norm2, k, k_norm2, v_t, n_lat):
    bsz, s_tot, _ = k.shape
    tq = TQ_C
    tk = min(TK_C, n_lat)
    k_scale = jnp.sqrt(jnp.max(k_norm2[:, :C_KV, :], axis=-1)) * BOUND_SLACK
    bound = jnp.sqrt(jnp.max(q_norm2)) * jnp.max(k_scale)
    fast = (bound <= MAX_SCORE_BOUND).astype(jnp.int32).reshape(1)
    kern = functools.partial(_attn_c_kernel, n_lat=n_lat, n_ctx=s_tot - n_lat, tk=tk)
    smem = pl.BlockSpec(memory_space=pltpu.SMEM)
    return pl.pallas_call(
        kern,
        grid=(bsz, s_tot // tq),
        in_specs=[smem, smem,
                  pl.BlockSpec((None, QC_W, tq), lambda b, i: (b, 0, i)),
                  pl.BlockSpec((None, MOD_ROWS, tq), lambda b, i: (b, 0, i)),
                  pl.BlockSpec((None, s_tot, 2 * KC_W), lambda b, i: (b, 0, 0)),
                  pl.BlockSpec((None, C_KV, V_ROWS, s_tot), lambda b, i: (b, 0, 0, 0))],
        out_specs=pl.BlockSpec((None, tq, QC_W), lambda b, i: (b, i, 0)),
        out_shape=jax.ShapeDtypeStruct((bsz, s_tot, QC_W), BF16),
        scratch_shapes=[pltpu.VMEM((C_KV, 2 * KC_W, GQA_GROUP * tq), BF16),
                        pltpu.VMEM((C_KV, 1, GQA_GROUP * tq), F32),
                        pltpu.VMEM((C_KV, V_ROWS, GQA_GROUP * tq), F32),
                        pltpu.VMEM((QC_W, tq), F32)],
        compiler_params=_params(("parallel", "arbitrary")),
    )(fast, k_scale.reshape(-1), q_t, q_norm2, k, v_t)


def _store_token_tiles(ref, val):
    n = val.shape[0]
    for c in range(TILE_ROWS):
        ref[pl.ds(c, n, stride=TILE_ROWS), :] = val[:, c * 128:(c + 1) * 128]


def _load_token_tiles(ref, n):
    return jnp.concatenate([ref[pl.ds(c, n, stride=TILE_ROWS), :] for c in range(TILE_ROWS)], axis=1)


def _tile_copy(src_hbm, dst, idx_ref, j, sem):
    src = pl.multiple_of(idx_ref[0, 0, j] * TILE_ROWS, TILE_ROWS)
    return pltpu.make_async_copy(src_hbm.at[pl.ds(src, TILE_ROWS), :],
                                 dst.at[pl.ds(pl.multiple_of(j * TILE_ROWS, TILE_ROWS), TILE_ROWS), :], sem)


def _start_gather(src_hbm, dst, idx_ref, n, sem):
    def issue(j, carry):
        _tile_copy(src_hbm, dst, idx_ref, j, sem).start()
        return carry

    lax.fori_loop(0, n, issue, 0, unroll=8)


def _wait_gather(src_hbm, dst, n, sem):
    pltpu.make_async_copy(src_hbm.at[pl.ds(0, n * TILE_ROWS), :], dst, sem).wait()


def _layer_norm(h, g, b):
    mu = jnp.mean(h, axis=-1, keepdims=True)
    hc = h - mu
    var = jnp.mean(hc * hc, axis=-1, keepdims=True)
    return hc * lax.rsqrt(var + LN_EPS) * g + b


def _post_kernel(oa_ref, ob_ref, oc_ref, x_ref, mod_ref, w_ref, g_ref, b_ref, wr_ref, rb_ref,
                 x1_ref, u2_ref, eid_ref, gate_ref, o_scr, ub_scr, *, alpha):
    tm = x_ref.shape[0]
    o_cat = jnp.concatenate([oa_ref[...], ob_ref[...], oc_ref[...]], axis=1)
    o_scr[...] = jnp.dot(o_cat, w_ref[...], preferred_element_type=F32)
    gate1 = 1.0 + mod_ref[2:3, :]
    scale2 = 1.0 + mod_ref[4:5, :]
    shift2 = mod_ref[3:4, :]
    ln_g = g_ref[...]
    ln_b = b_ref[...]

    def row_group(r, carry):
        r0 = pl.multiple_of(r * LN_ROWS, LN_ROWS)
        rows = pl.ds(r0, LN_ROWS)
        x1 = _layer_norm(alpha * x_ref[rows, :] + gate1 * o_scr[rows, :], ln_g, ln_b)
        x1_ref[rows, :] = x1
        u2 = x1 * scale2 + shift2
        ub_scr[rows, :] = u2.astype(BF16)
        for c in range(TILE_ROWS):
            u2_ref[pl.ds(r0 * TILE_ROWS + c, LN_ROWS, stride=TILE_ROWS), :] = u2[:, c * 128:(c + 1) * 128]
        return carry

    lax.fori_loop(0, tm // LN_ROWS, row_group, 0, unroll=LN_UNROLL)
    logits = lax.dot_general(wr_ref[...], ub_scr[...], (((1,), (1,)), ((), ())), preferred_element_type=F32)
    aff_all = jax.nn.sigmoid(logits)
    sel_all = aff_all + rb_ref[...]
    aff = [aff_all[e:e + 1, :] for e in range(N_EXPERTS)]
    sel = [sel_all[e:e + 1, :] for e in range(N_EXPERTS)]
    gsum = []
    for g in range(N_EXPERT_GROUPS):
        a, b, c, d = sel[g * PER_GROUP:(g + 1) * PER_GROUP]
        hi1, lo1, hi2, lo2 = jnp.maximum(a, b), jnp.minimum(a, b), jnp.maximum(c, d), jnp.minimum(c, d)
        gsum.append(jnp.maximum(hi1, hi2) + jnp.maximum(jnp.minimum(hi1, hi2), jnp.maximum(lo1, lo2)))
    best_g = jnp.zeros_like(gsum[0], dtype=jnp.int32)
    best_v = gsum[0]
    for g in range(1, N_EXPERT_GROUPS):
        better = gsum[g] > best_v
        best_g = jnp.where(better, g, best_g)
        best_v = jnp.where(better, gsum[g], best_v)
    s_loc, a_loc = [], []
    for j in range(PER_GROUP):
        sj, aj = sel[j], aff[j]
        for g in range(1, N_EXPERT_GROUPS):
            pick = best_g == g
            sj = jnp.where(pick, sel[g * PER_GROUP + j], sj)
            aj = jnp.where(pick, aff[g * PER_GROUP + j], aj)
        s_loc.append(sj)
        a_loc.append(aj)
    i1 = jnp.zeros_like(best_g)
    v1, g1 = s_loc[0], a_loc[0]
    for j in range(1, PER_GROUP):
        better = s_loc[j] > v1
        i1 = jnp.where(better, j, i1)
        v1 = jnp.where(better, s_loc[j], v1)
        g1 = jnp.where(better, a_loc[j], g1)
    i2 = jnp.full_like(best_g, -1)
    v2 = jnp.full_like(v1, -jnp.inf)
    g2 = jnp.zeros_like(g1)
    for j in range(PER_GROUP):
        better = (i1 != j) & ((s_loc[j] > v2) | (i2 < 0))
        i2 = jnp.where(better, j, i2)
        v2 = jnp.where(better, s_loc[j], v2)
        g2 = jnp.where(better, a_loc[j], g2)
    total = g1 + g2
    tm = x_ref.shape[0]
    eid_ref[...] = jnp.concatenate([best_g * PER_GROUP + i1, best_g * PER_GROUP + i2,
                                    jnp.zeros((MOD_ROWS - TOP_K, tm), jnp.int32)], axis=0)
    gate_ref[...] = jnp.concatenate([g1 / total, g2 / total, jnp.zeros((MOD_ROWS - TOP_K, tm), F32)], axis=0)


def _post(oa, ob, oc, xa, mod, w_out, ln_g, ln_b, wr_t, rb, n_lat, alpha):
    bsz, s_tot, d = xa.shape
    tm = TOK_TILE
    n_lat_tiles = n_lat // tm
    n_tiles = s_tot // tm
    sd = jax.ShapeDtypeStruct

    def tmajor(width):
        return pl.BlockSpec((None, tm, width), lambda b, i: (b, i, 0))

    def const(shape):
        return pl.BlockSpec(shape, lambda b, i: (0,) * len(shape))

    lanes = pl.BlockSpec((MOD_ROWS, tm), lambda b, i: (0, b * n_tiles + i))
    return pl.pallas_call(
        functools.partial(_post_kernel, alpha=alpha),
        grid=(bsz, n_tiles),
        in_specs=[tmajor(QA_W), tmajor(QB_W), tmajor(QC_W), tmajor(d),
                  pl.BlockSpec((None, N_MOD, d), lambda b, i: (jnp.where(i >= n_lat_tiles, bsz, b), 0, 0)),
                  const((d, d)), const((1, d)), const((1, d)), const((N_EXPERTS, d)), const((N_EXPERTS, 1))],
        out_specs=[tmajor(d), pl.BlockSpec((tm * TILE_ROWS, 128), lambda b, i: (b * n_tiles + i, 0)), lanes, lanes],
        out_shape=[sd((bsz, s_tot, d), F32), sd((bsz * s_tot * TILE_ROWS, 128), F32),
                   sd((MOD_ROWS, bsz * s_tot), jnp.int32), sd((MOD_ROWS, bsz * s_tot), F32)],
        scratch_shapes=[pltpu.VMEM((tm, d), F32), pltpu.VMEM((tm, d), BF16)],
        compiler_params=_params(("parallel", "parallel")),
    )(oa, ob, oc, xa, mod, w_out, ln_g, ln_b, wr_t, rb)


def _ffn_kernel(bexp_ref, nused_ref, tok_ref, tok_next_ref, u_hbm, w1_ref, w3_ref, w2_ref, y_ref,
                xbuf, w1b, w3b, w2b, sem):
    blk = y_ref.shape[0] // TILE_ROWS
    step = pl.program_id(0)
    n_used = nused_ref[0]
    slot = step % 2

    @pl.when(step < n_used)
    def _():
        @pl.when(step == 0)
        def _():
            _start_gather(u_hbm, xbuf.at[0], tok_ref, blk, sem.at[0])

        @pl.when(step + 1 < n_used)
        def _():
            _start_gather(u_hbm, xbuf.at[1 - slot], tok_next_ref, blk, sem.at[1 - slot])

        @pl.when((step == 0) | (bexp_ref[step] != bexp_ref[jnp.maximum(step - 1, 0)]))
        def _():
            w1b[...] = w1_ref[...].astype(BF16)
            w3b[...] = w3_ref[...].astype(BF16)
            w2b[...] = w2_ref[...].astype(BF16)

        _wait_gather(u_hbm, xbuf.at[slot], blk, sem.at[slot])
        xb = _load_token_tiles(xbuf.at[slot], blk).astype(BF16)
        h1 = jnp.dot(xb, w1b[...], preferred_element_type=F32)
        h3 = jnp.dot(xb, w3b[...], preferred_element_type=F32)
        hid = (h1 * jax.nn.sigmoid(h1) * h3).astype(BF16)
        _store_token_tiles(y_ref, jnp.dot(hid, w2b[...], preferred_element_type=F32))

    @pl.when(step >= n_used)
    def _():
        y_ref[...] = jnp.zeros_like(y_ref)


def _expert_ffn(block_expert, n_used, slot_tok, u2_tiles, w1, w3, w2, layer):
    d = w1.shape[2]
    n_blocks = block_expert.shape[0]
    blk = EXPERT_BLOCK
    slot_tok = slot_tok.reshape(n_blocks, 1, blk)
    grid_spec = pltpu.PrefetchScalarGridSpec(
        num_scalar_prefetch=2,
        grid=(n_blocks,),
        in_specs=[pl.BlockSpec((1, 1, blk), lambda i, be, nu: (i, 0, 0), memory_space=pltpu.SMEM),
                  pl.BlockSpec((1, 1, blk), lambda i, be, nu: (jnp.minimum(i + 1, n_blocks - 1), 0, 0),
                               memory_space=pltpu.SMEM),
                  pl.BlockSpec(memory_space=pl.ANY),
                  pl.BlockSpec((None, None, d, D_EXPERT), lambda i, be, nu: (layer, be[i], 0, 0)),
                  pl.BlockSpec((None, None, d, D_EXPERT), lambda i, be, nu: (layer, be[i], 0, 0)),
                  pl.BlockSpec((None, None, D_EXPERT, d), lambda i, be, nu: (layer, be[i], 0, 0))],
        out_specs=pl.BlockSpec((blk * TILE_ROWS, 128), lambda i, be, nu: (i, 0)),
        scratch_shapes=[pltpu.VMEM((2, blk * TILE_ROWS, 128), F32),
                        pltpu.VMEM((d, D_EXPERT), BF16), pltpu.VMEM((d, D_EXPERT), BF16),
                        pltpu.VMEM((D_EXPERT, d), BF16), pltpu.SemaphoreType.DMA((2,))],
    )
    return pl.pallas_call(
        _ffn_kernel,
        grid_spec=grid_spec,
        out_shape=jax.ShapeDtypeStruct((n_blocks * blk * TILE_ROWS, 128), F32),
        compiler_params=_params(("arbitrary",)),
    )(block_expert, n_used, slot_tok, slot_tok, u2_tiles, w1, w3, w2)


def _comb_kernel(pos_ref, pos_next_ref, y_hbm, x1_ref, gate_ref, mod_ref, g_ref, b_ref, o_ref, ybuf, sem, *, alpha):
    tm = x1_ref.shape[0]
    n = TOP_K * tm
    step = pl.program_id(0)
    slot = step % 2

    @pl.when(step == 0)
    def _():
        _start_gather(y_hbm, ybuf.at[0], pos_ref, n, sem.at[0])

    @pl.when(step + 1 < pl.num_programs(0))
    def _():
        _start_gather(y_hbm, ybuf.at[1 - slot], pos_next_ref, n, sem.at[1 - slot])

    _wait_gather(y_hbm, ybuf.at[slot], n, sem.at[slot])
    rows = _load_token_tiles(ybuf.at[slot], n)
    f = gate_ref[:, 0:1] * rows[0:tm, :] + gate_ref[:, 1:2] * rows[tm:n, :]
    o_ref[...] = _layer_norm(alpha * x1_ref[...] + (1.0 + mod_ref[5:6, :]) * f, g_ref[...], b_ref[...])


def _combine(pos, y_tiles, x1, gates, mod, ln_g, ln_b, n_lat, alpha):
    bsz, s_tot, d = x1.shape
    tm = TOK_TILE
    n_lat_tiles = n_lat // tm
    n_tiles = s_tot // tm
    n_steps = bsz * n_tiles

    def mod_row(t):
        return jnp.where(t % n_tiles >= n_lat_tiles, bsz, t // n_tiles)

    out = pl.pallas_call(
        functools.partial(_comb_kernel, alpha=alpha),
        grid=(n_steps,),
        in_specs=[pl.BlockSpec((1, 1, TOP_K * tm), lambda t: (t, 0, 0), memory_space=pltpu.SMEM),
                  pl.BlockSpec((1, 1, TOP_K * tm), lambda t: (jnp.minimum(t + 1, n_steps - 1), 0, 0),
                               memory_space=pltpu.SMEM),
                  pl.BlockSpec(memory_space=pl.ANY),
                  pl.BlockSpec((tm, d), lambda t: (t, 0)),
                  pl.BlockSpec((tm, TOP_K), lambda t: (t, 0)),
                  pl.BlockSpec((None, N_MOD, d), lambda t: (mod_row(t), 0, 0)),
                  pl.BlockSpec((1, d), lambda t: (0, 0)),
                  pl.BlockSpec((1, d), lambda t: (0, 0))],
        out_specs=pl.BlockSpec((tm, d), lambda t: (t, 0)),
        out_shape=jax.ShapeDtypeStruct((bsz * s_tot, d), F32),
        scratch_shapes=[pltpu.VMEM((2, TOP_K * tm * TILE_ROWS, 128), F32), pltpu.SemaphoreType.DMA((2,))],
        compiler_params=_params(("arbitrary",)),
    )(pos, pos, y_tiles, x1.reshape(bsz * s_tot, d), gates, mod, ln_g, ln_b)
    return out.reshape(bsz, s_tot, d)


def _deinterleave(n_heads):
    one = np.concatenate([np.arange(0, HEAD_DIM, 2), np.arange(1, HEAD_DIM, 2)])
    return np.concatenate([h * HEAD_DIM + one for h in range(n_heads)])


def _proj_weight(w_in_l):
    sizes = (QA_W, KA_W, KA_W, QB_W, KB_W, KB_W, QC_W, KC_W, KC_W)
    offs = np.concatenate([[0], np.cumsum(sizes)])
    qa, ka, va, qb, kb, vb, qc, kc, vc = [w_in_l[:, offs[i]:offs[i + 1]] for i in range(len(sizes))]
    scale = HEAD_DIM ** -0.5
    qa = qa[:, _deinterleave(A_HEADS)] * scale
    ka = ka[:, _deinterleave(A_KV)]
    qb = qb * scale
    qc = qc[:, _deinterleave(C_HEADS)]
    kc = kc[:, _deinterleave(C_KV)]
    return jnp.concatenate([qa, qb, qc, ka, kb, kc, va, vb, vc], axis=1).T.astype(BF16)


def _rope_tables(n_lat, n_ctx):
    t = np.arange(n_lat)
    inv_freq = ROPE_BASE ** (-np.arange(0, HALF, 2, dtype=np.float32) / HALF)
    ang = jnp.concatenate([jnp.asarray(t // GRID_W, F32)[None, :] * jnp.asarray(inv_freq)[:, None],
                           jnp.asarray(t % GRID_W, F32)[None, :] * jnp.asarray(inv_freq)[:, None]], axis=0)
    cos = jnp.concatenate([jnp.cos(ang), jnp.ones((HALF, n_ctx), F32)], axis=1)
    sin = jnp.concatenate([jnp.sin(ang), jnp.zeros((HALF, n_ctx), F32)], axis=1)
    return cos, sin


def _dispatch(eid, n_tok):
    n_assign = n_tok * TOP_K
    flat_e = eid.T.reshape(-1)
    onehot = (flat_e[:, None] == jnp.arange(N_EXPERTS, dtype=jnp.int32)[None, :]).astype(jnp.int32)
    csum = jnp.cumsum(onehot, axis=0)
    rank = jnp.take_along_axis(csum, flat_e[:, None], axis=1)[:, 0] - 1
    counts = csum[-1]
    padded = (counts + EXPERT_BLOCK - 1) // EXPERT_BLOCK * EXPERT_BLOCK
    pad_ends = jnp.cumsum(padded)
    dest = (pad_ends - padded)[flat_e] + rank
    n_blocks = -(-n_assign // EXPERT_BLOCK) + N_EXPERTS
    tok = jnp.arange(n_assign, dtype=jnp.int32) // TOP_K
    slot_tok = jnp.zeros((n_blocks * EXPERT_BLOCK,), jnp.int32).at[dest].set(tok)
    block_start = jnp.arange(n_blocks, dtype=jnp.int32) * EXPERT_BLOCK
    block_expert = jnp.minimum(jnp.sum((block_start[:, None] >= pad_ends[None, :]).astype(jnp.int32), axis=1),
                               N_EXPERTS - 1)
    n_used = (pad_ends[-1] // EXPERT_BLOCK).astype(jnp.int32).reshape(1)
    return slot_tok, block_expert, dest.reshape(n_tok, TOP_K).astype(jnp.int32), n_used


def kernel(x, c, ctx, c_ctx, w_in, w_out, sink, rpb, q_gain, k_gain, w_ada, b_ada,
           ln1_g, ln1_b, ln2_g, ln2_b, w_router, router_bias, w1, w3, w2):
    bsz, n_lat, d = x.shape
    n_ctx = ctx.shape[1]
    s_tot = n_lat + n_ctx
    depth = w_in.shape[0]
    assert d == D_MODEL and n_ctx == TOK_TILE and n_lat % min(TK_C, n_lat) == 0
    assert n_lat % (2 * TOK_TILE) == 0 and bsz + 1 <= MOD_ROWS
    alpha = float((2 * depth) ** 0.25)
    n_tok = bsz * s_tot
    tm = TOK_TILE

    cond = jnp.zeros((MOD_ROWS, d), F32).at[:bsz].set(c).at[bsz].set(c_ctx)
    mods = _ada_table(cond, w_ada, b_ada).reshape(depth, MOD_ROWS, N_MOD, d)
    cos_t, sin_t = _rope_tables(n_lat, n_ctx)
    gain_perm = _deinterleave(1)
    wr_t = w_router.T.astype(BF16)
    rb = router_bias.reshape(N_EXPERTS, 1).astype(F32)

    bias_tiles, starts, variants, n_keys = _nb_bias_tiles(rpb, n_lat, s_tot)

    xa = jnp.concatenate([x, ctx], axis=1)
    for l in range(depth):
        mod = mods[l]
        qa, qb, qc, ka, kb, kc, va, vb, vc, qn, kn = _project(
            xa, mod, _proj_weight(w_in[l]), cos_t, sin_t,
            q_gain[l][gain_perm].reshape(HEAD_DIM, 1) * (HEAD_DIM ** -0.5 * LOG2_E),
            k_gain[l][gain_perm].reshape(HEAD_DIM, 1), n_lat)
        oa = _attn_a(sink[l], qa, ka, va, n_lat)
        ob = _attn_b(qb, kb, vb, bias_tiles[l], starts, variants, n_keys, n_lat)
        oc = _attn_c(qc, qn, kc, kn, vc, n_lat)
        x1, u2, eid8, gate8 = _post(oa, ob, oc, xa, mod, w_out[l].astype(BF16),
                                    ln1_g[l].reshape(1, d), ln1_b[l].reshape(1, d), wr_t, rb, n_lat, alpha)
        slot_tok, block_expert, dest, n_used = _dispatch(eid8[:TOP_K], n_tok)
        y = _expert_ffn(block_expert, n_used, slot_tok, u2, w1, w3, w2, l)
        pos = dest.reshape(n_tok // tm, tm, TOP_K).transpose(0, 2, 1).reshape(n_tok // tm, 1, TOP_K * tm)
        xa = _combine(pos, y, x1, gate8[:TOP_K].T, mod, ln2_g[l].reshape(1, d), ln2_b[l].reshape(1, d),
                      n_lat, alpha)
    return xa[:, :n_lat]
```

```python
import functools

import numpy as np
import jax
import jax.numpy as jnp
from jax import lax
from jax.experimental import pallas as pl
from jax.experimental.pallas import tpu as pltpu

F32 = jnp.float32
BF16 = jnp.bfloat16

D_MODEL = 1024
HEAD_DIM = 64
HALF = HEAD_DIM // 2
GRID_W = 64
A_HEADS, A_KV = 6, 2
WINDOW = 128
B_HEADS = 4
NA_ROWS, NA_COLS = 8, 16
C_HEADS, C_KV = 6, 2
GQA_GROUP = 3
ROPE_BASE = 10000.0
N_EXPERTS = 16
N_EXPERT_GROUPS = 4
PER_GROUP = N_EXPERTS // N_EXPERT_GROUPS
TOP_K = 2
D_EXPERT = 512
PAIRS_PER_GROUP = PER_GROUP * (PER_GROUP - 1) // 2
N_CLASSES = N_EXPERT_GROUPS * PAIRS_PER_GROUP
CLASS_BLOCK = 256
N_MOD = 6
LN_EPS = 1e-6
QK_EPS = 1e-6
NEG_INF = -1e30

QA_W, QB_W, QC_W = A_HEADS * HEAD_DIM, B_HEADS * HEAD_DIM, C_HEADS * HEAD_DIM
KA_W, KB_W, KC_W = A_KV * HEAD_DIM, B_HEADS * HEAD_DIM, C_KV * HEAD_DIM
R_QA = 0
R_QB = R_QA + QA_W
R_QC = R_QB + QB_W
R_KA = R_QC + QC_W
R_KB = R_KA + KA_W
R_KC = R_KB + KB_W
R_VA = R_KC + KC_W
R_VB = R_VA + KA_W
R_VC = R_VB + KB_W
PROJ_WIDTH = R_VC + KC_W

V7X_VMEM_LIMIT = 56 * 1024 * 1024
TOK_TILE = 256
TQ_A = 256
TQ_B = 2 * GRID_W
TQ_C = 256
TK_C = 2048
UNROLL_C = 1
MOD_ROWS = 8
LOG2_E = 1.4426950408889634
MAX_SCORE_BOUND = 60.0
BOUND_SLACK = 1.004
TILE_ROWS = 8
BF16_ROWS = 16
LN_ROWS = 64
LN_UNROLL = 4
V_ROWS = HEAD_DIM + BF16_ROWS


def _params(sem):
    return pltpu.CompilerParams(dimension_semantics=sem, vmem_limit_bytes=V7X_VMEM_LIMIT)


def _ada_kernel(c_ref, w_ref, b_ref, o_ref):
    c = c_ref[...]
    act = c * jax.nn.sigmoid(c)
    o_ref[...] = jnp.dot(act, w_ref[...], preferred_element_type=F32,
                         precision=lax.Precision.HIGHEST) + b_ref[...]


def _ada_table(cond, w_ada, b_ada):
    depth, d, n = w_ada.shape
    tn = n // 4
    return pl.pallas_call(
        _ada_kernel,
        grid=(depth, n // tn),
        in_specs=[pl.BlockSpec((MOD_ROWS, d), lambda l, j: (0, 0)),
                  pl.BlockSpec((None, d, tn), lambda l, j: (l, 0, j)),
                  pl.BlockSpec((None, 1, tn), lambda l, j: (l, 0, j))],
        out_specs=pl.BlockSpec((None, MOD_ROWS, tn), lambda l, j: (l, 0, j)),
        out_shape=jax.ShapeDtypeStruct((depth, MOD_ROWS, n), F32),
        compiler_params=_params(("parallel", "parallel")),
    )(cond, w_ada, b_ada.reshape(depth, 1, n))


def _sq_norm(parts):
    return sum(jnp.sum(jnp.square(p.astype(F32)), axis=0, keepdims=True) for p in parts)


def _proj_kernel(x_ref, mod_ref, w_ref, cos_ref, sin_ref, qg_ref, kg_ref,
                 qa_ref, qb_ref, qc_ref, ka_ref, kb_ref, kc_ref, va_ref, vb_ref, vc_ref, qn_ref, kn_ref):
    tm = x_ref.shape[0]
    u = (x_ref[...] * (1.0 + mod_ref[1:2, :]) + mod_ref[0:1, :]).astype(BF16)

    def proj(lo, width):
        return lax.dot_general(w_ref[lo:lo + width, :], u, (((1,), (1,)), ((), ())),
                               preferred_element_type=F32)

    cos = cos_ref[...]
    sin = sin_ref[...]

    def rope(blk):
        x0, x1 = blk[:HALF], blk[HALF:]
        return x0 * cos - x1 * sin, x0 * sin + x1 * cos

    def qk_norm(blk, gain):
        ms = jnp.mean(blk * blk, axis=0, keepdims=True)
        return blk * lax.rsqrt(ms + QK_EPS) * gain

    p = proj(R_QA, QA_W)
    for h in range(A_HEADS):
        o0, o1 = rope(p[h * HEAD_DIM:(h + 1) * HEAD_DIM])
        qa_ref[h * HEAD_DIM:h * HEAD_DIM + HALF, :] = o0.astype(BF16)
        qa_ref[h * HEAD_DIM + HALF:(h + 1) * HEAD_DIM, :] = o1.astype(BF16)
    qb_ref[...] = proj(R_QB, QB_W).astype(BF16)
    p = proj(R_QC, QC_W)
    qg = qg_ref[...]
    for h in range(C_HEADS):
        o0, o1 = [o.astype(BF16) for o in rope(qk_norm(p[h * HEAD_DIM:(h + 1) * HEAD_DIM], qg))]
        qc_ref[h * HEAD_DIM:h * HEAD_DIM + HALF, :] = o0
        qc_ref[h * HEAD_DIM + HALF:(h + 1) * HEAD_DIM, :] = o1
        qn_ref[h:h + 1, :] = _sq_norm([o0, o1])
    qn_ref[C_HEADS:MOD_ROWS, :] = jnp.zeros((MOD_ROWS - C_HEADS, tm), F32)
    p = proj(R_KA, KA_W)
    parts = []
    for h in range(A_KV):
        parts.extend(rope(p[h * HEAD_DIM:(h + 1) * HEAD_DIM]))
    ka_ref[...] = jnp.concatenate(parts, axis=0).T.astype(BF16)
    kb_ref[...] = proj(R_KB, KB_W).T.astype(BF16)
    p = proj(R_KC, KC_W)
    kg = kg_ref[...]
    parts = []
    for h in range(C_KV):
        o0, o1 = [o.astype(BF16) for o in rope(qk_norm(p[h * HEAD_DIM:(h + 1) * HEAD_DIM], kg))]
        kn_ref[h:h + 1, :] = _sq_norm([o0, o1])
        parts.extend([o0.astype(F32), o1.astype(F32)])
    kn_ref[C_KV:MOD_ROWS, :] = jnp.zeros((MOD_ROWS - C_KV, tm), F32)
    kc_ref[:, 0:KC_W] = jnp.concatenate(parts, axis=0).T.astype(BF16)
    kc_ref[:, KC_W:2 * KC_W] = (lax.broadcasted_iota(jnp.int32, (tm, KC_W), 1) == 0).astype(BF16)
    ones_row = (lax.broadcasted_iota(jnp.int32, (BF16_ROWS, tm), 0) == 0).astype(BF16)
    for lo, ref in ((R_VA, va_ref), (R_VC, vc_ref)):
        p = proj(lo, KA_W)
        for g in range(A_KV):
            ref[g, 0:HEAD_DIM, :] = p[g * HEAD_DIM:(g + 1) * HEAD_DIM].astype(BF16)
            ref[g, HEAD_DIM:V_ROWS, :] = ones_row
    vb_ref[...] = proj(R_VB, KB_W).astype(BF16)


def _project(xa, mod, w_t, cos_t, sin_t, q_gain, k_gain, n_lat):
    bsz, s_tot, d = xa.shape
    tm = TOK_TILE
    n_lat_tiles = n_lat // tm
    sd = jax.ShapeDtypeStruct

    def dmajor(width):
        return pl.BlockSpec((None, width, tm), lambda b, i: (b, 0, i))

    def tmajor(width):
        return pl.BlockSpec((None, tm, width), lambda b, i: (b, i, 0))

    aug = pl.BlockSpec((None, A_KV, V_ROWS,tm), lambda b, i: (b, 0, 0, i))
    return pl.pallas_call(
        _proj_kernel,
        grid=(bsz, s_tot // tm),
        in_specs=[tmajor(d),
                  pl.BlockSpec((None, N_MOD, d), lambda b, i: (jnp.where(i >= n_lat_tiles, bsz, b), 0, 0)),
                  pl.BlockSpec((PROJ_WIDTH, d), lambda b, i: (0, 0)),
                  pl.BlockSpec((HALF, tm), lambda b, i: (0, i)),
                  pl.BlockSpec((HALF, tm), lambda b, i: (0, i)),
                  pl.BlockSpec((HEAD_DIM, 1), lambda b, i: (0, 0)),
                  pl.BlockSpec((HEAD_DIM, 1), lambda b, i: (0, 0))],
        out_specs=[dmajor(QA_W), dmajor(QB_W), dmajor(QC_W),
                   tmajor(KA_W), tmajor(KB_W), tmajor(2 * KC_W),
                   aug, dmajor(KB_W), aug, dmajor(MOD_ROWS), dmajor(MOD_ROWS)],
        out_shape=[sd((bsz, QA_W, s_tot), BF16), sd((bsz, QB_W, s_tot), BF16), sd((bsz, QC_W, s_tot), BF16),
                   sd((bsz, s_tot, KA_W), BF16), sd((bsz, s_tot, KB_W), BF16), sd((bsz, s_tot, 2 * KC_W), BF16),
                   sd((bsz, A_KV, V_ROWS,s_tot), BF16), sd((bsz, KB_W, s_tot), BF16),
                   sd((bsz, A_KV, V_ROWS,s_tot), BF16),
                   sd((bsz, MOD_ROWS, s_tot), F32), sd((bsz, MOD_ROWS, s_tot), F32)],
        compiler_params=_params(("parallel", "parallel")),
    )(xa, mod, w_t, cos_t, sin_t, q_gain, k_gain)


def _padded_q(q_ref, h, g):
    blk = q_ref[h * HEAD_DIM:(h + 1) * HEAD_DIM, :]
    zeros = jnp.zeros_like(blk)
    return jnp.concatenate([blk, zeros] if g == 0 else [zeros, blk], axis=0)


def _attn_a_kernel(sink_ref, q_ref, k_ref, v_ref, o_ref, ot_ref, *, n_lat, n_ctx):
    tq = q_ref.shape[1]
    span = tq + 2 * WINDOW
    start = pl.program_id(1) * tq
    ks = pl.multiple_of(jnp.clip(start - WINDOW, 0, n_lat - span), 128)
    kpos = ks + lax.broadcasted_iota(jnp.int32, (span, tq), 0)
    qpos = start + lax.broadcasted_iota(jnp.int32, (span, tq), 1)
    ok = (jnp.abs(qpos - kpos) <= WINDOW) & (qpos < n_lat)
    k_loc = k_ref[pl.ds(ks, span), :]
    k_ctx = k_ref[n_lat:n_lat + n_ctx, :]
    for g in range(A_KV):
        v_loc = v_ref[g, :, pl.ds(ks, span)]
        v_ctx = v_ref[g, :, n_lat:n_lat + n_ctx]
        for hh in range(GQA_GROUP):
            h = g * GQA_GROUP + hh
            qp = _padded_q(q_ref, h, g)
            s_loc = jnp.where(ok, jnp.dot(k_loc, qp, preferred_element_type=F32), NEG_INF)
            s_ctx = jnp.dot(k_ctx, qp, preferred_element_type=F32)
            sink = sink_ref[h]
            m = jnp.maximum(jnp.maximum(jnp.max(s_loc, axis=0, keepdims=True),
                                        jnp.max(s_ctx, axis=0, keepdims=True)), sink)
            p_loc = jnp.exp(s_loc - m).astype(BF16)
            p_ctx = jnp.exp(s_ctx - m).astype(BF16)
            acc = (jnp.dot(v_loc, p_loc, preferred_element_type=F32)
                   + jnp.dot(v_ctx, p_ctx, preferred_element_type=F32))
            denom = acc[HEAD_DIM:HEAD_DIM + 1, :] + jnp.exp(sink - m)
            ot_ref[h * HEAD_DIM:(h + 1) * HEAD_DIM, :] = acc[0:HEAD_DIM, :] / denom
    o_ref[...] = ot_ref[...].T.astype(BF16)


def _attn_a(sink, q_t, k, v_t, n_lat):
    bsz, s_tot, _ = k.shape
    tq = TQ_A
    kern = functools.partial(_attn_a_kernel, n_lat=n_lat, n_ctx=s_tot - n_lat)
    return pl.pallas_call(
        kern,
        grid=(bsz, s_tot // tq),
        in_specs=[pl.BlockSpec(memory_space=pltpu.SMEM),
                  pl.BlockSpec((None, QA_W, tq), lambda b, i: (b, 0, i)),
                  pl.BlockSpec((None, s_tot, KA_W), lambda b, i: (b, 0, 0)),
                  pl.BlockSpec((None, A_KV, V_ROWS,s_tot), lambda b, i: (b, 0, 0, 0))],
        out_specs=pl.BlockSpec((None, tq, QA_W), lambda b, i: (b, i, 0)),
        out_shape=jax.ShapeDtypeStruct((bsz, s_tot, QA_W), BF16),
        scratch_shapes=[pltpu.VMEM((QA_W, tq), F32)],
        compiler_params=_params(("parallel", "arbitrary")),
    )(sink, q_t, k, v_t)


def _nb_plan(n_lat, s_tot):
    rows = n_lat // GRID_W
    kh = min(NA_ROWS, rows)
    q_rows = TQ_B // GRID_W
    w_rows = kh + q_rows
    n_lat_blocks = n_lat // TQ_B
    n_blocks = s_tot // TQ_B
    starts, variants, keys, reps = [], [], {}, []
    for i in range(n_lat_blocks):
        w0 = int(np.clip(q_rows * i - kh // 2, 0, rows - w_rows))
        r0s = tuple(int(np.clip(q_rows * i + j - kh // 2, 0, rows - kh)) - w0 for j in range(q_rows))
        key = (q_rows * i - w0,) + r0s
        if key not in keys:
            keys[key] = len(reps)
            reps.append((i, w0))
        starts.append(w0 * GRID_W)
        variants.append(keys[key])
    masked = len(reps)
    starts.extend([0] * (n_blocks - n_lat_blocks))
    variants.extend([masked] * (n_blocks - n_lat_blocks))
    return rows, kh, w_rows, reps, np.asarray(starts, np.int32), np.asarray(variants, np.int32)


def _rel_bias_kernel(r_ref, e_ref, m_ref, o_ref):
    o_ref[...] = jnp.dot(r_ref[...], e_ref[...], preferred_element_type=F32,
                         precision=lax.Precision.HIGHEST) + m_ref[...]


def _rel_bias_blocks(rpb):
    depth, heads, n_dr, n_dc = rpb.shape
    kc = np.arange(GRID_W)[:, None]
    cq = np.arange(GRID_W)[None, :]
    dc = (kc - cq + NA_COLS - 1).reshape(-1)
    n_dc_pad = 32
    onehot = (np.arange(n_dc_pad)[:, None] == dc[None, :]).astype(np.float32)
    c0 = np.clip(cq - NA_COLS // 2, 0, GRID_W - NA_COLS)
    outside = ~((kc >= c0) & (kc < c0 + NA_COLS))
    col_mask = np.where(outside, NEG_INF, 0.0).astype(np.float32).reshape(1, -1)
    table = jnp.pad(rpb.astype(F32).reshape(depth * heads * n_dr, n_dc), ((0, 0), (0, n_dc_pad - n_dc)))
    blocks = pl.pallas_call(
        _rel_bias_kernel,
        out_shape=jax.ShapeDtypeStruct((depth * heads * n_dr, GRID_W * GRID_W), F32),
    )(table, jnp.asarray(onehot), jnp.asarray(col_mask))
    blocks = blocks.reshape(depth, heads, n_dr, GRID_W, GRID_W)
    return jnp.concatenate([blocks, jnp.full((depth, heads, 1, GRID_W, GRID_W), NEG_INF, F32)], axis=2)


def _nb_bias_tiles(rpb, n_lat, s_tot):
    rows, kh, w_rows, reps, starts, variants = _nb_plan(n_lat, s_tot)
    q_rows = TQ_B // GRID_W
    masked_block = 2 * NA_ROWS - 1
    dr_map = np.full((len(reps) + 1, w_rows, q_rows), masked_block, np.int32)
    for v, (i, w0) in enumerate(reps):
        for a in range(w_rows):
            for b in range(q_rows):
                r = q_rows * i + b
                r0 = int(np.clip(r - kh // 2, 0, rows - kh))
                if r0 <= w0 + a < r0 + kh:
                    dr_map[v, a, b] = w0 + a - r + NA_ROWS - 1
    blocks = _rel_bias_blocks(rpb)[:, :, dr_map]
    depth = rpb.shape[0]
    tiles = jnp.transpose(blocks, (0, 2, 3, 5, 1, 4, 6)).reshape(
        depth, len(reps) + 1, w_rows * GRID_W, B_HEADS * TQ_B)
    return tiles, jnp.asarray(starts), jnp.asarray(variants), w_rows * GRID_W


def _attn_b_kernel(start_ref, var_ref, q_ref, k_ref, v_ref, bias_ref, o_ref, qp_ref, ot_ref, *, n_lat, n_ctx, n_keys):
    tq = q_ref.shape[1]
    ws = pl.multiple_of(start_ref[pl.program_id(1)], 128)
    qp_ref[...] = jnp.zeros_like(qp_ref)
    for h in range(B_HEADS):
        qp_ref[h * HEAD_DIM:(h + 1) * HEAD_DIM, h * tq:(h + 1) * tq] = q_ref[h * HEAD_DIM:(h + 1) * HEAD_DIM, :]
    qp = qp_ref[...]
    s_loc = jnp.dot(k_ref[pl.ds(ws, n_keys), :], qp, preferred_element_type=F32) + bias_ref[...]
    s_ctx = jnp.dot(k_ref[n_lat:n_lat + n_ctx, :], qp, preferred_element_type=F32)
    m = jnp.maximum(jnp.max(s_loc, axis=0, keepdims=True), jnp.max(s_ctx, axis=0, keepdims=True))
    p_loc = jnp.exp(s_loc - m)
    p_ctx = jnp.exp(s_ctx - m)
    denom = jnp.sum(p_loc, axis=0, keepdims=True) + jnp.sum(p_ctx, axis=0, keepdims=True)
    acc = (jnp.dot(v_ref[:, pl.ds(ws, n_keys)], p_loc.astype(BF16), preferred_element_type=F32)
           + jnp.dot(v_ref[:, n_lat:n_lat + n_ctx], p_ctx.astype(BF16), preferred_element_type=F32))
    for h in range(B_HEADS):
        ot_ref[h * HEAD_DIM:(h + 1) * HEAD_DIM, :] = (
            acc[h * HEAD_DIM:(h + 1) * HEAD_DIM, h * tq:(h + 1) * tq] / denom[:, h * tq:(h + 1) * tq])
    o_ref[...] = ot_ref[...].T.astype(BF16)


def _attn_b(q_t, k, v_t, bias_tiles, starts, variants, n_keys, n_lat):
    bsz, s_tot, _ = k.shape
    tq = TQ_B
    kern = functools.partial(_attn_b_kernel, n_lat=n_lat, n_ctx=s_tot - n_lat, n_keys=n_keys)
    grid_spec = pltpu.PrefetchScalarGridSpec(
        num_scalar_prefetch=2,
        grid=(bsz, s_tot // tq),
        in_specs=[pl.BlockSpec((None, QB_W, tq), lambda b, i, st, va: (b, 0, i)),
                  pl.BlockSpec((None, s_tot, KB_W), lambda b, i, st, va: (b, 0, 0)),
                  pl.BlockSpec((None, KB_W, s_tot), lambda b, i, st, va: (b, 0, 0)),
                  pl.BlockSpec((None, n_keys, B_HEADS * tq), lambda b, i, st, va: (va[i], 0, 0))],
        out_specs=pl.BlockSpec((None, tq, QB_W), lambda b, i, st, va: (b, i, 0)),
        scratch_shapes=[pltpu.VMEM((QB_W, B_HEADS * tq), BF16), pltpu.VMEM((QB_W, tq), F32)],
    )
    return pl.pallas_call(
        kern,
        grid_spec=grid_spec,
        out_shape=jax.ShapeDtypeStruct((bsz, s_tot, QB_W), BF16),
        compiler_params=_params(("parallel", "arbitrary")),
    )(starts, variants, q_t, k, v_t, bias_tiles)


def _attn_c_kernel(fast_ref, ksc_ref, q_ref, qn_ref, k_ref, v_ref, o_ref, qp_ref, m_ref, acc_ref, ot_ref,
                   *, n_lat, n_ctx, tk):
    tq = q_ref.shape[1]
    is_ctx = pl.program_id(1) >= n_lat // tq
    n_chunks = jnp.where(is_ctx, 0, n_lat // (tk * UNROLL_C))
    fast = fast_ref[0] == 1
    first_row = lax.broadcasted_iota(jnp.int32, (KC_W, tq), 0) == 0
    for g in range(C_KV):
        k_scale = ksc_ref[pl.program_id(0) * C_KV + g]
        for hh in range(GQA_GROUP):
            h = g * GQA_GROUP + hh
            qp_ref[g, 0:KC_W, hh * tq:(hh + 1) * tq] = _padded_q(q_ref, h, g)
            offset = jnp.where(fast, -jnp.sqrt(qn_ref[h:h + 1, :]) * k_scale, 0.0)
            qp_ref[g, KC_W:2 * KC_W, hh * tq:(hh + 1) * tq] = jnp.where(first_row, offset, 0.0).astype(BF16)
    acc_ref[...] = jnp.zeros_like(acc_ref)

    def bounded_step(g, start, size):
        s = jnp.dot(k_ref[pl.ds(start, size), :], qp_ref[g], preferred_element_type=F32)
        p = jnp.exp2(s).astype(BF16)
        acc_ref[g] += jnp.dot(v_ref[g, :, pl.ds(start, size)], p, preferred_element_type=F32)

    def online_step(g, start, size):
        s = jnp.dot(k_ref[pl.ds(start, size), :], qp_ref[g], preferred_element_type=F32)
        m_prev = m_ref[g]
        m_new = jnp.maximum(m_prev, jnp.max(s, axis=0, keepdims=True))
        p = jnp.exp2(s - m_new).astype(BF16)
        pv = jnp.dot(v_ref[g, :, pl.ds(start, size)], p, preferred_element_type=F32)
        acc_ref[g] = acc_ref[g] * jnp.exp2(m_prev - m_new) + pv
        m_ref[g] = m_new

    def sweep(step):
        def body(c, carry):
            for u in range(UNROLL_C):
                for g in range(C_KV):
                    step(g, pl.multiple_of((c * UNROLL_C + u) * tk, tk), tk)
            return carry

        lax.fori_loop(0, n_chunks, body, 0)
        for g in range(C_KV):
            step(g, n_lat, n_ctx)

    @pl.when(fast)
    def _():
        sweep(bounded_step)

    @pl.when(jnp.logical_not(fast))
    def _():
        m_ref[...] = jnp.full_like(m_ref, NEG_INF)
        sweep(online_step)

    for g in range(C_KV):
        acc = acc_ref[g]
        out = acc[0:HEAD_DIM, :] / acc[HEAD_DIM:HEAD_DIM + 1, :]
        for hh in range(GQA_GROUP):
            h = g * GQA_GROUP + hh
            ot_ref[h * HEAD_DIM:(h + 1) * HEAD_DIM, :] = out[:, hh * tq:(hh + 1) * tq]
    o_ref[...] = ot_ref[...].T.astype(BF16)


def _attn_c(q_t, q_norm2, k, k_norm2, v_t, n_lat):
    bsz, s_tot, _ = k.shape
    tq = TQ_C
    tk = min(TK_C, n_lat)
    k_scale = jnp.sqrt(jnp.max(k_norm2[:, :C_KV, :], axis=-1)) * BOUND_SLACK
    bound = jnp.sqrt(jnp.max(q_norm2)) * jnp.max(k_scale)
    fast = (bound <= MAX_SCORE_BOUND).astype(jnp.int32).reshape(1)
    kern = functools.partial(_attn_c_kernel, n_lat=n_lat, n_ctx=s_tot - n_lat, tk=tk)
    smem = pl.BlockSpec(memory_space=pltpu.SMEM)
    return pl.pallas_call(
        kern,
        grid=(bsz, s_tot // tq),
        in_specs=[smem, smem,
                  pl.BlockSpec((None, QC_W, tq), lambda b, i: (b, 0, i)),
                  pl.BlockSpec((None, MOD_ROWS, tq), lambda b, i: (b, 0, i)),
                  pl.BlockSpec((None, s_tot, 2 * KC_W), lambda b, i: (b, 0, 0)),
                  pl.BlockSpec((None, C_KV, V_ROWS, s_tot), lambda b, i: (b, 0, 0, 0))],
        out_specs=pl.BlockSpec((None, tq, QC_W), lambda b, i: (b, i, 0)),
        out_shape=jax.ShapeDtypeStruct((bsz, s_tot, QC_W), BF16),
        scratch_shapes=[pltpu.VMEM((C_KV, 2 * KC_W, GQA_GROUP * tq), BF16),
                        pltpu.VMEM((C_KV, 1, GQA_GROUP * tq), F32),
                        pltpu.VMEM((C_KV, V_ROWS, GQA_GROUP * tq), F32),
                        pltpu.VMEM((QC_W, tq), F32)],
        compiler_params=_params(("parallel", "arbitrary")),
    )(fast, k_scale.reshape(-1), q_t, q_norm2, k, v_t)


def _store_token_tiles(ref, val, pitch=TILE_ROWS, first=0):
    n = val.shape[0]
    for c in range(TILE_ROWS):
        ref[pl.ds(first + c, n, stride=pitch), :] = val[:, c * 128:(c + 1) * 128]


def _load_token_tiles(ref, n, pitch=TILE_ROWS, first=0):
    return jnp.concatenate([ref[pl.ds(first + c, n, stride=pitch), :] for c in range(TILE_ROWS)], axis=1)


def _tile_copy(src_hbm, dst, idx_ref, j, sem, pitch):
    src = pl.multiple_of(idx_ref[0, 0, j] * pitch, pitch)
    return pltpu.make_async_copy(src_hbm.at[pl.ds(src, pitch), :],
                                 dst.at[pl.ds(pl.multiple_of(j * pitch, pitch), pitch), :], sem)


def _start_gather(src_hbm, dst, idx_ref, n, sem, pitch=TILE_ROWS):
    def issue(j, carry):
        _tile_copy(src_hbm, dst, idx_ref, j, sem, pitch).start()
        return carry

    lax.fori_loop(0, n, issue, 0, unroll=8)


def _wait_gather(src_hbm, dst, n, sem, pitch=TILE_ROWS):
    pltpu.make_async_copy(src_hbm.at[pl.ds(0, n * pitch), :], dst, sem).wait()


def _layer_norm(h, g, b):
    mu = jnp.mean(h, axis=-1, keepdims=True)
    hc = h - mu
    var = jnp.mean(hc * hc, axis=-1, keepdims=True)
    return hc * lax.rsqrt(var + LN_EPS) * g + b


def _post_kernel(oa_ref, ob_ref, oc_ref, x_ref, mod_ref, w_ref, g_ref, b_ref, wr_ref, rb_ref,
                 x1_ref, u2_ref, eid_ref, gate_ref, o_scr, ub_scr, *, alpha):
    tm = x_ref.shape[0]
    o_cat = jnp.concatenate([oa_ref[...], ob_ref[...], oc_ref[...]], axis=1)
    o_scr[...] = jnp.dot(o_cat, w_ref[...], preferred_element_type=F32)
    gate1 = 1.0 + mod_ref[2:3, :]
    scale2 = 1.0 + mod_ref[4:5, :]
    shift2 = mod_ref[3:4, :]
    ln_g = g_ref[...]
    ln_b = b_ref[...]

    def row_group(r, carry):
        r0 = pl.multiple_of(r * LN_ROWS, LN_ROWS)
        rows = pl.ds(r0, LN_ROWS)
        x1 = _layer_norm(alpha * x_ref[rows, :] + gate1 * o_scr[rows, :], ln_g, ln_b)
        x1_ref[rows, :] = x1
        u2 = x1 * scale2 + shift2
        ub_scr[rows, :] = u2.astype(BF16)
        for c in range(TILE_ROWS):
            u2_ref[pl.ds(r0 * TILE_ROWS + c, LN_ROWS, stride=TILE_ROWS), :] = u2[:, c * 128:(c + 1) * 128]
        return carry

    lax.fori_loop(0, tm // LN_ROWS, row_group, 0, unroll=LN_UNROLL)
    logits = lax.dot_general(wr_ref[...], ub_scr[...], (((1,), (1,)), ((), ())), preferred_element_type=F32)
    aff_all = jax.nn.sigmoid(logits)
    sel_all = aff_all + rb_ref[...]
    aff = [aff_all[e:e + 1, :] for e in range(N_EXPERTS)]
    sel = [sel_all[e:e + 1, :] for e in range(N_EXPERTS)]
    gsum = []
    for g in range(N_EXPERT_GROUPS):
        a, b, c, d = sel[g * PER_GROUP:(g + 1) * PER_GROUP]
        hi1, lo1, hi2, lo2 = jnp.maximum(a, b), jnp.minimum(a, b), jnp.maximum(c, d), jnp.minimum(c, d)
        gsum.append(jnp.maximum(hi1, hi2) + jnp.maximum(jnp.minimum(hi1, hi2), jnp.maximum(lo1, lo2)))
    best_g = jnp.zeros_like(gsum[0], dtype=jnp.int32)
    best_v = gsum[0]
    for g in range(1, N_EXPERT_GROUPS):
        better = gsum[g] > best_v
        best_g = jnp.where(better, g, best_g)
        best_v = jnp.where(better, gsum[g], best_v)
    s_loc, a_loc = [], []
    for j in range(PER_GROUP):
        sj, aj = sel[j], aff[j]
        for g in range(1, N_EXPERT_GROUPS):
            pick = best_g == g
            sj = jnp.where(pick, sel[g * PER_GROUP + j], sj)
            aj = jnp.where(pick, aff[g * PER_GROUP + j], aj)
        s_loc.append(sj)
        a_loc.append(aj)
    i1 = jnp.zeros_like(best_g)
    v1, g1 = s_loc[0], a_loc[0]
    for j in range(1, PER_GROUP):
        better = s_loc[j] > v1
        i1 = jnp.where(better, j, i1)
        v1 = jnp.where(better, s_loc[j], v1)
        g1 = jnp.where(better, a_loc[j], g1)
    i2 = jnp.full_like(best_g, -1)
    v2 = jnp.full_like(v1, -jnp.inf)
    g2 = jnp.zeros_like(g1)
    for j in range(PER_GROUP):
        better = (i1 != j) & ((s_loc[j] > v2) | (i2 < 0))
        i2 = jnp.where(better, j, i2)
        v2 = jnp.where(better, s_loc[j], v2)
        g2 = jnp.where(better, a_loc[j], g2)
    total = g1 + g2
    lo, hi = jnp.minimum(i1, i2), jnp.maximum(i1, i2)
    pair = jnp.where(lo == 0, hi - 1, jnp.where(lo == 1, hi + 1, PAIRS_PER_GROUP - 1))
    first_is_lo = i1 < i2
    g_lo = jnp.where(first_is_lo, g1, g2) / total
    g_hi = jnp.where(first_is_lo, g2, g1) / total
    eid_ref[...] = jnp.concatenate([best_g * PAIRS_PER_GROUP + pair,
                                    jnp.zeros((MOD_ROWS - 1, tm), jnp.int32)], axis=0)
    gate_ref[...] = jnp.concatenate([g_lo, g_hi, jnp.zeros((MOD_ROWS - TOP_K, tm), F32)], axis=0)


def _post(oa, ob, oc, xa, mod, w_out, ln_g, ln_b, wr_t, rb, n_lat, alpha):
    bsz, s_tot, d = xa.shape
    tm = TOK_TILE
    n_lat_tiles = n_lat // tm
    n_tiles = s_tot // tm
    sd = jax.ShapeDtypeStruct

    def tmajor(width):
        return pl.BlockSpec((None, tm, width), lambda b, i: (b, i, 0))

    def const(shape):
        return pl.BlockSpec(shape, lambda b, i: (0,) * len(shape))

    lanes = pl.BlockSpec((MOD_ROWS, tm), lambda b, i: (0, b * n_tiles + i))
    return pl.pallas_call(
        functools.partial(_post_kernel, alpha=alpha),
        grid=(bsz, n_tiles),
        in_specs=[tmajor(QA_W), tmajor(QB_W), tmajor(QC_W), tmajor(d),
                  pl.BlockSpec((None, N_MOD, d), lambda b, i: (jnp.where(i >= n_lat_tiles, bsz, b), 0, 0)),
                  const((d, d)), const((1, d)), const((1, d)), const((N_EXPERTS, d)), const((N_EXPERTS, 1))],
        out_specs=[tmajor(d), pl.BlockSpec((tm * TILE_ROWS, 128), lambda b, i: (b * n_tiles + i, 0)), lanes, lanes],
        out_shape=[sd((bsz, s_tot, d), F32), sd((bsz * s_tot * TILE_ROWS, 128), F32),
                   sd((MOD_ROWS, bsz * s_tot), jnp.int32), sd((MOD_ROWS, bsz * s_tot), F32)],
        scratch_shapes=[pltpu.VMEM((tm, d), F32), pltpu.VMEM((tm, d), BF16)],
        compiler_params=_params(("parallel", "parallel")),
    )(oa, ob, oc, xa, mod, w_out, ln_g, ln_b, wr_t, rb)


def _ffn_kernel(elo_ref, ehi_ref, nused_ref, tok_ref, tok_next_ref, u_hbm,
                w1lo_ref, w3lo_ref, w2lo_ref, w1hi_ref, w3hi_ref, w2hi_ref, y_ref,
                xbuf, w1lo, w3lo, w2lo, w1hi, w3hi, w2hi, sem):
    blk = y_ref.shape[0] // (TOP_K * TILE_ROWS)
    step = pl.program_id(0)
    n_used = nused_ref[0]
    slot = step % 2
    prev = jnp.maximum(step - 1, 0)

    @pl.when(step < n_used)
    def _():
        @pl.when(step == 0)
        def _():
            _start_gather(u_hbm, xbuf.at[0], tok_ref, blk, sem.at[0])

        @pl.when(step + 1 < n_used)
        def _():
            _start_gather(u_hbm, xbuf.at[1 - slot], tok_next_ref, blk, sem.at[1 - slot])

        @pl.when((step == 0) | (elo_ref[step] != elo_ref[prev]))
        def _():
            w1lo[...] = w1lo_ref[...].astype(BF16)
            w3lo[...] = w3lo_ref[...].astype(BF16)
            w2lo[...] = w2lo_ref[...].astype(BF16)

        @pl.when((step == 0) | (ehi_ref[step] != ehi_ref[prev]))
        def _():
            w1hi[...] = w1hi_ref[...].astype(BF16)
            w3hi[...] = w3hi_ref[...].astype(BF16)
            w2hi[...] = w2hi_ref[...].astype(BF16)

        _wait_gather(u_hbm, xbuf.at[slot], blk, sem.at[slot])
        xb = _load_token_tiles(xbuf.at[slot], blk).astype(BF16)
        for first, (w1b, w3b, w2b) in ((0, (w1lo, w3lo, w2lo)), (TILE_ROWS, (w1hi, w3hi, w2hi))):
            h1 = jnp.dot(xb, w1b[...], preferred_element_type=F32)
            h3 = jnp.dot(xb, w3b[...], preferred_element_type=F32)
            hid = (h1 * jax.nn.sigmoid(h1) * h3).astype(BF16)
            _store_token_tiles(y_ref, jnp.dot(hid, w2b[...], preferred_element_type=F32),
                               pitch=TOP_K * TILE_ROWS, first=first)

    @pl.when(step >= n_used)
    def _():
        y_ref[...] = jnp.zeros_like(y_ref)


def _expert_ffn(block_lo, block_hi, n_used, slot_tok, u2_tiles, w1, w3, w2, layer):
    d = w1.shape[2]
    n_blocks = block_lo.shape[0]
    blk = CLASS_BLOCK
    slot_tok = slot_tok.reshape(n_blocks, 1, blk)

    def weight(shape, which):
        return pl.BlockSpec((None, None) + shape, lambda i, lo, hi, nu: (layer, (lo, hi)[which][i], 0, 0))

    up, down = (d, D_EXPERT), (D_EXPERT, d)
    grid_spec = pltpu.PrefetchScalarGridSpec(
        num_scalar_prefetch=3,
        grid=(n_blocks,),
        in_specs=[pl.BlockSpec((1, 1, blk), lambda i, lo, hi, nu: (i, 0, 0), memory_space=pltpu.SMEM),
                  pl.BlockSpec((1, 1, blk), lambda i, lo, hi, nu: (jnp.minimum(i + 1, n_blocks - 1), 0, 0),
                               memory_space=pltpu.SMEM),
                  pl.BlockSpec(memory_space=pl.ANY),
                  weight(up, 0), weight(up, 0), weight(down, 0), weight(up, 1), weight(up, 1), weight(down, 1)],
        out_specs=pl.BlockSpec((blk * TOP_K * TILE_ROWS, 128), lambda i, lo, hi, nu: (i, 0)),
        scratch_shapes=[pltpu.VMEM((2, blk * TILE_ROWS, 128), F32)]
        + [pltpu.VMEM(s, BF16) for s in (up, up, down, up, up, down)] + [pltpu.SemaphoreType.DMA((2,))],
    )
    return pl.pallas_call(
        _ffn_kernel,
        grid_spec=grid_spec,
        out_shape=jax.ShapeDtypeStruct((n_blocks * blk * TOP_K * TILE_ROWS, 128), F32),
        compiler_params=_params(("arbitrary",)),
    )(block_lo, block_hi, n_used, slot_tok, slot_tok, u2_tiles, w1, w3, w2, w1, w3, w2)


def _comb_kernel(pos_ref, pos_next_ref, y_hbm, x1_ref, gate_ref, mod_ref, g_ref, b_ref, o_ref, ybuf, sem, *, alpha):
    tm = x1_ref.shape[0]
    pitch = TOP_K * TILE_ROWS
    step = pl.program_id(0)
    slot = step % 2

    @pl.when(step == 0)
    def _():
        _start_gather(y_hbm, ybuf.at[0], pos_ref, tm, sem.at[0], pitch)

    @pl.when(step + 1 < pl.num_programs(0))
    def _():
        _start_gather(y_hbm, ybuf.at[1 - slot], pos_next_ref, tm, sem.at[1 - slot], pitch)

    _wait_gather(y_hbm, ybuf.at[slot], tm, sem.at[slot], pitch)
    y_lo = _load_token_tiles(ybuf.at[slot], tm, pitch, 0)
    y_hi = _load_token_tiles(ybuf.at[slot], tm, pitch, TILE_ROWS)
    f = gate_ref[:, 0:1] * y_lo + gate_ref[:, 1:2] * y_hi
    o_ref[...] = _layer_norm(alpha * x1_ref[...] + (1.0 + mod_ref[5:6, :]) * f, g_ref[...], b_ref[...])


def _combine(pos, y_tiles, x1, gates, mod, ln_g, ln_b, n_lat, alpha):
    bsz, s_tot, d = x1.shape
    tm = TOK_TILE
    n_lat_tiles = n_lat // tm
    n_tiles = s_tot // tm
    n_steps = bsz * n_tiles

    def mod_row(t):
        return jnp.where(t % n_tiles >= n_lat_tiles, bsz, t // n_tiles)

    out = pl.pallas_call(
        functools.partial(_comb_kernel, alpha=alpha),
        grid=(n_steps,),
        in_specs=[pl.BlockSpec((1, 1, tm), lambda t: (t, 0, 0), memory_space=pltpu.SMEM),
                  pl.BlockSpec((1, 1, tm), lambda t: (jnp.minimum(t + 1, n_steps - 1), 0, 0),
                               memory_space=pltpu.SMEM),
                  pl.BlockSpec(memory_space=pl.ANY),
                  pl.BlockSpec((tm, d), lambda t: (t, 0)),
                  pl.BlockSpec((tm, TOP_K), lambda t: (t, 0)),
                  pl.BlockSpec((None, N_MOD, d), lambda t: (mod_row(t), 0, 0)),
                  pl.BlockSpec((1, d), lambda t: (0, 0)),
                  pl.BlockSpec((1, d), lambda t: (0, 0))],
        out_specs=pl.BlockSpec((tm, d), lambda t: (t, 0)),
        out_shape=jax.ShapeDtypeStruct((bsz * s_tot, d), F32),
        scratch_shapes=[pltpu.VMEM((2, TOP_K * tm * TILE_ROWS, 128), F32), pltpu.SemaphoreType.DMA((2,))],
        compiler_params=_params(("arbitrary",)),
    )(pos, pos, y_tiles, x1.reshape(bsz * s_tot, d), gates, mod, ln_g, ln_b)
    return out.reshape(bsz, s_tot, d)


def _deinterleave(n_heads):
    one = np.concatenate([np.arange(0, HEAD_DIM, 2), np.arange(1, HEAD_DIM, 2)])
    return np.concatenate([h * HEAD_DIM + one for h in range(n_heads)])


def _proj_weight(w_in_l):
    sizes = (QA_W, KA_W, KA_W, QB_W, KB_W, KB_W, QC_W, KC_W, KC_W)
    offs = np.concatenate([[0], np.cumsum(sizes)])
    qa, ka, va, qb, kb, vb, qc, kc, vc = [w_in_l[:, offs[i]:offs[i + 1]] for i in range(len(sizes))]
    scale = HEAD_DIM ** -0.5
    qa = qa[:, _deinterleave(A_HEADS)] * scale
    ka = ka[:, _deinterleave(A_KV)]
    qb = qb * scale
    qc = qc[:, _deinterleave(C_HEADS)]
    kc = kc[:, _deinterleave(C_KV)]
    return jnp.concatenate([qa, qb, qc, ka, kb, kc, va, vb, vc], axis=1).T.astype(BF16)


def _rope_tables(n_lat, n_ctx):
    t = np.arange(n_lat)
    inv_freq = ROPE_BASE ** (-np.arange(0, HALF, 2, dtype=np.float32) / HALF)
    ang = jnp.concatenate([jnp.asarray(t // GRID_W, F32)[None, :] * jnp.asarray(inv_freq)[:, None],
                           jnp.asarray(t % GRID_W, F32)[None, :] * jnp.asarray(inv_freq)[:, None]], axis=0)
    cos = jnp.concatenate([jnp.cos(ang), jnp.ones((HALF, n_ctx), F32)], axis=1)
    sin = jnp.concatenate([jnp.sin(ang), jnp.zeros((HALF, n_ctx), F32)], axis=1)
    return cos, sin


def _dispatch(cls, n_tok):
    onehot = (cls[:, None] == jnp.arange(N_CLASSES, dtype=jnp.int32)[None, :]).astype(jnp.int32)
    csum = jnp.cumsum(onehot, axis=0)
    counts = csum[-1]
    padded = (counts + CLASS_BLOCK - 1) // CLASS_BLOCK * CLASS_BLOCK
    pad_ends = jnp.cumsum(padded)
    dest = jnp.sum(onehot * (csum - 1 + (pad_ends - padded)[None, :]), axis=1)
    n_blocks = -(-n_tok // CLASS_BLOCK) + N_CLASSES
    slot_tok = jnp.zeros((n_blocks * CLASS_BLOCK,), jnp.int32).at[dest].set(jnp.arange(n_tok, dtype=jnp.int32))
    block_start = jnp.arange(n_blocks, dtype=jnp.int32) * CLASS_BLOCK
    block_cls = jnp.minimum(jnp.sum((block_start[:, None] >= pad_ends[None, :]).astype(jnp.int32), axis=1),
                            N_CLASSES - 1)
    group, pair = block_cls // PAIRS_PER_GROUP, block_cls % PAIRS_PER_GROUP
    lo = (pair >= 3).astype(jnp.int32) + (pair >= 5).astype(jnp.int32)
    hi = jnp.where(pair < 3, pair + 1, jnp.where(pair < 5, pair - 1, 3))
    n_used = (pad_ends[-1] // CLASS_BLOCK).astype(jnp.int32).reshape(1)
    return slot_tok, group * PER_GROUP + lo, group * PER_GROUP + hi, dest.astype(jnp.int32), n_used


def kernel(x, c, ctx, c_ctx, w_in, w_out, sink, rpb, q_gain, k_gain, w_ada, b_ada,
           ln1_g, ln1_b, ln2_g, ln2_b, w_router, router_bias, w1, w3, w2):
    bsz, n_lat, d = x.shape
    n_ctx = ctx.shape[1]
    s_tot = n_lat + n_ctx
    depth = w_in.shape[0]
    assert d == D_MODEL and n_ctx == TOK_TILE and n_lat % min(TK_C, n_lat) == 0
    assert n_lat % (2 * TOK_TILE) == 0 and bsz + 1 <= MOD_ROWS
    alpha = float((2 * depth) ** 0.25)
    n_tok = bsz * s_tot
    tm = TOK_TILE

    cond = jnp.zeros((MOD_ROWS, d), F32).at[:bsz].set(c).at[bsz].set(c_ctx)
    mods = _ada_table(cond, w_ada, b_ada).reshape(depth, MOD_ROWS, N_MOD, d)
    cos_t, sin_t = _rope_tables(n_lat, n_ctx)
    gain_perm = _deinterleave(1)
    wr_t = w_router.T.astype(BF16)
    rb = router_bias.reshape(N_EXPERTS, 1).astype(F32)

    bias_tiles, starts, variants, n_keys = _nb_bias_tiles(rpb, n_lat, s_tot)

    xa = jnp.concatenate([x, ctx], axis=1)
    for l in range(depth):
        mod = mods[l]
        qa, qb, qc, ka, kb, kc, va, vb, vc, qn, kn = _project(
            xa, mod, _proj_weight(w_in[l]), cos_t, sin_t,
            q_gain[l][gain_perm].reshape(HEAD_DIM, 1) * (HEAD_DIM ** -0.5 * LOG2_E),
            k_gain[l][gain_perm].reshape(HEAD_DIM, 1), n_lat)
        oa = _attn_a(sink[l], qa, ka, va, n_lat)
        ob = _attn_b(qb, kb, vb, bias_tiles[l], starts, variants, n_keys, n_lat)
        oc = _attn_c(qc, qn, kc, kn, vc, n_lat)
        x1, u2, eid8, gate8 = _post(oa, ob, oc, xa, mod, w_out[l].astype(BF16),
                                    ln1_g[l].reshape(1, d), ln1_b[l].reshape(1, d), wr_t, rb, n_lat, alpha)
        slot_tok, block_lo, block_hi, dest, n_used = _dispatch(eid8[0], n_tok)
        y = _expert_ffn(block_lo, block_hi, n_used, slot_tok, u2, w1, w3, w2, l)
        xa = _combine(dest.reshape(n_tok // tm, 1, tm), y, x1, gate8[:TOP_K].T, mod,
                      ln2_g[l].reshape(1, d), ln2_b[l].reshape(1, d), n_lat, alpha)
    return xa[:, :n_lat]
```

```python
import functools

import numpy as np
import jax
import jax.numpy as jnp
from jax import lax
from jax.experimental import pallas as pl
from jax.experimental.pallas import tpu as pltpu

F32 = jnp.float32
BF16 = jnp.bfloat16

D_MODEL = 1024
HEAD_DIM = 64
HALF = HEAD_DIM // 2
GRID_W = 64
A_HEADS, A_KV = 6, 2
WINDOW = 128
B_HEADS = 4
NA_ROWS, NA_COLS = 8, 16
C_HEADS, C_KV = 6, 2
GQA_GROUP = 3
ROPE_BASE = 10000.0
N_EXPERTS = 16
N_EXPERT_GROUPS = 4
PER_GROUP = N_EXPERTS // N_EXPERT_GROUPS
TOP_K = 2
D_EXPERT = 512
PAIRS_PER_GROUP = PER_GROUP * (PER_GROUP - 1) // 2
N_CLASSES = N_EXPERT_GROUPS * PAIRS_PER_GROUP
CLASS_BLOCK = 256
N_MOD = 6
LN_EPS = 1e-6
QK_EPS = 1e-6
NEG_INF = -1e30

QA_W, QB_W, QC_W = A_HEADS * HEAD_DIM, B_HEADS * HEAD_DIM, C_HEADS * HEAD_DIM
KA_W, KB_W, KC_W = A_KV * HEAD_DIM, B_HEADS * HEAD_DIM, C_KV * HEAD_DIM
R_QA = 0
R_QB = R_QA + QA_W
R_QC = R_QB + QB_W
R_KA = R_QC + QC_W
R_KB = R_KA + KA_W
R_KC = R_KB + KB_W
R_VA = R_KC + KC_W
R_VB = R_VA + KA_W
R_VC = R_VB + KB_W
PROJ_WIDTH = R_VC + KC_W

V7X_VMEM_LIMIT = 56 * 1024 * 1024
TOK_TILE = 256
TQ_A = 256
TQ_B = 2 * GRID_W
TQ_C = 256
TK_C = 2048
UNROLL_C = 1
MOD_ROWS = 8
LOG2_E = 1.4426950408889634
QN_C, QN_A, QN_B, QN_ROWS = 0, C_HEADS, C_HEADS + A_HEADS, C_HEADS + A_HEADS + B_HEADS
KN_C, KN_A, KN_B, KN_ROWS = 0, C_KV, C_KV + A_KV, C_KV + A_KV + B_HEADS
MAX_SCORE_BOUND = 60.0
BOUND_SLACK = 1.004
TILE_ROWS = 8
BF16_ROWS = 16
GATHER_UNROLL = 8
LN_ROWS = 64
LN_UNROLL = 4
V_ROWS = HEAD_DIM + BF16_ROWS


def _params(sem):
    return pltpu.CompilerParams(dimension_semantics=sem, vmem_limit_bytes=V7X_VMEM_LIMIT)


def _ada_kernel(c_ref, w_ref, b_ref, o_ref):
    c = c_ref[...]
    act = c * jax.nn.sigmoid(c)
    o_ref[...] = jnp.dot(act, w_ref[...], preferred_element_type=F32,
                         precision=lax.Precision.HIGHEST) + b_ref[...]


def _ada_table(cond, w_ada, b_ada):
    depth, d, n = w_ada.shape
    tn = n // 4
    return pl.pallas_call(
        _ada_kernel,
        grid=(depth, n // tn),
        in_specs=[pl.BlockSpec((MOD_ROWS, d), lambda l, j: (0, 0)),
                  pl.BlockSpec((None, d, tn), lambda l, j: (l, 0, j)),
                  pl.BlockSpec((None, 1, tn), lambda l, j: (l, 0, j))],
        out_specs=pl.BlockSpec((None, MOD_ROWS, tn), lambda l, j: (l, 0, j)),
        out_shape=jax.ShapeDtypeStruct((depth, MOD_ROWS, n), F32),
        compiler_params=_params(("parallel", "parallel")),
    )(cond, w_ada, b_ada.reshape(depth, 1, n))


def _sq_norm(parts):
    return sum(jnp.sum(jnp.square(p.astype(F32)), axis=0, keepdims=True) for p in parts)


def _proj_kernel(x_ref, mod_ref, w_ref, cos_ref, sin_ref, qg_ref, kg_ref,
                 qa_ref, qb_ref, qc_ref, ka_ref, kb_ref, kc_ref, va_ref, vb_ref, vc_ref, qn_ref, kn_ref):
    tm = x_ref.shape[0]
    u = (x_ref[...] * (1.0 + mod_ref[1:2, :]) + mod_ref[0:1, :]).astype(BF16)

    def proj(lo, width):
        return lax.dot_general(w_ref[lo:lo + width, :], u, (((1,), (1,)), ((), ())),
                               preferred_element_type=F32)

    cos = cos_ref[...]
    sin = sin_ref[...]

    def rope(blk):
        x0, x1 = blk[:HALF], blk[HALF:]
        return x0 * cos - x1 * sin, x0 * sin + x1 * cos

    def qk_norm(blk, gain):
        ms = jnp.mean(blk * blk, axis=0, keepdims=True)
        return blk * lax.rsqrt(ms + QK_EPS) * gain

    p = proj(R_QA, QA_W)
    for h in range(A_HEADS):
        o0, o1 = [o.astype(BF16) for o in rope(p[h * HEAD_DIM:(h + 1) * HEAD_DIM])]
        qa_ref[h * HEAD_DIM:h * HEAD_DIM + HALF, :] = o0
        qa_ref[h * HEAD_DIM + HALF:(h + 1) * HEAD_DIM, :] = o1
        qn_ref[QN_A + h:QN_A + h + 1, :] = _sq_norm([o0, o1])
    p = proj(R_QB, QB_W).astype(BF16)
    qb_ref[...] = p
    for h in range(B_HEADS):
        qn_ref[QN_B + h:QN_B + h + 1, :] = _sq_norm([p[h * HEAD_DIM:(h + 1) * HEAD_DIM]])
    p = proj(R_QC, QC_W)
    qg = qg_ref[...]
    for h in range(C_HEADS):
        o0, o1 = [o.astype(BF16) for o in rope(qk_norm(p[h * HEAD_DIM:(h + 1) * HEAD_DIM], qg))]
        qc_ref[h * HEAD_DIM:h * HEAD_DIM + HALF, :] = o0
        qc_ref[h * HEAD_DIM + HALF:(h + 1) * HEAD_DIM, :] = o1
        qn_ref[QN_C + h:QN_C + h + 1, :] = _sq_norm([o0, o1])
    one_lane = (lax.broadcasted_iota(jnp.int32, (tm, KA_W), 1) == 0).astype(BF16)
    p = proj(R_KA, KA_W)
    parts = []
    for h in range(A_KV):
        o0, o1 = [o.astype(BF16) for o in rope(p[h * HEAD_DIM:(h + 1) * HEAD_DIM])]
        kn_ref[KN_A + h:KN_A + h + 1, :] = _sq_norm([o0, o1])
        parts.extend([o0.astype(F32), o1.astype(F32)])
    ka_ref[:, 0:KA_W] = jnp.concatenate(parts, axis=0).T.astype(BF16)
    ka_ref[:, KA_W:2 * KA_W] = one_lane
    p = proj(R_KB, KB_W).astype(BF16)
    for h in range(B_HEADS):
        kn_ref[KN_B + h:KN_B + h + 1, :] = _sq_norm([p[h * HEAD_DIM:(h + 1) * HEAD_DIM]])
    kb_ref[...] = p.astype(F32).T.astype(BF16)
    p = proj(R_KC, KC_W)
    kg = kg_ref[...]
    parts = []
    for h in range(C_KV):
        o0, o1 = [o.astype(BF16) for o in rope(qk_norm(p[h * HEAD_DIM:(h + 1) * HEAD_DIM], kg))]
        kn_ref[KN_C + h:KN_C + h + 1, :] = _sq_norm([o0, o1])
        parts.extend([o0.astype(F32), o1.astype(F32)])
    kc_ref[:, 0:KC_W] = jnp.concatenate(parts, axis=0).T.astype(BF16)
    kc_ref[:, KC_W:2 * KC_W] = one_lane
    ones_row = (lax.broadcasted_iota(jnp.int32, (BF16_ROWS, tm), 0) == 0).astype(BF16)
    for lo, ref in ((R_VA, va_ref), (R_VC, vc_ref)):
        p = proj(lo, KA_W)
        for g in range(A_KV):
            ref[g, 0:HEAD_DIM, :] = p[g * HEAD_DIM:(g + 1) * HEAD_DIM].astype(BF16)
            ref[g, HEAD_DIM:V_ROWS, :] = ones_row
    vb_ref[...] = proj(R_VB, KB_W).astype(BF16)


def _project(xa, mod, w_t, cos_t, sin_t, q_gain, k_gain, n_lat):
    bsz, s_tot, d = xa.shape
    tm = TOK_TILE
    n_lat_tiles = n_lat // tm
    sd = jax.ShapeDtypeStruct

    def dmajor(width):
        return pl.BlockSpec((None, width, tm), lambda b, i: (b, 0, i))

    def tmajor(width):
        return pl.BlockSpec((None, tm, width), lambda b, i: (b, i, 0))

    aug = pl.BlockSpec((None, A_KV, V_ROWS,tm), lambda b, i: (b, 0, 0, i))
    return pl.pallas_call(
        _proj_kernel,
        grid=(bsz, s_tot // tm),
        in_specs=[tmajor(d),
                  pl.BlockSpec((None, N_MOD, d), lambda b, i: (jnp.where(i >= n_lat_tiles, bsz, b), 0, 0)),
                  pl.BlockSpec((PROJ_WIDTH, d), lambda b, i: (0, 0)),
                  pl.BlockSpec((HALF, tm), lambda b, i: (0, i)),
                  pl.BlockSpec((HALF, tm), lambda b, i: (0, i)),
                  pl.BlockSpec((HEAD_DIM, 1), lambda b, i: (0, 0)),
                  pl.BlockSpec((HEAD_DIM, 1), lambda b, i: (0, 0))],
        out_specs=[dmajor(QA_W), dmajor(QB_W), dmajor(QC_W),
                   tmajor(2 * KA_W), tmajor(KB_W), tmajor(2 * KC_W),
                   aug, dmajor(KB_W), aug, dmajor(QN_ROWS), dmajor(KN_ROWS)],
        out_shape=[sd((bsz, QA_W, s_tot), BF16), sd((bsz, QB_W, s_tot), BF16), sd((bsz, QC_W, s_tot), BF16),
                   sd((bsz, s_tot, 2 * KA_W), BF16), sd((bsz, s_tot, KB_W), BF16), sd((bsz, s_tot, 2 * KC_W), BF16),
                   sd((bsz, A_KV, V_ROWS, s_tot), BF16), sd((bsz, KB_W, s_tot), BF16),
                   sd((bsz, A_KV, V_ROWS, s_tot), BF16),
                   sd((bsz, QN_ROWS, s_tot), F32), sd((bsz, KN_ROWS, s_tot), F32)],
        compiler_params=_params(("parallel", "parallel")),
    )(xa, mod, w_t, cos_t, sin_t, q_gain, k_gain)


def _padded_q(q_ref, h, g):
    blk = q_ref[h * HEAD_DIM:(h + 1) * HEAD_DIM, :]
    zeros = jnp.zeros_like(blk)
    return jnp.concatenate([blk, zeros] if g == 0 else [zeros, blk], axis=0)


def _attn_a_kernel(fast_ref, ksc_ref, sink_ref, q_ref, qn_ref, k_ref, v_ref, o_ref, ot_ref, *, n_lat, n_ctx):
    tq = q_ref.shape[1]
    span = tq + 2 * WINDOW
    start = pl.program_id(1) * tq
    ks = pl.multiple_of(jnp.clip(start - WINDOW, 0, n_lat - span), 128)
    kpos = ks + lax.broadcasted_iota(jnp.int32, (span, tq), 0)
    qpos = start + lax.broadcasted_iota(jnp.int32, (span, tq), 1)
    ok = (jnp.abs(qpos - kpos) <= WINDOW) & (qpos < n_lat)
    k_loc = k_ref[pl.ds(ks, span), :]
    k_ctx = k_ref[n_lat:n_lat + n_ctx, :]
    first_row = lax.broadcasted_iota(jnp.int32, (KA_W, tq), 0) == 0

    def head(h, g, bounded):
        v_loc = v_ref[g, :, pl.ds(ks, span)]
        v_ctx = v_ref[g, :, n_lat:n_lat + n_ctx]
        sink = sink_ref[h]
        if bounded:
            offset = jnp.sqrt(qn_ref[QN_A + h:QN_A + h + 1, :]) * ksc_ref[pl.program_id(0) * A_KV + g]
            qp = jnp.concatenate([_padded_q(q_ref, h, g), jnp.where(first_row, -offset, 0.0).astype(BF16)], axis=0)
        else:
            qp = jnp.concatenate([_padded_q(q_ref, h, g), jnp.zeros((KA_W, tq), BF16)], axis=0)
        s_loc = jnp.where(ok, jnp.dot(k_loc, qp, preferred_element_type=F32), NEG_INF)
        s_ctx = jnp.dot(k_ctx, qp, preferred_element_type=F32)
        if bounded:
            m = offset
        else:
            m = jnp.maximum(jnp.maximum(jnp.max(s_loc, axis=0, keepdims=True),
                                        jnp.max(s_ctx, axis=0, keepdims=True)), sink)
            s_loc, s_ctx = s_loc - m, s_ctx - m
        acc = (jnp.dot(v_loc, jnp.exp2(s_loc).astype(BF16), preferred_element_type=F32)
               + jnp.dot(v_ctx, jnp.exp2(s_ctx).astype(BF16), preferred_element_type=F32))
        denom = acc[HEAD_DIM:HEAD_DIM + 1, :] + jnp.exp2(sink - m)
        ot_ref[h * HEAD_DIM:(h + 1) * HEAD_DIM, :] = acc[0:HEAD_DIM, :] / denom

    for bounded in (True, False):
        @pl.when((fast_ref[0] == 1) == bounded)
        def _():
            for g in range(A_KV):
                for hh in range(GQA_GROUP):
                    head(g * GQA_GROUP + hh, g, bounded)

    o_ref[...] = ot_ref[...].T.astype(BF16)


def _attn_a(sink, q_t, q_norm2, k, k_norm2, v_t, n_lat):
    bsz, s_tot, _ = k.shape
    tq = TQ_A
    sink2 = sink.astype(F32) * LOG2_E
    k_scale = jnp.sqrt(jnp.max(k_norm2[:, KN_A:KN_A + A_KV, :], axis=-1)) * BOUND_SLACK
    bound = jnp.sqrt(jnp.max(q_norm2[:, QN_A:QN_A + A_HEADS, :])) * jnp.max(k_scale)
    fast = ((bound <= MAX_SCORE_BOUND) & (jnp.max(jnp.abs(sink2)) <= MAX_SCORE_BOUND)).astype(jnp.int32).reshape(1)
    kern = functools.partial(_attn_a_kernel, n_lat=n_lat, n_ctx=s_tot - n_lat)
    smem = pl.BlockSpec(memory_space=pltpu.SMEM)
    return pl.pallas_call(
        kern,
        grid=(bsz, s_tot // tq),
        in_specs=[smem, smem, smem,
                  pl.BlockSpec((None, QA_W, tq), lambda b, i: (b, 0, i)),
                  pl.BlockSpec((None, QN_ROWS, tq), lambda b, i: (b, 0, i)),
                  pl.BlockSpec((None, s_tot, 2 * KA_W), lambda b, i: (b, 0, 0)),
                  pl.BlockSpec((None, A_KV, V_ROWS, s_tot), lambda b, i: (b, 0, 0, 0))],
        out_specs=pl.BlockSpec((None, tq, QA_W), lambda b, i: (b, i, 0)),
        out_shape=jax.ShapeDtypeStruct((bsz, s_tot, QA_W), BF16),
        scratch_shapes=[pltpu.VMEM((QA_W, tq), F32)],
        compiler_params=_params(("parallel", "arbitrary")),
    )(fast, k_scale.reshape(-1), sink2, q_t, q_norm2, k, v_t)


def _nb_plan(n_lat, s_tot):
    rows = n_lat // GRID_W
    kh = min(NA_ROWS, rows)
    q_rows = TQ_B // GRID_W
    w_rows = kh + q_rows
    n_lat_blocks = n_lat // TQ_B
    n_blocks = s_tot // TQ_B
    starts, variants, keys, reps = [], [], {}, []
    for i in range(n_lat_blocks):
        w0 = int(np.clip(q_rows * i - kh // 2, 0, rows - w_rows))
        r0s = tuple(int(np.clip(q_rows * i + j - kh // 2, 0, rows - kh)) - w0 for j in range(q_rows))
        key = (q_rows * i - w0,) + r0s
        if key not in keys:
            keys[key] = len(reps)
            reps.append((i, w0))
        starts.append(w0 * GRID_W)
        variants.append(keys[key])
    masked = len(reps)
    starts.extend([0] * (n_blocks - n_lat_blocks))
    variants.extend([masked] * (n_blocks - n_lat_blocks))
    return rows, kh, w_rows, reps, np.asarray(starts, np.int32), np.asarray(variants, np.int32)


def _rel_bias_kernel(r_ref, e_ref, m_ref, o_ref):
    o_ref[...] = jnp.dot(r_ref[...], e_ref[...], preferred_element_type=F32,
                         precision=lax.Precision.HIGHEST) + m_ref[...]


def _rel_bias_blocks(rpb):
    depth, heads, n_dr, n_dc = rpb.shape
    kc = np.arange(GRID_W)[:, None]
    cq = np.arange(GRID_W)[None, :]
    dc = (kc - cq + NA_COLS - 1).reshape(-1)
    n_dc_pad = 32
    onehot = (np.arange(n_dc_pad)[:, None] == dc[None, :]).astype(np.float32)
    c0 = np.clip(cq - NA_COLS // 2, 0, GRID_W - NA_COLS)
    outside = ~((kc >= c0) & (kc < c0 + NA_COLS))
    col_mask = np.where(outside, NEG_INF, 0.0).astype(np.float32).reshape(1, -1)
    table = jnp.pad((rpb.astype(F32) * LOG2_E).reshape(depth * heads * n_dr, n_dc), ((0, 0), (0, n_dc_pad - n_dc)))
    blocks = pl.pallas_call(
        _rel_bias_kernel,
        out_shape=jax.ShapeDtypeStruct((depth * heads * n_dr, GRID_W * GRID_W), F32),
    )(table, jnp.asarray(onehot), jnp.asarray(col_mask))
    blocks = blocks.reshape(depth, heads, n_dr, GRID_W, GRID_W)
    return jnp.concatenate([blocks, jnp.full((depth, heads, 1, GRID_W, GRID_W), NEG_INF, F32)], axis=2)


def _nb_bias_tiles(rpb, n_lat, s_tot):
    rows, kh, w_rows, reps, starts, variants = _nb_plan(n_lat, s_tot)
    q_rows = TQ_B // GRID_W
    masked_block = 2 * NA_ROWS - 1
    dr_map = np.full((len(reps) + 1, w_rows, q_rows), masked_block, np.int32)
    for v, (i, w0) in enumerate(reps):
        for a in range(w_rows):
            for b in range(q_rows):
                r = q_rows * i + b
                r0 = int(np.clip(r - kh // 2, 0, rows - kh))
                if r0 <= w0 + a < r0 + kh:
                    dr_map[v, a, b] = w0 + a - r + NA_ROWS - 1
    blocks = _rel_bias_blocks(rpb)[:, :, dr_map]
    depth = rpb.shape[0]
    tiles = jnp.transpose(blocks, (0, 2, 3, 5, 1, 4, 6)).reshape(
        depth, len(reps) + 1, w_rows * GRID_W, B_HEADS * TQ_B)
    return tiles, jnp.asarray(starts), jnp.asarray(variants), w_rows * GRID_W


def _attn_b_kernel(start_ref, var_ref, fast_ref, ksc_ref, q_ref, qn_ref, k_ref, v_ref, bias_ref, o_ref,
                   qp_ref, ot_ref, m_ref, *, n_lat, n_ctx, n_keys):
    tq = q_ref.shape[1]
    ws = pl.multiple_of(start_ref[pl.program_id(1)], 128)
    qp_ref[...] = jnp.zeros_like(qp_ref)
    for h in range(B_HEADS):
        qp_ref[h * HEAD_DIM:(h + 1) * HEAD_DIM, h * tq:(h + 1) * tq] = q_ref[h * HEAD_DIM:(h + 1) * HEAD_DIM, :]
    qp = qp_ref[...]
    s_loc = jnp.dot(k_ref[pl.ds(ws, n_keys), :], qp, preferred_element_type=F32) + bias_ref[...]
    s_ctx = jnp.dot(k_ref[n_lat:n_lat + n_ctx, :], qp, preferred_element_type=F32)
    for bounded in (True, False):
        @pl.when((fast_ref[0] == 1) == bounded)
        def _():
            if bounded:
                m_ref[...] = jnp.concatenate(
                    [jnp.sqrt(qn_ref[QN_B + h:QN_B + h + 1, :]) * ksc_ref[pl.program_id(0) * B_HEADS + h]
                     for h in range(B_HEADS)], axis=1)
            else:
                m_ref[...] = jnp.maximum(jnp.max(s_loc, axis=0, keepdims=True), jnp.max(s_ctx, axis=0, keepdims=True))
    m = m_ref[...]
    p_loc = jnp.exp2(s_loc - m)
    p_ctx = jnp.exp2(s_ctx - m)
    denom = jnp.sum(p_loc, axis=0, keepdims=True) + jnp.sum(p_ctx, axis=0, keepdims=True)
    acc = (jnp.dot(v_ref[:, pl.ds(ws, n_keys)], p_loc.astype(BF16), preferred_element_type=F32)
           + jnp.dot(v_ref[:, n_lat:n_lat + n_ctx], p_ctx.astype(BF16), preferred_element_type=F32))
    for h in range(B_HEADS):
        ot_ref[h * HEAD_DIM:(h + 1) * HEAD_DIM, :] = (
            acc[h * HEAD_DIM:(h + 1) * HEAD_DIM, h * tq:(h + 1) * tq] / denom[:, h * tq:(h + 1) * tq])
    o_ref[...] = ot_ref[...].T.astype(BF16)


def _attn_b(q_t, q_norm2, k, k_norm2, v_t, bias_tiles, max_bias, starts, variants, n_keys, n_lat):
    bsz, s_tot, _ = k.shape
    tq = TQ_B
    k_scale = jnp.sqrt(jnp.max(k_norm2[:, KN_B:KN_B + B_HEADS, :], axis=-1))
    bound = jnp.sqrt(jnp.max(q_norm2[:, QN_B:QN_B + B_HEADS, :])) * jnp.max(k_scale) + max_bias
    fast = (bound <= MAX_SCORE_BOUND).astype(jnp.int32).reshape(1)
    kern = functools.partial(_attn_b_kernel, n_lat=n_lat, n_ctx=s_tot - n_lat, n_keys=n_keys)
    smem = pl.BlockSpec(memory_space=pltpu.SMEM)
    grid_spec = pltpu.PrefetchScalarGridSpec(
        num_scalar_prefetch=2,
        grid=(bsz, s_tot // tq),
        in_specs=[smem, smem,
                  pl.BlockSpec((None, QB_W, tq), lambda b, i, st, va: (b, 0, i)),
                  pl.BlockSpec((None, QN_ROWS, tq), lambda b, i, st, va: (b, 0, i)),
                  pl.BlockSpec((None, s_tot, KB_W), lambda b, i, st, va: (b, 0, 0)),
                  pl.BlockSpec((None, KB_W, s_tot), lambda b, i, st, va: (b, 0, 0)),
                  pl.BlockSpec((None, n_keys, B_HEADS * tq), lambda b, i, st, va: (va[i], 0, 0))],
        out_specs=pl.BlockSpec((None, tq, QB_W), lambda b, i, st, va: (b, i, 0)),
        scratch_shapes=[pltpu.VMEM((QB_W, B_HEADS * tq), BF16), pltpu.VMEM((QB_W, tq), F32),
                        pltpu.VMEM((1, B_HEADS * tq), F32)],
    )
    return pl.pallas_call(
        kern,
        grid_spec=grid_spec,
        out_shape=jax.ShapeDtypeStruct((bsz, s_tot, QB_W), BF16),
        compiler_params=_params(("parallel", "arbitrary")),
    )(starts, variants, fast, k_scale.reshape(-1), q_t, q_norm2, k, v_t, bias_tiles)


def _attn_c_kernel(fast_ref, ksc_ref, q_ref, qn_ref, k_ref, v_ref, o_ref, qp_ref, m_ref, acc_ref, ot_ref,
                   *, n_lat, n_ctx, tk):
    tq = q_ref.shape[1]
    is_ctx = pl.program_id(1) >= n_lat // tq
    n_chunks = jnp.where(is_ctx, 0, n_lat // (tk * UNROLL_C))
    fast = fast_ref[0] == 1
    first_row = lax.broadcasted_iota(jnp.int32, (KC_W, tq), 0) == 0
    for g in range(C_KV):
        k_scale = ksc_ref[pl.program_id(0) * C_KV + g]
        for hh in range(GQA_GROUP):
            h = g * GQA_GROUP + hh
            qp_ref[g, 0:KC_W, hh * tq:(hh + 1) * tq] = _padded_q(q_ref, h, g)
            offset = jnp.where(fast, -jnp.sqrt(qn_ref[h:h + 1, :]) * k_scale, 0.0)
            qp_ref[g, KC_W:2 * KC_W, hh * tq:(hh + 1) * tq] = jnp.where(first_row, offset, 0.0).astype(BF16)
    acc_ref[...] = jnp.zeros_like(acc_ref)

    def bounded_step(g, start, size):
        s = jnp.dot(k_ref[pl.ds(start, size), :], qp_ref[g], preferred_element_type=F32)
        p = jnp.exp2(s).astype(BF16)
        acc_ref[g] += jnp.dot(v_ref[g, :, pl.ds(start, size)], p, preferred_element_type=F32)

    def online_step(g, start, size):
        s = jnp.dot(k_ref[pl.ds(start, size), :], qp_ref[g], preferred_element_type=F32)
        m_prev = m_ref[g]
        m_new = jnp.maximum(m_prev, jnp.max(s, axis=0, keepdims=True))
        p = jnp.exp2(s - m_new).astype(BF16)
        pv = jnp.dot(v_ref[g, :, pl.ds(start, size)], p, preferred_element_type=F32)
        acc_ref[g] = acc_ref[g] * jnp.exp2(m_prev - m_new) + pv
        m_ref[g] = m_new

    def sweep(step):
        def body(c, carry):
            for u in range(UNROLL_C):
                for g in range(C_KV):
                    step(g, pl.multiple_of((c * UNROLL_C + u) * tk, tk), tk)
            return carry

        lax.fori_loop(0, n_chunks, body, 0)
        for g in range(C_KV):
            step(g, n_lat, n_ctx)

    @pl.when(fast)
    def _():
        sweep(bounded_step)

    @pl.when(jnp.logical_not(fast))
    def _():
        m_ref[...] = jnp.full_like(m_ref, NEG_INF)
        sweep(online_step)

    for g in range(C_KV):
        acc = acc_ref[g]
        out = acc[0:HEAD_DIM, :] / acc[HEAD_DIM:HEAD_DIM + 1, :]
        for hh in range(GQA_GROUP):
            h = g * GQA_GROUP + hh
            ot_ref[h * HEAD_DIM:(h + 1) * HEAD_DIM, :] = out[:, hh * tq:(hh + 1) * tq]
    o_ref[...] = ot_ref[...].T.astype(BF16)


def _attn_c(q_t, q_norm2, k, k_norm2, v_t, n_lat):
    bsz, s_tot, _ = k.shape
    tq = TQ_C
    tk = min(TK_C, n_lat)
    k_scale = jnp.sqrt(jnp.max(k_norm2[:, KN_C:KN_C + C_KV, :], axis=-1)) * BOUND_SLACK
    bound = jnp.sqrt(jnp.max(q_norm2[:, QN_C:QN_C + C_HEADS, :])) * jnp.max(k_scale)
    fast = (bound <= MAX_SCORE_BOUND).astype(jnp.int32).reshape(1)
    kern = functools.partial(_attn_c_kernel, n_lat=n_lat, n_ctx=s_tot - n_lat, tk=tk)
    smem = pl.BlockSpec(memory_space=pltpu.SMEM)
    return pl.pallas_call(
        kern,
        grid=(bsz, s_tot // tq),
        in_specs=[smem, smem,
                  pl.BlockSpec((None, QC_W, tq), lambda b, i: (b, 0, i)),
                  pl.BlockSpec((None, QN_ROWS, tq), lambda b, i: (b, 0, i)),
                  pl.BlockSpec((None, s_tot, 2 * KC_W), lambda b, i: (b, 0, 0)),
                  pl.BlockSpec((None, C_KV, V_ROWS, s_tot), lambda b, i: (b, 0, 0, 0))],
        out_specs=pl.BlockSpec((None, tq, QC_W), lambda b, i: (b, i, 0)),
        out_shape=jax.ShapeDtypeStruct((bsz, s_tot, QC_W), BF16),
        scratch_shapes=[pltpu.VMEM((C_KV, 2 * KC_W, GQA_GROUP * tq), BF16),
                        pltpu.VMEM((C_KV, 1, GQA_GROUP * tq), F32),
                        pltpu.VMEM((C_KV, V_ROWS, GQA_GROUP * tq), F32),
                        pltpu.VMEM((QC_W, tq), F32)],
        compiler_params=_params(("parallel", "arbitrary")),
    )(fast, k_scale.reshape(-1), q_t, q_norm2, k, v_t)


def _store_token_tiles(ref, val, pitch=TILE_ROWS, first=0):
    n = val.shape[0]
    for c in range(TILE_ROWS):
        ref[pl.ds(first + c, n, stride=pitch), :] = val[:, c * 128:(c + 1) * 128]


def _load_token_tiles(ref, n, pitch=TILE_ROWS, first=0):
    return jnp.concatenate([ref[pl.ds(first + c, n, stride=pitch), :] for c in range(TILE_ROWS)], axis=1)


def _tile_copy(src_hbm, dst, idx_ref, j, sem, pitch):
    src = pl.multiple_of(idx_ref[0, 0, j] * pitch, pitch)
    return pltpu.make_async_copy(src_hbm.at[pl.ds(src, pitch), :],
                                 dst.at[pl.ds(pl.multiple_of(j * pitch, pitch), pitch), :], sem)


def _start_gather(src_hbm, dst, idx_ref, n, sem, pitch=TILE_ROWS):
    def issue(i, carry):
        for u in range(GATHER_UNROLL):
            _tile_copy(src_hbm, dst, idx_ref, i * GATHER_UNROLL + u, sem, pitch).start(priority=u % 2)
        return carry

    lax.fori_loop(0, n // GATHER_UNROLL, issue, 0)


def _wait_gather(src_hbm, dst, n, sem, pitch=TILE_ROWS):
    pltpu.make_async_copy(src_hbm.at[pl.ds(0, n * pitch), :], dst, sem).wait()


def _layer_norm(h, g, b):
    mu = jnp.mean(h, axis=-1, keepdims=True)
    hc = h - mu
    var = jnp.mean(hc * hc, axis=-1, keepdims=True)
    return hc * lax.rsqrt(var + LN_EPS) * g + b


def _post_kernel(oa_ref, ob_ref, oc_ref, x_ref, mod_ref, w_ref, g_ref, b_ref, wr_ref, rb_ref,
                 x1_ref, u2_ref, eid_ref, gate_ref, o_scr, ub_scr, *, alpha):
    tm = x_ref.shape[0]
    o_cat = jnp.concatenate([oa_ref[...], ob_ref[...], oc_ref[...]], axis=1)
    o_scr[...] = jnp.dot(o_cat, w_ref[...], preferred_element_type=F32)
    gate1 = 1.0 + mod_ref[2:3, :]
    scale2 = 1.0 + mod_ref[4:5, :]
    shift2 = mod_ref[3:4, :]
    ln_g = g_ref[...]
    ln_b = b_ref[...]

    def row_group(r, carry):
        r0 = pl.multiple_of(r * LN_ROWS, LN_ROWS)
        rows = pl.ds(r0, LN_ROWS)
        x1 = _layer_norm(alpha * x_ref[rows, :] + gate1 * o_scr[rows, :], ln_g, ln_b)
        x1_ref[rows, :] = x1
        u2 = x1 * scale2 + shift2
        ub_scr[rows, :] = u2.astype(BF16)
        for c in range(TILE_ROWS):
            u2_ref[pl.ds(r0 * TILE_ROWS + c, LN_ROWS, stride=TILE_ROWS), :] = u2[:, c * 128:(c + 1) * 128]
        return carry

    lax.fori_loop(0, tm // LN_ROWS, row_group, 0, unroll=LN_UNROLL)
    logits = lax.dot_general(wr_ref[...], ub_scr[...], (((1,), (1,)), ((), ())), preferred_element_type=F32)
    aff_all = jax.nn.sigmoid(logits)
    sel_all = aff_all + rb_ref[...]
    aff = [aff_all[e:e + 1, :] for e in range(N_EXPERTS)]
    sel = [sel_all[e:e + 1, :] for e in range(N_EXPERTS)]
    gsum = []
    for g in range(N_EXPERT_GROUPS):
        a, b, c, d = sel[g * PER_GROUP:(g + 1) * PER_GROUP]
        hi1, lo1, hi2, lo2 = jnp.maximum(a, b), jnp.minimum(a, b), jnp.maximum(c, d), jnp.minimum(c, d)
        gsum.append(jnp.maximum(hi1, hi2) + jnp.maximum(jnp.minimum(hi1, hi2), jnp.maximum(lo1, lo2)))
    best_g = jnp.zeros_like(gsum[0], dtype=jnp.int32)
    best_v = gsum[0]
    for g in range(1, N_EXPERT_GROUPS):
        better = gsum[g] > best_v
        best_g = jnp.where(better, g, best_g)
        best_v = jnp.where(better, gsum[g], best_v)
    s_loc, a_loc = [], []
    for j in range(PER_GROUP):
        sj, aj = sel[j], aff[j]
        for g in range(1, N_EXPERT_GROUPS):
            pick = best_g == g
            sj = jnp.where(pick, sel[g * PER_GROUP + j], sj)
            aj = jnp.where(pick, aff[g * PER_GROUP + j], aj)
        s_loc.append(sj)
        a_loc.append(aj)
    i1 = jnp.zeros_like(best_g)
    v1, g1 = s_loc[0], a_loc[0]
    for j in range(1, PER_GROUP):
        better = s_loc[j] > v1
        i1 = jnp.where(better, j, i1)
        v1 = jnp.where(better, s_loc[j], v1)
        g1 = jnp.where(better, a_loc[j], g1)
    i2 = jnp.full_like(best_g, -1)
    v2 = jnp.full_like(v1, -jnp.inf)
    g2 = jnp.zeros_like(g1)
    for j in range(PER_GROUP):
        better = (i1 != j) & ((s_loc[j] > v2) | (i2 < 0))
        i2 = jnp.where(better, j, i2)
        v2 = jnp.where(better, s_loc[j], v2)
        g2 = jnp.where(better, a_loc[j], g2)
    total = g1 + g2
    lo, hi = jnp.minimum(i1, i2), jnp.maximum(i1, i2)
    pair = jnp.where(lo == 0, hi - 1, jnp.where(lo == 1, hi + 1, PAIRS_PER_GROUP - 1))
    first_is_lo = i1 < i2
    g_lo = jnp.where(first_is_lo, g1, g2) / total
    g_hi = jnp.where(first_is_lo, g2, g1) / total
    eid_ref[...] = jnp.concatenate([best_g * PAIRS_PER_GROUP + pair,
                                    jnp.zeros((MOD_ROWS - 1, tm), jnp.int32)], axis=0)
    gate_ref[...] = jnp.concatenate([g_lo, g_hi, jnp.zeros((MOD_ROWS - TOP_K, tm), F32)], axis=0)


def _post(oa, ob, oc, xa, mod, w_out, ln_g, ln_b, wr_t, rb, n_lat, alpha):
    bsz, s_tot, d = xa.shape
    tm = TOK_TILE
    n_lat_tiles = n_lat // tm
    n_tiles = s_tot // tm
    sd = jax.ShapeDtypeStruct

    def tmajor(width):
        return pl.BlockSpec((None, tm, width), lambda b, i: (b, i, 0))

    def const(shape):
        return pl.BlockSpec(shape, lambda b, i: (0,) * len(shape))

    lanes = pl.BlockSpec((MOD_ROWS, tm), lambda b, i: (0, b * n_tiles + i))
    return pl.pallas_call(
        functools.partial(_post_kernel, alpha=alpha),
        grid=(bsz, n_tiles),
        in_specs=[tmajor(QA_W), tmajor(QB_W), tmajor(QC_W), tmajor(d),
                  pl.BlockSpec((None, N_MOD, d), lambda b, i: (jnp.where(i >= n_lat_tiles, bsz, b), 0, 0)),
                  const((d, d)), const((1, d)), const((1, d)), const((N_EXPERTS, d)), const((N_EXPERTS, 1))],
        out_specs=[tmajor(d), pl.BlockSpec((tm * TILE_ROWS, 128), lambda b, i: (b * n_tiles + i, 0)), lanes, lanes],
        out_shape=[sd((bsz, s_tot, d), F32), sd((bsz * s_tot * TILE_ROWS, 128), F32),
                   sd((MOD_ROWS, bsz * s_tot), jnp.int32), sd((MOD_ROWS, bsz * s_tot), F32)],
        scratch_shapes=[pltpu.VMEM((tm, d), F32), pltpu.VMEM((tm, d), BF16)],
        compiler_params=_params(("parallel", "parallel")),
    )(oa, ob, oc, xa, mod, w_out, ln_g, ln_b, wr_t, rb)


def _ffn_kernel(elo_ref, ehi_ref, nused_ref, tok_ref, tok_next_ref, u_hbm,
                w1lo_ref, w3lo_ref, w2lo_ref, w1hi_ref, w3hi_ref, w2hi_ref, y_ref,
                xbuf, w1lo, w3lo, w2lo, w1hi, w3hi, w2hi, sem):
    blk = y_ref.shape[0] // (TOP_K * TILE_ROWS)
    step = pl.program_id(0)
    n_used = nused_ref[0]
    slot = step % 2
    prev = jnp.maximum(step - 1, 0)

    @pl.when(step < n_used)
    def _():
        @pl.when(step == 0)
        def _():
            _start_gather(u_hbm, xbuf.at[0], tok_ref, blk, sem.at[0])

        @pl.when(step + 1 < n_used)
        def _():
            _start_gather(u_hbm, xbuf.at[1 - slot], tok_next_ref, blk, sem.at[1 - slot])

        @pl.when((step == 0) | (elo_ref[step] != elo_ref[prev]))
        def _():
            w1lo[...] = w1lo_ref[...].astype(BF16)
            w3lo[...] = w3lo_ref[...].astype(BF16)
            w2lo[...] = w2lo_ref[...].astype(BF16)

        @pl.when((step == 0) | (ehi_ref[step] != ehi_ref[prev]))
        def _():
            w1hi[...] = w1hi_ref[...].astype(BF16)
            w3hi[...] = w3hi_ref[...].astype(BF16)
            w2hi[...] = w2hi_ref[...].astype(BF16)

        _wait_gather(u_hbm, xbuf.at[slot], blk, sem.at[slot])
        xb = _load_token_tiles(xbuf.at[slot], blk).astype(BF16)
        for first, (w1b, w3b, w2b) in ((0, (w1lo, w3lo, w2lo)), (TILE_ROWS, (w1hi, w3hi, w2hi))):
            h1 = jnp.dot(xb, w1b[...], preferred_element_type=F32)
            h3 = jnp.dot(xb, w3b[...], preferred_element_type=F32)
            hid = (h1 * jax.nn.sigmoid(h1) * h3).astype(BF16)
            _store_token_tiles(y_ref, jnp.dot(hid, w2b[...], preferred_element_type=F32),
                               pitch=TOP_K * TILE_ROWS, first=first)

    @pl.when(step >= n_used)
    def _():
        y_ref[...] = jnp.zeros_like(y_ref)


def _expert_ffn(block_lo, block_hi, n_used, slot_tok, u2_tiles, w1, w3, w2, layer):
    d = w1.shape[2]
    n_blocks = block_lo.shape[0]
    blk = CLASS_BLOCK
    slot_tok = slot_tok.reshape(n_blocks, 1, blk)

    def weight(shape, which):
        return pl.BlockSpec((None, None) + shape, lambda i, lo, hi, nu: (layer, (lo, hi)[which][i], 0, 0))

    up, down = (d, D_EXPERT), (D_EXPERT, d)
    grid_spec = pltpu.PrefetchScalarGridSpec(
        num_scalar_prefetch=3,
        grid=(n_blocks,),
        in_specs=[pl.BlockSpec((1, 1, blk), lambda i, lo, hi, nu: (i, 0, 0), memory_space=pltpu.SMEM),
                  pl.BlockSpec((1, 1, blk), lambda i, lo, hi, nu: (jnp.minimum(i + 1, n_blocks - 1), 0, 0),
                               memory_space=pltpu.SMEM),
                  pl.BlockSpec(memory_space=pl.ANY),
                  weight(up, 0), weight(up, 0), weight(down, 0), weight(up, 1), weight(up, 1), weight(down, 1)],
        out_specs=pl.BlockSpec((blk * TOP_K * TILE_ROWS, 128), lambda i, lo, hi, nu: (i, 0)),
        scratch_shapes=[pltpu.VMEM((2, blk * TILE_ROWS, 128), F32)]
        + [pltpu.VMEM(s, BF16) for s in (up, up, down, up, up, down)] + [pltpu.SemaphoreType.DMA((2,))],
    )
    return pl.pallas_call(
        _ffn_kernel,
        grid_spec=grid_spec,
        out_shape=jax.ShapeDtypeStruct((n_blocks * blk * TOP_K * TILE_ROWS, 128), F32),
        compiler_params=_params(("arbitrary",)),
    )(block_lo, block_hi, n_used, slot_tok, slot_tok, u2_tiles, w1, w3, w2, w1, w3, w2)


def _comb_kernel(pos_ref, pos_next_ref, y_hbm, x1_ref, gate_ref, mod_ref, g_ref, b_ref, o_ref, ybuf, sem, *, alpha):
    tm = x1_ref.shape[0]
    pitch = TOP_K * TILE_ROWS
    step = pl.program_id(0)
    slot = step % 2

    @pl.when(step == 0)
    def _():
        _start_gather(y_hbm, ybuf.at[0], pos_ref, tm, sem.at[0], pitch)

    @pl.when(step + 1 < pl.num_programs(0))
    def _():
        _start_gather(y_hbm, ybuf.at[1 - slot], pos_next_ref, tm, sem.at[1 - slot], pitch)

    _wait_gather(y_hbm, ybuf.at[slot], tm, sem.at[slot], pitch)
    y_lo = _load_token_tiles(ybuf.at[slot], tm, pitch, 0)
    y_hi = _load_token_tiles(ybuf.at[slot], tm, pitch, TILE_ROWS)
    f = gate_ref[:, 0:1] * y_lo + gate_ref[:, 1:2] * y_hi
    o_ref[...] = _layer_norm(alpha * x1_ref[...] + (1.0 + mod_ref[5:6, :]) * f, g_ref[...], b_ref[...])


def _combine(pos, y_tiles, x1, gates, mod, ln_g, ln_b, n_lat, alpha):
    bsz, s_tot, d = x1.shape
    tm = TOK_TILE
    n_lat_tiles = n_lat // tm
    n_tiles = s_tot // tm
    n_steps = bsz * n_tiles

    def mod_row(t):
        return jnp.where(t % n_tiles >= n_lat_tiles, bsz, t // n_tiles)

    out = pl.pallas_call(
        functools.partial(_comb_kernel, alpha=alpha),
        grid=(n_steps,),
        in_specs=[pl.BlockSpec((1, 1, tm), lambda t: (t, 0, 0), memory_space=pltpu.SMEM),
                  pl.BlockSpec((1, 1, tm), lambda t: (jnp.minimum(t + 1, n_steps - 1), 0, 0),
                               memory_space=pltpu.SMEM),
                  pl.BlockSpec(memory_space=pl.ANY),
                  pl.BlockSpec((tm, d), lambda t: (t, 0)),
                  pl.BlockSpec((tm, TOP_K), lambda t: (t, 0)),
                  pl.BlockSpec((None, N_MOD, d), lambda t: (mod_row(t), 0, 0)),
                  pl.BlockSpec((1, d), lambda t: (0, 0)),
                  pl.BlockSpec((1, d), lambda t: (0, 0))],
        out_specs=pl.BlockSpec((tm, d), lambda t: (t, 0)),
        out_shape=jax.ShapeDtypeStruct((bsz * s_tot, d), F32),
        scratch_shapes=[pltpu.VMEM((2, TOP_K * tm * TILE_ROWS, 128), F32), pltpu.SemaphoreType.DMA((2,))],
        compiler_params=_params(("arbitrary",)),
    )(pos, pos, y_tiles, x1.reshape(bsz * s_tot, d), gates, mod, ln_g, ln_b)
    return out.reshape(bsz, s_tot, d)


def _deinterleave(n_heads):
    one = np.concatenate([np.arange(0, HEAD_DIM, 2), np.arange(1, HEAD_DIM, 2)])
    return np.concatenate([h * HEAD_DIM + one for h in range(n_heads)])


def _proj_weight(w_in_l):
    sizes = (QA_W, KA_W, KA_W, QB_W, KB_W, KB_W, QC_W, KC_W, KC_W)
    offs = np.concatenate([[0], np.cumsum(sizes)])
    qa, ka, va, qb, kb, vb, qc, kc, vc = [w_in_l[:, offs[i]:offs[i + 1]] for i in range(len(sizes))]
    scale = HEAD_DIM ** -0.5 * LOG2_E
    qa = qa[:, _deinterleave(A_HEADS)] * scale
    ka = ka[:, _deinterleave(A_KV)]
    qb = qb * scale
    qc = qc[:, _deinterleave(C_HEADS)]
    kc = kc[:, _deinterleave(C_KV)]
    return jnp.concatenate([qa, qb, qc, ka, kb, kc, va, vb, vc], axis=1).T.astype(BF16)


def _rope_tables(n_lat, n_ctx):
    t = np.arange(n_lat)
    inv_freq = ROPE_BASE ** (-np.arange(0, HALF, 2, dtype=np.float32) / HALF)
    ang = jnp.concatenate([jnp.asarray(t // GRID_W, F32)[None, :] * jnp.asarray(inv_freq)[:, None],
                           jnp.asarray(t % GRID_W, F32)[None, :] * jnp.asarray(inv_freq)[:, None]], axis=0)
    cos = jnp.concatenate([jnp.cos(ang), jnp.ones((HALF, n_ctx), F32)], axis=1)
    sin = jnp.concatenate([jnp.sin(ang), jnp.zeros((HALF, n_ctx), F32)], axis=1)
    return cos, sin


def _dispatch(cls, n_tok):
    onehot = (cls[:, None] == jnp.arange(N_CLASSES, dtype=jnp.int32)[None, :]).astype(jnp.int32)
    csum = jnp.cumsum(onehot, axis=0)
    counts = csum[-1]
    padded = (counts + CLASS_BLOCK - 1) // CLASS_BLOCK * CLASS_BLOCK
    pad_ends = jnp.cumsum(padded)
    dest = jnp.sum(onehot * (csum - 1 + (pad_ends - padded)[None, :]), axis=1)
    n_blocks = -(-n_tok // CLASS_BLOCK) + N_CLASSES
    slot_tok = jnp.zeros((n_blocks * CLASS_BLOCK,), jnp.int32).at[dest].set(jnp.arange(n_tok, dtype=jnp.int32))
    block_start = jnp.arange(n_blocks, dtype=jnp.int32) * CLASS_BLOCK
    block_cls = jnp.minimum(jnp.sum((block_start[:, None] >= pad_ends[None, :]).astype(jnp.int32), axis=1),
                            N_CLASSES - 1)
    group, pair = block_cls // PAIRS_PER_GROUP, block_cls % PAIRS_PER_GROUP
    lo = (pair >= 3).astype(jnp.int32) + (pair >= 5).astype(jnp.int32)
    hi = jnp.where(pair < 3, pair + 1, jnp.where(pair < 5, pair - 1, 3))
    n_used = (pad_ends[-1] // CLASS_BLOCK).astype(jnp.int32).reshape(1)
    return slot_tok, group * PER_GROUP + lo, group * PER_GROUP + hi, dest.astype(jnp.int32), n_used


def kernel(x, c, ctx, c_ctx, w_in, w_out, sink, rpb, q_gain, k_gain, w_ada, b_ada,
           ln1_g, ln1_b, ln2_g, ln2_b, w_router, router_bias, w1, w3, w2):
    bsz, n_lat, d = x.shape
    n_ctx = ctx.shape[1]
    s_tot = n_lat + n_ctx
    depth = w_in.shape[0]
    assert d == D_MODEL and n_ctx == TOK_TILE and n_lat % min(TK_C, n_lat) == 0
    assert n_lat % (2 * TOK_TILE) == 0 and bsz + 1 <= MOD_ROWS
    alpha = float((2 * depth) ** 0.25)
    n_tok = bsz * s_tot
    tm = TOK_TILE

    cond = jnp.zeros((MOD_ROWS, d), F32).at[:bsz].set(c).at[bsz].set(c_ctx)
    mods = _ada_table(cond, w_ada, b_ada).reshape(depth, MOD_ROWS, N_MOD, d)
    cos_t, sin_t = _rope_tables(n_lat, n_ctx)
    gain_perm = _deinterleave(1)
    wr_t = w_router.T.astype(BF16)
    rb = router_bias.reshape(N_EXPERTS, 1).astype(F32)

    bias_tiles, starts, variants, n_keys = _nb_bias_tiles(rpb, n_lat, s_tot)

    xa = jnp.concatenate([x, ctx], axis=1)
    for l in range(depth):
        mod = mods[l]
        qa, qb, qc, ka, kb, kc, va, vb, vc, qn, kn = _project(
            xa, mod, _proj_weight(w_in[l]), cos_t, sin_t,
            q_gain[l][gain_perm].reshape(HEAD_DIM, 1) * (HEAD_DIM ** -0.5 * LOG2_E),
            k_gain[l][gain_perm].reshape(HEAD_DIM, 1), n_lat)
        oa = _attn_a(sink[l], qa, qn, ka, kn, va, n_lat)
        ob = _attn_b(qb, qn, kb, kn, vb, bias_tiles[l], jnp.max(jnp.abs(rpb[l])) * LOG2_E,
                     starts, variants, n_keys, n_lat)
        oc = _attn_c(qc, qn, kc, kn, vc, n_lat)
        x1, u2, eid8, gate8 = _post(oa, ob, oc, xa, mod, w_out[l].astype(BF16),
                                    ln1_g[l].reshape(1, d), ln1_b[l].reshape(1, d), wr_t, rb, n_lat, alpha)
        slot_tok, block_lo, block_hi, dest, n_used = _dispatch(eid8[0], n_tok)
        y = _expert_ffn(block_lo, block_hi, n_used, slot_tok, u2, w1, w3, w2, l)
        xa = _combine(dest.reshape(n_tok // tm, 1, tm), y, x1, gate8[:TOP_K].T, mod,
                      ln2_g[l].reshape(1, d), ln2_b[l].reshape(1, d), n_lat, alpha)
    return xa[:, :n_lat]
```

```python
import functools

import numpy as np
import jax
import jax.numpy as jnp
from jax import lax
from jax.experimental import pallas as pl
from jax.experimental.pallas import tpu as pltpu
from jax.experimental.pallas import tpu_sc as plsc

F32 = jnp.float32
BF16 = jnp.bfloat16

D_MODEL = 1024
HEAD_DIM = 64
HALF = HEAD_DIM // 2
GRID_W = 64
A_HEADS, A_KV = 6, 2
WINDOW = 128
B_HEADS = 4
NA_ROWS, NA_COLS = 8, 16
C_HEADS, C_KV = 6, 2
GQA_GROUP = 3
ROPE_BASE = 10000.0
N_EXPERTS = 16
N_EXPERT_GROUPS = 4
PER_GROUP = N_EXPERTS // N_EXPERT_GROUPS
TOP_K = 2
D_EXPERT = 512
PAIRS_PER_GROUP = PER_GROUP * (PER_GROUP - 1) // 2
N_CLASSES = N_EXPERT_GROUPS * PAIRS_PER_GROUP
CLASS_BLOCK = 256
N_MOD = 6
LN_EPS = 1e-6
QK_EPS = 1e-6
NEG_INF = -1e30

QA_W, QB_W, QC_W = A_HEADS * HEAD_DIM, B_HEADS * HEAD_DIM, C_HEADS * HEAD_DIM
KA_W, KB_W, KC_W = A_KV * HEAD_DIM, B_HEADS * HEAD_DIM, C_KV * HEAD_DIM
R_QA = 0
R_QB = R_QA + QA_W
R_QC = R_QB + QB_W
R_KA = R_QC + QC_W
R_KB = R_KA + KA_W
R_KC = R_KB + KB_W
R_VA = R_KC + KC_W
R_VB = R_VA + KA_W
R_VC = R_VB + KB_W
PROJ_WIDTH = R_VC + KC_W

V7X_VMEM_LIMIT = 56 * 1024 * 1024
TOK_TILE = 256
TQ_A = 256
TQ_B = 2 * GRID_W
TQ_C = 256
TK_C = 2048
MOD_ROWS = 8
LOG2_E = 1.4426950408889634
QN_C, QN_A, QN_B, QN_ROWS = 0, C_HEADS, C_HEADS + A_HEADS, C_HEADS + A_HEADS + B_HEADS
KN_C, KN_A, KN_B, KN_ROWS = 0, C_KV, C_KV + A_KV, C_KV + A_KV + B_HEADS
MAX_SCORE_BOUND = 60.0
BOUND_SLACK = 1.004
TILE_ROWS = 8
BF16_ROWS = 16
SC_CORES, SC_SUBCORES, SC_ALIGN = 2, 16, 8
SC_TILE_BYTES = 320 * 1024
LN_ROWS = 64
LN_UNROLL = 4
V_ROWS = HEAD_DIM + BF16_ROWS


def _params(sem):
    return pltpu.CompilerParams(dimension_semantics=sem, vmem_limit_bytes=V7X_VMEM_LIMIT)


def _ada_kernel(c_ref, w_ref, b_ref, o_ref):
    c = c_ref[...]
    act = c * jax.nn.sigmoid(c)
    o_ref[...] = jnp.dot(act, w_ref[...], preferred_element_type=F32,
                         precision=lax.Precision.HIGHEST) + b_ref[...]


def _ada_table(cond, w_ada, b_ada):
    depth, d, n = w_ada.shape
    tn = n // 4
    return pl.pallas_call(
        _ada_kernel,
        grid=(depth, n // tn),
        in_specs=[pl.BlockSpec((MOD_ROWS, d), lambda l, j: (0, 0)),
                  pl.BlockSpec((None, d, tn), lambda l, j: (l, 0, j)),
                  pl.BlockSpec((None, 1, tn), lambda l, j: (l, 0, j))],
        out_specs=pl.BlockSpec((None, MOD_ROWS, tn), lambda l, j: (l, 0, j)),
        out_shape=jax.ShapeDtypeStruct((depth, MOD_ROWS, n), F32),
        compiler_params=_params(("parallel", "parallel")),
    )(cond, w_ada, b_ada.reshape(depth, 1, n))


def _sq_norm(parts):
    return sum(jnp.sum(jnp.square(p.astype(F32)), axis=0, keepdims=True) for p in parts)


def _proj_kernel(x_ref, mod_ref, w_ref, cos_ref, sin_ref, qg_ref, kg_ref,
                 qa_ref, qb_ref, qc_ref, ka_ref, kb_ref, kc_ref, va_ref, vb_ref, vc_ref, qn_ref, kn_ref):
    tm = x_ref.shape[0]
    u = (x_ref[...] * (1.0 + mod_ref[1:2, :]) + mod_ref[0:1, :]).astype(BF16)

    def proj(lo, width):
        return lax.dot_general(w_ref[lo:lo + width, :], u, (((1,), (1,)), ((), ())),
                               preferred_element_type=F32)

    cos = cos_ref[...]
    sin = sin_ref[...]

    def rope(blk):
        x0, x1 = blk[:HALF], blk[HALF:]
        return x0 * cos - x1 * sin, x0 * sin + x1 * cos

    def qk_norm(blk, gain):
        ms = jnp.mean(blk * blk, axis=0, keepdims=True)
        return blk * lax.rsqrt(ms + QK_EPS) * gain

    p = proj(R_QA, QA_W)
    for h in range(A_HEADS):
        o0, o1 = [o.astype(BF16) for o in rope(p[h * HEAD_DIM:(h + 1) * HEAD_DIM])]
        qa_ref[h * HEAD_DIM:h * HEAD_DIM + HALF, :] = o0
        qa_ref[h * HEAD_DIM + HALF:(h + 1) * HEAD_DIM, :] = o1
        qn_ref[QN_A + h:QN_A + h + 1, :] = _sq_norm([o0, o1])
    p = proj(R_QB, QB_W).astype(BF16)
    qb_ref[...] = p
    for h in range(B_HEADS):
        qn_ref[QN_B + h:QN_B + h + 1, :] = _sq_norm([p[h * HEAD_DIM:(h + 1) * HEAD_DIM]])
    p = proj(R_QC, QC_W)
    qg = qg_ref[...]
    for h in range(C_HEADS):
        o0, o1 = [o.astype(BF16) for o in rope(qk_norm(p[h * HEAD_DIM:(h + 1) * HEAD_DIM], qg))]
        qc_ref[h * HEAD_DIM:h * HEAD_DIM + HALF, :] = o0
        qc_ref[h * HEAD_DIM + HALF:(h + 1) * HEAD_DIM, :] = o1
        qn_ref[QN_C + h:QN_C + h + 1, :] = _sq_norm([o0, o1])
    one_lane = (lax.broadcasted_iota(jnp.int32, (tm, KA_W), 1) == 0).astype(BF16)
    p = proj(R_KA, KA_W)
    parts = []
    for h in range(A_KV):
        o0, o1 = [o.astype(BF16) for o in rope(p[h * HEAD_DIM:(h + 1) * HEAD_DIM])]
        kn_ref[KN_A + h:KN_A + h + 1, :] = _sq_norm([o0, o1])
        parts.extend([o0.astype(F32), o1.astype(F32)])
    ka_ref[:, 0:KA_W] = jnp.concatenate(parts, axis=0).T.astype(BF16)
    ka_ref[:, KA_W:2 * KA_W] = one_lane
    p = proj(R_KB, KB_W).astype(BF16)
    for h in range(B_HEADS):
        kn_ref[KN_B + h:KN_B + h + 1, :] = _sq_norm([p[h * HEAD_DIM:(h + 1) * HEAD_DIM]])
    kb_ref[...] = p.astype(F32).T.astype(BF16)
    p = proj(R_KC, KC_W)
    kg = kg_ref[...]
    parts = []
    for h in range(C_KV):
        o0, o1 = [o.astype(BF16) for o in rope(qk_norm(p[h * HEAD_DIM:(h + 1) * HEAD_DIM], kg))]
        kn_ref[KN_C + h:KN_C + h + 1, :] = _sq_norm([o0, o1])
        parts.extend([o0.astype(F32), o1.astype(F32)])
    kc_ref[:, 0:KC_W] = jnp.concatenate(parts, axis=0).T.astype(BF16)
    kc_ref[:, KC_W:2 * KC_W] = one_lane
    ones_row = (lax.broadcasted_iota(jnp.int32, (BF16_ROWS, tm), 0) == 0).astype(BF16)
    for lo, ref in ((R_VA, va_ref), (R_VC, vc_ref)):
        p = proj(lo, KA_W)
        for g in range(A_KV):
            ref[g, 0:HEAD_DIM, :] = p[g * HEAD_DIM:(g + 1) * HEAD_DIM].astype(BF16)
            ref[g, HEAD_DIM:V_ROWS, :] = ones_row
    vb_ref[...] = proj(R_VB, KB_W).astype(BF16)


def _project(xa, mod, w_t, cos_t, sin_t, q_gain, k_gain, n_lat):
    bsz, s_tot, d = xa.shape
    tm = TOK_TILE
    n_lat_tiles = n_lat // tm
    sd = jax.ShapeDtypeStruct

    def dmajor(width):
        return pl.BlockSpec((None, width, tm), lambda b, i: (b, 0, i))

    def tmajor(width):
        return pl.BlockSpec((None, tm, width), lambda b, i: (b, i, 0))

    aug = pl.BlockSpec((None, A_KV, V_ROWS,tm), lambda b, i: (b, 0, 0, i))
    return pl.pallas_call(
        _proj_kernel,
        grid=(bsz, s_tot // tm),
        in_specs=[tmajor(d),
                  pl.BlockSpec((None, N_MOD, d), lambda b, i: (jnp.where(i >= n_lat_tiles, bsz, b), 0, 0)),
                  pl.BlockSpec((PROJ_WIDTH, d), lambda b, i: (0, 0)),
                  pl.BlockSpec((HALF, tm), lambda b, i: (0, i)),
                  pl.BlockSpec((HALF, tm), lambda b, i: (0, i)),
                  pl.BlockSpec((HEAD_DIM, 1), lambda b, i: (0, 0)),
                  pl.BlockSpec((HEAD_DIM, 1), lambda b, i: (0, 0))],
        out_specs=[dmajor(QA_W), dmajor(QB_W), dmajor(QC_W),
                   tmajor(2 * KA_W), tmajor(KB_W), tmajor(2 * KC_W),
                   aug, dmajor(KB_W), aug, dmajor(QN_ROWS), dmajor(KN_ROWS)],
        out_shape=[sd((bsz, QA_W, s_tot), BF16), sd((bsz, QB_W, s_tot), BF16), sd((bsz, QC_W, s_tot), BF16),
                   sd((bsz, s_tot, 2 * KA_W), BF16), sd((bsz, s_tot, KB_W), BF16), sd((bsz, s_tot, 2 * KC_W), BF16),
                   sd((bsz, A_KV, V_ROWS, s_tot), BF16), sd((bsz, KB_W, s_tot), BF16),
                   sd((bsz, A_KV, V_ROWS, s_tot), BF16),
                   sd((bsz, QN_ROWS, s_tot), F32), sd((bsz, KN_ROWS, s_tot), F32)],
        compiler_params=_params(("parallel", "parallel")),
    )(xa, mod, w_t, cos_t, sin_t, q_gain, k_gain)


def _padded_q(q_ref, h, g):
    blk = q_ref[h * HEAD_DIM:(h + 1) * HEAD_DIM, :]
    zeros = jnp.zeros_like(blk)
    return jnp.concatenate([blk, zeros] if g == 0 else [zeros, blk], axis=0)


def _attn_a_kernel(fast_ref, ksc_ref, sink_ref, q_ref, qn_ref, k_ref, v_ref, o_ref, ot_ref, *, n_lat, n_ctx):
    tq = q_ref.shape[1]
    span = tq + 2 * WINDOW
    start = pl.program_id(1) * tq
    ks = pl.multiple_of(jnp.clip(start - WINDOW, 0, n_lat - span), 128)
    kpos = ks + lax.broadcasted_iota(jnp.int32, (span, tq), 0)
    qpos = start + lax.broadcasted_iota(jnp.int32, (span, tq), 1)
    ok = (jnp.abs(qpos - kpos) <= WINDOW) & (qpos < n_lat)
    k_loc = k_ref[pl.ds(ks, span), :]
    k_ctx = k_ref[n_lat:n_lat + n_ctx, :]
    first_row = lax.broadcasted_iota(jnp.int32, (KA_W, tq), 0) == 0

    def head(h, g, bounded):
        v_loc = v_ref[g, :, pl.ds(ks, span)]
        v_ctx = v_ref[g, :, n_lat:n_lat + n_ctx]
        sink = sink_ref[h]
        if bounded:
            offset = jnp.sqrt(qn_ref[QN_A + h:QN_A + h + 1, :]) * ksc_ref[pl.program_id(0) * A_KV + g]
            qp = jnp.concatenate([_padded_q(q_ref, h, g), jnp.where(first_row, -offset, 0.0).astype(BF16)], axis=0)
        else:
            qp = jnp.concatenate([_padded_q(q_ref, h, g), jnp.zeros((KA_W, tq), BF16)], axis=0)
        s_loc = jnp.where(ok, jnp.dot(k_loc, qp, preferred_element_type=F32), NEG_INF)
        s_ctx = jnp.dot(k_ctx, qp, preferred_element_type=F32)
        if bounded:
            m = offset
        else:
            m = jnp.maximum(jnp.maximum(jnp.max(s_loc, axis=0, keepdims=True),
                                        jnp.max(s_ctx, axis=0, keepdims=True)), sink)
            s_loc, s_ctx = s_loc - m, s_ctx - m
        acc = (jnp.dot(v_loc, jnp.exp2(s_loc).astype(BF16), preferred_element_type=F32)
               + jnp.dot(v_ctx, jnp.exp2(s_ctx).astype(BF16), preferred_element_type=F32))
        denom = acc[HEAD_DIM:HEAD_DIM + 1, :] + jnp.exp2(sink - m)
        ot_ref[h * HEAD_DIM:(h + 1) * HEAD_DIM, :] = acc[0:HEAD_DIM, :] / denom

    for bounded in (True, False):
        @pl.when((fast_ref[0] == 1) == bounded)
        def _():
            for g in range(A_KV):
                for hh in range(GQA_GROUP):
                    head(g * GQA_GROUP + hh, g, bounded)

    o_ref[...] = ot_ref[...].T.astype(BF16)


def _attn_a(sink, q_t, q_norm2, k, k_norm2, v_t, n_lat):
    bsz, s_tot, _ = k.shape
    tq = TQ_A
    sink2 = sink.astype(F32) * LOG2_E
    k_scale = jnp.sqrt(jnp.max(k_norm2[:, KN_A:KN_A + A_KV, :], axis=-1)) * BOUND_SLACK
    bound = jnp.sqrt(jnp.max(q_norm2[:, QN_A:QN_A + A_HEADS, :])) * jnp.max(k_scale)
    fast = ((bound <= MAX_SCORE_BOUND) & (jnp.max(jnp.abs(sink2)) <= MAX_SCORE_BOUND)).astype(jnp.int32).reshape(1)
    kern = functools.partial(_attn_a_kernel, n_lat=n_lat, n_ctx=s_tot - n_lat)
    smem = pl.BlockSpec(memory_space=pltpu.SMEM)
    return pl.pallas_call(
        kern,
        grid=(bsz, s_tot // tq),
        in_specs=[smem, smem, smem,
                  pl.BlockSpec((None, QA_W, tq), lambda b, i: (b, 0, i)),
                  pl.BlockSpec((None, QN_ROWS, tq), lambda b, i: (b, 0, i)),
                  pl.BlockSpec((None, s_tot, 2 * KA_W), lambda b, i: (b, 0, 0)),
                  pl.BlockSpec((None, A_KV, V_ROWS, s_tot), lambda b, i: (b, 0, 0, 0))],
        out_specs=pl.BlockSpec((None, tq, QA_W), lambda b, i: (b, i, 0)),
        out_shape=jax.ShapeDtypeStruct((bsz, s_tot, QA_W), BF16),
        scratch_shapes=[pltpu.VMEM((QA_W, tq), F32)],
        compiler_params=_params(("parallel", "arbitrary")),
    )(fast, k_scale.reshape(-1), sink2, q_t, q_norm2, k, v_t)


def _nb_plan(n_lat, s_tot):
    rows = n_lat // GRID_W
    kh = min(NA_ROWS, rows)
    q_rows = TQ_B // GRID_W
    w_rows = kh + q_rows
    n_lat_blocks = n_lat // TQ_B
    n_blocks = s_tot // TQ_B
    starts, variants, keys, reps = [], [], {}, []
    for i in range(n_lat_blocks):
        w0 = int(np.clip(q_rows * i - kh // 2, 0, rows - w_rows))
        r0s = tuple(int(np.clip(q_rows * i + j - kh // 2, 0, rows - kh)) - w0 for j in range(q_rows))
        key = (q_rows * i - w0,) + r0s
        if key not in keys:
            keys[key] = len(reps)
            reps.append((i, w0))
        starts.append(w0 * GRID_W)
        variants.append(keys[key])
    masked = len(reps)
    starts.extend([0] * (n_blocks - n_lat_blocks))
    variants.extend([masked] * (n_blocks - n_lat_blocks))
    return rows, kh, w_rows, reps, np.asarray(starts, np.int32), np.asarray(variants, np.int32)


def _rel_bias_kernel(r_ref, e_ref, m_ref, o_ref):
    o_ref[...] = jnp.dot(r_ref[...], e_ref[...], preferred_element_type=F32,
                         precision=lax.Precision.HIGHEST) + m_ref[...]


def _rel_bias_blocks(rpb):
    depth, heads, n_dr, n_dc = rpb.shape
    kc = np.arange(GRID_W)[:, None]
    cq = np.arange(GRID_W)[None, :]
    dc = (kc - cq + NA_COLS - 1).reshape(-1)
    n_dc_pad = 32
    onehot = (np.arange(n_dc_pad)[:, None] == dc[None, :]).astype(np.float32)
    c0 = np.clip(cq - NA_COLS // 2, 0, GRID_W - NA_COLS)
    outside = ~((kc >= c0) & (kc < c0 + NA_COLS))
    col_mask = np.where(outside, NEG_INF, 0.0).astype(np.float32).reshape(1, -1)
    table = jnp.pad((rpb.astype(F32) * LOG2_E).reshape(depth * heads * n_dr, n_dc), ((0, 0), (0, n_dc_pad - n_dc)))
    blocks = pl.pallas_call(
        _rel_bias_kernel,
        out_shape=jax.ShapeDtypeStruct((depth * heads * n_dr, GRID_W * GRID_W), F32),
    )(table, jnp.asarray(onehot), jnp.asarray(col_mask))
    blocks = blocks.reshape(depth, heads, n_dr, GRID_W, GRID_W)
    return jnp.concatenate([blocks, jnp.full((depth, heads, 1, GRID_W, GRID_W), NEG_INF, F32)], axis=2)


def _nb_bias_tiles(rpb, n_lat, s_tot):
    rows, kh, w_rows, reps, starts, variants = _nb_plan(n_lat, s_tot)
    q_rows = TQ_B // GRID_W
    masked_block = 2 * NA_ROWS - 1
    dr_map = np.full((len(reps) + 1, w_rows, q_rows), masked_block, np.int32)
    for v, (i, w0) in enumerate(reps):
        for a in range(w_rows):
            for b in range(q_rows):
                r = q_rows * i + b
                r0 = int(np.clip(r - kh // 2, 0, rows - kh))
                if r0 <= w0 + a < r0 + kh:
                    dr_map[v, a, b] = w0 + a - r + NA_ROWS - 1
    blocks = _rel_bias_blocks(rpb)[:, :, dr_map]
    depth = rpb.shape[0]
    tiles = jnp.transpose(blocks, (0, 2, 3, 5, 1, 4, 6)).reshape(
        depth, len(reps) + 1, w_rows * GRID_W, B_HEADS * TQ_B)
    return tiles, jnp.asarray(starts), jnp.asarray(variants), w_rows * GRID_W


def _attn_b_kernel(start_ref, var_ref, fast_ref, ksc_ref, q_ref, qn_ref, k_ref, v_ref, bias_ref, o_ref,
                   qp_ref, ot_ref, m_ref, *, n_lat, n_ctx, n_keys):
    tq = q_ref.shape[1]
    ws = pl.multiple_of(start_ref[pl.program_id(1)], 128)
    qp_ref[...] = jnp.zeros_like(qp_ref)
    for h in range(B_HEADS):
        qp_ref[h * HEAD_DIM:(h + 1) * HEAD_DIM, h * tq:(h + 1) * tq] = q_ref[h * HEAD_DIM:(h + 1) * HEAD_DIM, :]
    qp = qp_ref[...]
    s_loc = jnp.dot(k_ref[pl.ds(ws, n_keys), :], qp, preferred_element_type=F32) + bias_ref[...]
    s_ctx = jnp.dot(k_ref[n_lat:n_lat + n_ctx, :], qp, preferred_element_type=F32)
    for bounded in (True, False):
        @pl.when((fast_ref[0] == 1) == bounded)
        def _():
            if bounded:
                m_ref[...] = jnp.concatenate(
                    [jnp.sqrt(qn_ref[QN_B + h:QN_B + h + 1, :]) * ksc_ref[pl.program_id(0) * B_HEADS + h]
                     for h in range(B_HEADS)], axis=1)
            else:
                m_ref[...] = jnp.maximum(jnp.max(s_loc, axis=0, keepdims=True), jnp.max(s_ctx, axis=0, keepdims=True))
    m = m_ref[...]
    p_loc = jnp.exp2(s_loc - m)
    p_ctx = jnp.exp2(s_ctx - m)
    denom = jnp.sum(p_loc, axis=0, keepdims=True) + jnp.sum(p_ctx, axis=0, keepdims=True)
    acc = (jnp.dot(v_ref[:, pl.ds(ws, n_keys)], p_loc.astype(BF16), preferred_element_type=F32)
           + jnp.dot(v_ref[:, n_lat:n_lat + n_ctx], p_ctx.astype(BF16), preferred_element_type=F32))
    for h in range(B_HEADS):
        ot_ref[h * HEAD_DIM:(h + 1) * HEAD_DIM, :] = (
            acc[h * HEAD_DIM:(h + 1) * HEAD_DIM, h * tq:(h + 1) * tq] / denom[:, h * tq:(h + 1) * tq])
    o_ref[...] = ot_ref[...].T.astype(BF16)


def _attn_b(q_t, q_norm2, k, k_norm2, v_t, bias_tiles, max_bias, starts, variants, n_keys, n_lat):
    bsz, s_tot, _ = k.shape
    tq = TQ_B
    k_scale = jnp.sqrt(jnp.max(k_norm2[:, KN_B:KN_B + B_HEADS, :], axis=-1))
    bound = jnp.sqrt(jnp.max(q_norm2[:, QN_B:QN_B + B_HEADS, :])) * jnp.max(k_scale) + max_bias
    fast = (bound <= MAX_SCORE_BOUND).astype(jnp.int32).reshape(1)
    kern = functools.partial(_attn_b_kernel, n_lat=n_lat, n_ctx=s_tot - n_lat, n_keys=n_keys)
    smem = pl.BlockSpec(memory_space=pltpu.SMEM)
    grid_spec = pltpu.PrefetchScalarGridSpec(
        num_scalar_prefetch=2,
        grid=(bsz, s_tot // tq),
        in_specs=[smem, smem,
                  pl.BlockSpec((None, QB_W, tq), lambda b, i, st, va: (b, 0, i)),
                  pl.BlockSpec((None, QN_ROWS, tq), lambda b, i, st, va: (b, 0, i)),
                  pl.BlockSpec((None, s_tot, KB_W), lambda b, i, st, va: (b, 0, 0)),
                  pl.BlockSpec((None, KB_W, s_tot), lambda b, i, st, va: (b, 0, 0)),
                  pl.BlockSpec((None, n_keys, B_HEADS * tq), lambda b, i, st, va: (va[i], 0, 0))],
        out_specs=pl.BlockSpec((None, tq, QB_W), lambda b, i, st, va: (b, i, 0)),
        scratch_shapes=[pltpu.VMEM((QB_W, B_HEADS * tq), BF16), pltpu.VMEM((QB_W, tq), F32),
                        pltpu.VMEM((1, B_HEADS * tq), F32)],
    )
    return pl.pallas_call(
        kern,
        grid_spec=grid_spec,
        out_shape=jax.ShapeDtypeStruct((bsz, s_tot, QB_W), BF16),
        compiler_params=_params(("parallel", "arbitrary")),
    )(starts, variants, fast, k_scale.reshape(-1), q_t, q_norm2, k, v_t, bias_tiles)


def _attn_c_kernel(fast_ref, ksc_ref, q_ref, qn_ref, k_ref, v_ref, o_ref, qp_ref, m_ref, acc_ref, ot_ref,
                   *, n_lat, n_ctx, tk):
    tq = q_ref.shape[1]
    is_ctx = pl.program_id(1) >= n_lat // tq
    n_chunks = jnp.where(is_ctx, 0, n_lat // tk - 1)
    fast = fast_ref[0] == 1
    first_row = lax.broadcasted_iota(jnp.int32, (KC_W, tq), 0) == 0
    for g in range(C_KV):
        k_scale = ksc_ref[pl.program_id(0) * C_KV + g]
        for hh in range(GQA_GROUP):
            h = g * GQA_GROUP + hh
            qp_ref[g, 0:KC_W, hh * tq:(hh + 1) * tq] = _padded_q(q_ref, h, g)
            offset = jnp.where(fast, -jnp.sqrt(qn_ref[h:h + 1, :]) * k_scale, 0.0)
            qp_ref[g, KC_W:2 * KC_W, hh * tq:(hh + 1) * tq] = jnp.where(first_row, offset, 0.0).astype(BF16)
    acc_ref[...] = jnp.zeros_like(acc_ref)

    def bounded_step(g, start, size):
        s = jnp.dot(k_ref[pl.ds(start, size), :], qp_ref[g], preferred_element_type=F32)
        p = jnp.exp2(s).astype(BF16)
        acc_ref[g] += jnp.dot(v_ref[g, :, pl.ds(start, size)], p, preferred_element_type=F32)

    def online_step(g, start, size):
        s = jnp.dot(k_ref[pl.ds(start, size), :], qp_ref[g], preferred_element_type=F32)
        m_prev = m_ref[g]
        m_new = jnp.maximum(m_prev, jnp.max(s, axis=0, keepdims=True))
        p = jnp.exp2(s - m_new).astype(BF16)
        pv = jnp.dot(v_ref[g, :, pl.ds(start, size)], p, preferred_element_type=F32)
        acc_ref[g] = acc_ref[g] * jnp.exp2(m_prev - m_new) + pv
        m_ref[g] = m_new

    def sweep(step):
        def body(c, carry):
            for g in range(C_KV):
                step(g, pl.multiple_of(c * tk, tk), tk)
            return carry

        lax.fori_loop(0, n_chunks, body, 0)

        @pl.when(is_ctx)
        def _():
            for g in range(C_KV):
                step(g, n_lat, n_ctx)

        @pl.when(jnp.logical_not(is_ctx))
        def _():
            for g in range(C_KV):
                step(g, n_lat - tk, tk + n_ctx)

    @pl.when(fast)
    def _():
        sweep(bounded_step)

    @pl.when(jnp.logical_not(fast))
    def _():
        m_ref[...] = jnp.full_like(m_ref, NEG_INF)
        sweep(online_step)

    for g in range(C_KV):
        acc = acc_ref[g]
        out = acc[0:HEAD_DIM, :] / acc[HEAD_DIM:HEAD_DIM + 1, :]
        for hh in range(GQA_GROUP):
            h = g * GQA_GROUP + hh
            ot_ref[h * HEAD_DIM:(h + 1) * HEAD_DIM, :] = out[:, hh * tq:(hh + 1) * tq]
    o_ref[...] = ot_ref[...].T.astype(BF16)


def _attn_c(q_t, q_norm2, k, k_norm2, v_t, n_lat):
    bsz, s_tot, _ = k.shape
    tq = TQ_C
    tk = min(TK_C, n_lat)
    k_scale = jnp.sqrt(jnp.max(k_norm2[:, KN_C:KN_C + C_KV, :], axis=-1)) * BOUND_SLACK
    bound = jnp.sqrt(jnp.max(q_norm2[:, QN_C:QN_C + C_HEADS, :])) * jnp.max(k_scale)
    fast = (bound <= MAX_SCORE_BOUND).astype(jnp.int32).reshape(1)
    kern = functools.partial(_attn_c_kernel, n_lat=n_lat, n_ctx=s_tot - n_lat, tk=tk)
    smem = pl.BlockSpec(memory_space=pltpu.SMEM)
    return pl.pallas_call(
        kern,
        grid=(bsz, s_tot // tq),
        in_specs=[smem, smem,
                  pl.BlockSpec((None, QC_W, tq), lambda b, i: (b, 0, i)),
                  pl.BlockSpec((None, QN_ROWS, tq), lambda b, i: (b, 0, i)),
                  pl.BlockSpec((None, s_tot, 2 * KC_W), lambda b, i: (b, 0, 0)),
                  pl.BlockSpec((None, C_KV, V_ROWS, s_tot), lambda b, i: (b, 0, 0, 0))],
        out_specs=pl.BlockSpec((None, tq, QC_W), lambda b, i: (b, i, 0)),
        out_shape=jax.ShapeDtypeStruct((bsz, s_tot, QC_W), BF16),
        scratch_shapes=[pltpu.VMEM((C_KV, 2 * KC_W, GQA_GROUP * tq), BF16),
                        pltpu.VMEM((C_KV, 1, GQA_GROUP * tq), F32),
                        pltpu.VMEM((C_KV, V_ROWS, GQA_GROUP * tq), F32),
                        pltpu.VMEM((QC_W, tq), F32)],
        compiler_params=_params(("parallel", "arbitrary")),
    )(fast, k_scale.reshape(-1), q_t, q_norm2, k, v_t)


def _store_token_tiles(ref, val, pitch=TILE_ROWS, first=0):
    n = val.shape[0]
    for c in range(TILE_ROWS):
        ref[pl.ds(first + c, n, stride=pitch), :] = val[:, c * 128:(c + 1) * 128]


def _load_token_tiles(ref, n, pitch=TILE_ROWS, first=0):
    return jnp.concatenate([ref[pl.ds(first + c, n, stride=pitch), :] for c in range(TILE_ROWS)], axis=1)


def _sc_gather(table, idx):
    _, r, lanes = table.shape
    b = idx.shape[0]
    n_workers = SC_CORES * SC_SUBCORES
    per_worker = b // n_workers
    assert b % n_workers == 0 and per_worker % SC_ALIGN == 0
    units = per_worker // SC_ALIGN
    max_units = SC_TILE_BYTES // (SC_ALIGN * r * lanes * 4)
    chunk = SC_ALIGN * max(u for u in range(1, max_units + 1) if units % u == 0)
    mesh = plsc.VectorSubcoreMesh(core_axis_name="c", subcore_axis_name="s")

    @functools.partial(
        pl.kernel, mesh=mesh,
        out_type=jax.ShapeDtypeStruct((b, r, lanes), F32),
        scratch_types=[pltpu.VMEM((chunk,), jnp.int32), pltpu.VMEM((chunk, r, lanes), F32),
                       pltpu.SemaphoreType.DMA],
    )
    def gather(table_hbm, idx_hbm, out_hbm, idx_v, rows_v, sem):
        worker = lax.axis_index("s") * SC_CORES + lax.axis_index("c")

        @pl.loop(0, per_worker // chunk)
        def _(j):
            base = pl.multiple_of(worker * per_worker + j * chunk, SC_ALIGN)
            pltpu.sync_copy(idx_hbm.at[pl.ds(base, chunk)], idx_v)
            pltpu.async_copy(table_hbm.at[idx_v], rows_v, sem).wait()
            pltpu.sync_copy(rows_v, out_hbm.at[pl.ds(base, chunk)])

    return gather(table, idx)


def _layer_norm(h, g, b):
    mu = jnp.mean(h, axis=-1, keepdims=True)
    hc = h - mu
    var = jnp.mean(hc * hc, axis=-1, keepdims=True)
    return hc * lax.rsqrt(var + LN_EPS) * g + b


def _post_kernel(oa_ref, ob_ref, oc_ref, x_ref, mod_ref, w_ref, g_ref, b_ref, wr_ref, rb_ref,
                 x1_ref, u2_ref, eid_ref, gate_ref, o_scr, ub_scr, *, alpha):
    tm = x_ref.shape[0]
    o_cat = jnp.concatenate([oa_ref[...], ob_ref[...], oc_ref[...]], axis=1)
    o_scr[...] = jnp.dot(o_cat, w_ref[...], preferred_element_type=F32)
    gate1 = 1.0 + mod_ref[2:3, :]
    scale2 = 1.0 + mod_ref[4:5, :]
    shift2 = mod_ref[3:4, :]
    ln_g = g_ref[...]
    ln_b = b_ref[...]

    def row_group(r, carry):
        r0 = pl.multiple_of(r * LN_ROWS, LN_ROWS)
        rows = pl.ds(r0, LN_ROWS)
        x1 = _layer_norm(alpha * x_ref[rows, :] + gate1 * o_scr[rows, :], ln_g, ln_b)
        x1_ref[rows, :] = x1
        u2 = x1 * scale2 + shift2
        ub_scr[rows, :] = u2.astype(BF16)
        for c in range(TILE_ROWS):
            u2_ref[pl.ds(r0 * TILE_ROWS + c, LN_ROWS, stride=TILE_ROWS), :] = u2[:, c * 128:(c + 1) * 128]
        return carry

    lax.fori_loop(0, tm // LN_ROWS, row_group, 0, unroll=LN_UNROLL)
    logits = lax.dot_general(wr_ref[...], ub_scr[...], (((1,), (1,)), ((), ())), preferred_element_type=F32)
    aff_all = jax.nn.sigmoid(logits)
    sel_all = aff_all + rb_ref[...]
    aff = [aff_all[e:e + 1, :] for e in range(N_EXPERTS)]
    sel = [sel_all[e:e + 1, :] for e in range(N_EXPERTS)]
    gsum = []
    for g in range(N_EXPERT_GROUPS):
        a, b, c, d = sel[g * PER_GROUP:(g + 1) * PER_GROUP]
        hi1, lo1, hi2, lo2 = jnp.maximum(a, b), jnp.minimum(a, b), jnp.maximum(c, d), jnp.minimum(c, d)
        gsum.append(jnp.maximum(hi1, hi2) + jnp.maximum(jnp.minimum(hi1, hi2), jnp.maximum(lo1, lo2)))
    best_g = jnp.zeros_like(gsum[0], dtype=jnp.int32)
    best_v = gsum[0]
    for g in range(1, N_EXPERT_GROUPS):
        better = gsum[g] > best_v
        best_g = jnp.where(better, g, best_g)
        best_v = jnp.where(better, gsum[g], best_v)
    s_loc, a_loc = [], []
    for j in range(PER_GROUP):
        sj, aj = sel[j], aff[j]
        for g in range(1, N_EXPERT_GROUPS):
            pick = best_g == g
            sj = jnp.where(pick, sel[g * PER_GROUP + j], sj)
            aj = jnp.where(pick, aff[g * PER_GROUP + j], aj)
        s_loc.append(sj)
        a_loc.append(aj)
    i1 = jnp.zeros_like(best_g)
    v1, g1 = s_loc[0], a_loc[0]
    for j in range(1, PER_GROUP):
        better = s_loc[j] > v1
        i1 = jnp.where(better, j, i1)
        v1 = jnp.where(better, s_loc[j], v1)
        g1 = jnp.where(better, a_loc[j], g1)
    i2 = jnp.full_like(best_g, -1)
    v2 = jnp.full_like(v1, -jnp.inf)
    g2 = jnp.zeros_like(g1)
    for j in range(PER_GROUP):
        better = (i1 != j) & ((s_loc[j] > v2) | (i2 < 0))
        i2 = jnp.where(better, j, i2)
        v2 = jnp.where(better, s_loc[j], v2)
        g2 = jnp.where(better, a_loc[j], g2)
    total = g1 + g2
    lo, hi = jnp.minimum(i1, i2), jnp.maximum(i1, i2)
    pair = jnp.where(lo == 0, hi - 1, jnp.where(lo == 1, hi + 1, PAIRS_PER_GROUP - 1))
    first_is_lo = i1 < i2
    g_lo = jnp.where(first_is_lo, g1, g2) / total
    g_hi = jnp.where(first_is_lo, g2, g1) / total
    eid_ref[...] = jnp.concatenate([best_g * PAIRS_PER_GROUP + pair,
                                    jnp.zeros((MOD_ROWS - 1, tm), jnp.int32)], axis=0)
    gate_ref[...] = jnp.concatenate([g_lo, g_hi, jnp.zeros((MOD_ROWS - TOP_K, tm), F32)], axis=0)


def _post(oa, ob, oc, xa, mod, w_out, ln_g, ln_b, wr_t, rb, n_lat, alpha):
    bsz, s_tot, d = xa.shape
    tm = TOK_TILE
    n_lat_tiles = n_lat // tm
    n_tiles = s_tot // tm
    sd = jax.ShapeDtypeStruct

    def tmajor(width):
        return pl.BlockSpec((None, tm, width), lambda b, i: (b, i, 0))

    def const(shape):
        return pl.BlockSpec(shape, lambda b, i: (0,) * len(shape))

    lanes = pl.BlockSpec((MOD_ROWS, tm), lambda b, i: (0, b * n_tiles + i))
    return pl.pallas_call(
        functools.partial(_post_kernel, alpha=alpha),
        grid=(bsz, n_tiles),
        in_specs=[tmajor(QA_W), tmajor(QB_W), tmajor(QC_W), tmajor(d),
                  pl.BlockSpec((None, N_MOD, d), lambda b, i: (jnp.where(i >= n_lat_tiles, bsz, b), 0, 0)),
                  const((d, d)), const((1, d)), const((1, d)), const((N_EXPERTS, d)), const((N_EXPERTS, 1))],
        out_specs=[tmajor(d), pl.BlockSpec((tm * TILE_ROWS, 128), lambda b, i: (b * n_tiles + i, 0)), lanes, lanes],
        out_shape=[sd((bsz, s_tot, d), F32), sd((bsz * s_tot * TILE_ROWS, 128), F32),
                   sd((MOD_ROWS, bsz * s_tot), jnp.int32), sd((MOD_ROWS, bsz * s_tot), F32)],
        scratch_shapes=[pltpu.VMEM((tm, d), F32), pltpu.VMEM((tm, d), BF16)],
        compiler_params=_params(("parallel", "parallel")),
    )(oa, ob, oc, xa, mod, w_out, ln_g, ln_b, wr_t, rb)


def _ffn_kernel(elo_ref, ehi_ref, nused_ref, x_ref,
                w1lo_ref, w3lo_ref, w2lo_ref, w1hi_ref, w3hi_ref, w2hi_ref, y_ref,
                w1lo, w3lo, w2lo, w1hi, w3hi, w2hi):
    blk = y_ref.shape[0] // (TOP_K * TILE_ROWS)
    step = pl.program_id(0)
    n_used = nused_ref[0]
    prev = jnp.maximum(step - 1, 0)

    @pl.when(step < n_used)
    def _():
        @pl.when((step == 0) | (elo_ref[step] != elo_ref[prev]))
        def _():
            w1lo[...] = w1lo_ref[...].astype(BF16)
            w3lo[...] = w3lo_ref[...].astype(BF16)
            w2lo[...] = w2lo_ref[...].astype(BF16)

        @pl.when((step == 0) | (ehi_ref[step] != ehi_ref[prev]))
        def _():
            w1hi[...] = w1hi_ref[...].astype(BF16)
            w3hi[...] = w3hi_ref[...].astype(BF16)
            w2hi[...] = w2hi_ref[...].astype(BF16)

        xb = _load_token_tiles(x_ref, blk).astype(BF16)
        for first, (w1b, w3b, w2b) in ((0, (w1lo, w3lo, w2lo)), (TILE_ROWS, (w1hi, w3hi, w2hi))):
            h1 = jnp.dot(xb, w1b[...], preferred_element_type=F32)
            h3 = jnp.dot(xb, w3b[...], preferred_element_type=F32)
            hid = (h1 * jax.nn.sigmoid(h1) * h3).astype(BF16)
            _store_token_tiles(y_ref, jnp.dot(hid, w2b[...], preferred_element_type=F32),
                               pitch=TOP_K * TILE_ROWS, first=first)

    @pl.when(step >= n_used)
    def _():
        y_ref[...] = jnp.zeros_like(y_ref)


def _expert_ffn(block_lo, block_hi, n_used, x_tiles, w1, w3, w2, layer):
    d = w1.shape[2]
    n_blocks = block_lo.shape[0]
    blk = CLASS_BLOCK

    def weight(shape, which):
        return pl.BlockSpec((None, None) + shape, lambda i, lo, hi, nu: (layer, (lo, hi)[which][i], 0, 0))

    up, down = (d, D_EXPERT), (D_EXPERT, d)
    grid_spec = pltpu.PrefetchScalarGridSpec(
        num_scalar_prefetch=3,
        grid=(n_blocks,),
        in_specs=[pl.BlockSpec((blk * TILE_ROWS, 128), lambda i, lo, hi, nu: (i, 0)),
                  weight(up, 0), weight(up, 0), weight(down, 0), weight(up, 1), weight(up, 1), weight(down, 1)],
        out_specs=pl.BlockSpec((blk * TOP_K * TILE_ROWS, 128), lambda i, lo, hi, nu: (i, 0)),
        scratch_shapes=[pltpu.VMEM(s, BF16) for s in (up, up, down, up, up, down)],
    )
    return pl.pallas_call(
        _ffn_kernel,
        grid_spec=grid_spec,
        out_shape=jax.ShapeDtypeStruct((n_blocks * blk * TOP_K * TILE_ROWS, 128), F32),
        compiler_params=_params(("arbitrary",)),
    )(block_lo, block_hi, n_used, x_tiles, w1, w3, w2, w1, w3, w2)


def _comb_kernel(y_ref, x1_ref, gate_ref, mod_ref, g_ref, b_ref, o_ref, *, alpha):
    tm = x1_ref.shape[0]
    pitch = TOP_K * TILE_ROWS
    y_lo = _load_token_tiles(y_ref, tm, pitch, 0)
    y_hi = _load_token_tiles(y_ref, tm, pitch, TILE_ROWS)
    f = gate_ref[:, 0:1] * y_lo + gate_ref[:, 1:2] * y_hi
    o_ref[...] = _layer_norm(alpha * x1_ref[...] + (1.0 + mod_ref[5:6, :]) * f, g_ref[...], b_ref[...])


def _combine(y_tiles, x1, gates, mod, ln_g, ln_b, n_lat, alpha):
    bsz, s_tot, d = x1.shape
    tm = TOK_TILE
    n_lat_tiles = n_lat // tm
    n_tiles = s_tot // tm
    return pl.pallas_call(
        functools.partial(_comb_kernel, alpha=alpha),
        grid=(bsz, n_tiles),
        in_specs=[pl.BlockSpec((tm * TOP_K * TILE_ROWS, 128), lambda b, i: (b * n_tiles + i, 0)),
                  pl.BlockSpec((None, tm, d), lambda b, i: (b, i, 0)),
                  pl.BlockSpec((tm, TOP_K), lambda b, i: (b * n_tiles + i, 0)),
                  pl.BlockSpec((None, N_MOD, d), lambda b, i: (jnp.where(i >= n_lat_tiles, bsz, b), 0, 0)),
                  pl.BlockSpec((1, d), lambda b, i: (0, 0)),
                  pl.BlockSpec((1, d), lambda b, i: (0, 0))],
        out_specs=pl.BlockSpec((None, tm, d), lambda b, i: (b, i, 0)),
        out_shape=jax.ShapeDtypeStruct((bsz, s_tot, d), F32),
        compiler_params=_params(("parallel", "parallel")),
    )(y_tiles, x1, gates, mod, ln_g, ln_b)


def _deinterleave(n_heads):
    one = np.concatenate([np.arange(0, HEAD_DIM, 2), np.arange(1, HEAD_DIM, 2)])
    return np.concatenate([h * HEAD_DIM + one for h in range(n_heads)])


def _proj_weight(w_in_l):
    sizes = (QA_W, KA_W, KA_W, QB_W, KB_W, KB_W, QC_W, KC_W, KC_W)
    offs = np.concatenate([[0], np.cumsum(sizes)])
    qa, ka, va, qb, kb, vb, qc, kc, vc = [w_in_l[:, offs[i]:offs[i + 1]] for i in range(len(sizes))]
    scale = HEAD_DIM ** -0.5 * LOG2_E
    qa = qa[:, _deinterleave(A_HEADS)] * scale
    ka = ka[:, _deinterleave(A_KV)]
    qb = qb * scale
    qc = qc[:, _deinterleave(C_HEADS)]
    kc = kc[:, _deinterleave(C_KV)]
    return jnp.concatenate([qa, qb, qc, ka, kb, kc, va, vb, vc], axis=1).T.astype(BF16)


def _rope_tables(n_lat, n_ctx):
    t = np.arange(n_lat)
    inv_freq = ROPE_BASE ** (-np.arange(0, HALF, 2, dtype=np.float32) / HALF)
    ang = jnp.concatenate([jnp.asarray(t // GRID_W, F32)[None, :] * jnp.asarray(inv_freq)[:, None],
                           jnp.asarray(t % GRID_W, F32)[None, :] * jnp.asarray(inv_freq)[:, None]], axis=0)
    cos = jnp.concatenate([jnp.cos(ang), jnp.ones((HALF, n_ctx), F32)], axis=1)
    sin = jnp.concatenate([jnp.sin(ang), jnp.zeros((HALF, n_ctx), F32)], axis=1)
    return cos, sin


def _dispatch(cls, n_tok):
    onehot = (cls[:, None] == jnp.arange(N_CLASSES, dtype=jnp.int32)[None, :]).astype(jnp.int32)
    csum = jnp.cumsum(onehot, axis=0)
    counts = csum[-1]
    padded = (counts + CLASS_BLOCK - 1) // CLASS_BLOCK * CLASS_BLOCK
    pad_ends = jnp.cumsum(padded)
    dest = jnp.sum(onehot * (csum - 1 + (pad_ends - padded)[None, :]), axis=1)
    n_blocks = -(-n_tok // CLASS_BLOCK) + N_CLASSES
    slot_tok = jnp.zeros((n_blocks * CLASS_BLOCK,), jnp.int32).at[dest].set(jnp.arange(n_tok, dtype=jnp.int32))
    block_start = jnp.arange(n_blocks, dtype=jnp.int32) * CLASS_BLOCK
    block_cls = jnp.minimum(jnp.sum((block_start[:, None] >= pad_ends[None, :]).astype(jnp.int32), axis=1),
                            N_CLASSES - 1)
    group, pair = block_cls // PAIRS_PER_GROUP, block_cls % PAIRS_PER_GROUP
    lo = (pair >= 3).astype(jnp.int32) + (pair >= 5).astype(jnp.int32)
    hi = jnp.where(pair < 3, pair + 1, jnp.where(pair < 5, pair - 1, 3))
    n_used = (pad_ends[-1] // CLASS_BLOCK).astype(jnp.int32).reshape(1)
    return slot_tok, group * PER_GROUP + lo, group * PER_GROUP + hi, dest.astype(jnp.int32), n_used


def kernel(x, c, ctx, c_ctx, w_in, w_out, sink, rpb, q_gain, k_gain, w_ada, b_ada,
           ln1_g, ln1_b, ln2_g, ln2_b, w_router, router_bias, w1, w3, w2):
    bsz, n_lat, d = x.shape
    n_ctx = ctx.shape[1]
    s_tot = n_lat + n_ctx
    depth = w_in.shape[0]
    assert d == D_MODEL and n_ctx == TOK_TILE and n_lat % min(TK_C, n_lat) == 0
    assert n_lat % (2 * TOK_TILE) == 0 and bsz + 1 <= MOD_ROWS
    alpha = float((2 * depth) ** 0.25)
    n_tok = bsz * s_tot
    tm = TOK_TILE

    cond = jnp.zeros((MOD_ROWS, d), F32).at[:bsz].set(c).at[bsz].set(c_ctx)
    mods = _ada_table(cond, w_ada, b_ada).reshape(depth, MOD_ROWS, N_MOD, d)
    cos_t, sin_t = _rope_tables(n_lat, n_ctx)
    gain_perm = _deinterleave(1)
    wr_t = w_router.T.astype(BF16)
    rb = router_bias.reshape(N_EXPERTS, 1).astype(F32)

    bias_tiles, starts, variants, n_keys = _nb_bias_tiles(rpb, n_lat, s_tot)

    xa = jnp.concatenate([x, ctx], axis=1)
    for l in range(depth):
        mod = mods[l]
        qa, qb, qc, ka, kb, kc, va, vb, vc, qn, kn = _project(
            xa, mod, _proj_weight(w_in[l]), cos_t, sin_t,
            q_gain[l][gain_perm].reshape(HEAD_DIM, 1) * (HEAD_DIM ** -0.5 * LOG2_E),
            k_gain[l][gain_perm].reshape(HEAD_DIM, 1), n_lat)
        oa = _attn_a(sink[l], qa, qn, ka, kn, va, n_lat)
        ob = _attn_b(qb, qn, kb, kn, vb, bias_tiles[l], jnp.max(jnp.abs(rpb[l])) * LOG2_E,
                     starts, variants, n_keys, n_lat)
        oc = _attn_c(qc, qn, kc, kn, vc, n_lat)
        x1, u2, eid8, gate8 = _post(oa, ob, oc, xa, mod, w_out[l].astype(BF16),
                                    ln1_g[l].reshape(1, d), ln1_b[l].reshape(1, d), wr_t, rb, n_lat, alpha)
        slot_tok, block_lo, block_hi, dest, n_used = _dispatch(eid8[0], n_tok)
        x_sorted = _sc_gather(u2.reshape(n_tok, TILE_ROWS, 128), slot_tok)
        y = _expert_ffn(block_lo, block_hi, n_used, x_sorted.reshape(-1, 128), w1, w3, w2, l)
        y_tok = _sc_gather(y.reshape(-1, TOP_K * TILE_ROWS, 128), dest)
        xa = _combine(y_tok.reshape(-1, 128), x1, gate8[:TOP_K].T, mod,
                      ln2_g[l].reshape(1, d), ln2_b[l].reshape(1, d), n_lat, alpha)
    return xa[:, :n_lat]
```

```python
import functools

import numpy as np
import jax
import jax.numpy as jnp
from jax import lax
from jax.experimental import pallas as pl
from jax.experimental.pallas import tpu as pltpu

F32 = jnp.float32
BF16 = jnp.bfloat16

D_MODEL = 1024
HEAD_DIM = 64
HALF = HEAD_DIM // 2
GRID_W = 64
A_HEADS, A_KV = 6, 2
WINDOW = 128
B_HEADS = 4
NA_ROWS, NA_COLS = 8, 16
C_HEADS, C_KV = 6, 2
GQA_GROUP = 3
ROPE_BASE = 10000.0
N_EXPERTS = 16
N_EXPERT_GROUPS = 4
PER_GROUP = N_EXPERTS // N_EXPERT_GROUPS
TOP_K = 2
D_EXPERT = 512
PAIRS_PER_GROUP = PER_GROUP * (PER_GROUP - 1) // 2
N_CLASSES = N_EXPERT_GROUPS * PAIRS_PER_GROUP
CLASS_BLOCK = 256
N_MOD = 6
LN_EPS = 1e-6
QK_EPS = 1e-6
NEG_INF = -1e30

QA_W, QB_W, QC_W = A_HEADS * HEAD_DIM, B_HEADS * HEAD_DIM, C_HEADS * HEAD_DIM
KA_W, KB_W, KC_W = A_KV * HEAD_DIM, B_HEADS * HEAD_DIM, C_KV * HEAD_DIM
R_QA = 0
R_QB = R_QA + QA_W
R_QC = R_QB + QB_W
R_KA = R_QC + QC_W
R_KB = R_KA + KA_W
R_KC = R_KB + KB_W
R_VA = R_KC + KC_W
R_VB = R_VA + KA_W
R_VC = R_VB + KB_W
PROJ_WIDTH = R_VC + KC_W

V7X_VMEM_LIMIT = 56 * 1024 * 1024
TOK_TILE = 256
TQ_A = 256
TQ_B = 2 * GRID_W
TQ_C = 256
TK_C = 2048
MOD_ROWS = 8
LOG2_E = 1.4426950408889634
QN_C, QN_A, QN_B, QN_ROWS = 0, C_HEADS, C_HEADS + A_HEADS, C_HEADS + A_HEADS + B_HEADS
KN_C, KN_A, KN_B, KN_ROWS = 0, C_KV, C_KV + A_KV, C_KV + A_KV + B_HEADS
MAX_SCORE_BOUND = 60.0
BOUND_SLACK = 1.004
TILE_ROWS = 8
BF16_ROWS = 16
GATHER_UNROLL = 8
LN_ROWS = 64
LN_UNROLL = 4
V_ROWS = HEAD_DIM + BF16_ROWS


def _params(sem):
    return pltpu.CompilerParams(dimension_semantics=sem, vmem_limit_bytes=V7X_VMEM_LIMIT)


def _ada_kernel(c_ref, w_ref, b_ref, o_ref):
    c = c_ref[...]
    act = c * jax.nn.sigmoid(c)
    o_ref[...] = jnp.dot(act, w_ref[...], preferred_element_type=F32,
                         precision=lax.Precision.HIGHEST) + b_ref[...]


def _ada_table(cond, w_ada, b_ada):
    depth, d, n = w_ada.shape
    tn = n // 4
    return pl.pallas_call(
        _ada_kernel,
        grid=(depth, n // tn),
        in_specs=[pl.BlockSpec((MOD_ROWS, d), lambda l, j: (0, 0)),
                  pl.BlockSpec((None, d, tn), lambda l, j: (l, 0, j)),
                  pl.BlockSpec((None, 1, tn), lambda l, j: (l, 0, j))],
        out_specs=pl.BlockSpec((None, MOD_ROWS, tn), lambda l, j: (l, 0, j)),
        out_shape=jax.ShapeDtypeStruct((depth, MOD_ROWS, n), F32),
        compiler_params=_params(("parallel", "parallel")),
    )(cond, w_ada, b_ada.reshape(depth, 1, n))


def _sq_norm(parts):
    return sum(jnp.sum(jnp.square(p.astype(F32)), axis=0, keepdims=True) for p in parts)


def _proj_kernel(x_ref, mod_ref, w_ref, cos_ref, sin_ref, qg_ref, kg_ref,
                 qa_ref, qb_ref, qc_ref, ka_ref, kb_ref, kc_ref, va_ref, vb_ref, vc_ref, qn_ref, kn_ref):
    tm = x_ref.shape[0]
    u = (x_ref[...] * (1.0 + mod_ref[1:2, :]) + mod_ref[0:1, :]).astype(BF16)

    def proj(lo, width):
        return lax.dot_general(w_ref[lo:lo + width, :], u, (((1,), (1,)), ((), ())),
                               preferred_element_type=F32)

    cos = cos_ref[...]
    sin = sin_ref[...]

    def rope(blk):
        x0, x1 = blk[:HALF], blk[HALF:]
        return x0 * cos - x1 * sin, x0 * sin + x1 * cos

    def qk_norm(blk, gain):
        ms = jnp.mean(blk * blk, axis=0, keepdims=True)
        return blk * lax.rsqrt(ms + QK_EPS) * gain

    p = proj(R_QA, QA_W)
    for h in range(A_HEADS):
        o0, o1 = [o.astype(BF16) for o in rope(p[h * HEAD_DIM:(h + 1) * HEAD_DIM])]
        qa_ref[h * HEAD_DIM:h * HEAD_DIM + HALF, :] = o0
        qa_ref[h * HEAD_DIM + HALF:(h + 1) * HEAD_DIM, :] = o1
        qn_ref[QN_A + h:QN_A + h + 1, :] = _sq_norm([o0, o1])
    p = proj(R_QB, QB_W).astype(BF16)
    qb_ref[...] = p
    for h in range(B_HEADS):
        qn_ref[QN_B + h:QN_B + h + 1, :] = _sq_norm([p[h * HEAD_DIM:(h + 1) * HEAD_DIM]])
    p = proj(R_QC, QC_W)
    qg = qg_ref[...]
    for h in range(C_HEADS):
        o0, o1 = [o.astype(BF16) for o in rope(qk_norm(p[h * HEAD_DIM:(h + 1) * HEAD_DIM], qg))]
        qc_ref[h * HEAD_DIM:h * HEAD_DIM + HALF, :] = o0
        qc_ref[h * HEAD_DIM + HALF:(h + 1) * HEAD_DIM, :] = o1
        qn_ref[QN_C + h:QN_C + h + 1, :] = _sq_norm([o0, o1])
    one_lane = (lax.broadcasted_iota(jnp.int32, (tm, KA_W), 1) == 0).astype(BF16)
    p = proj(R_KA, KA_W)
    parts = []
    for h in range(A_KV):
        o0, o1 = [o.astype(BF16) for o in rope(p[h * HEAD_DIM:(h + 1) * HEAD_DIM])]
        kn_ref[KN_A + h:KN_A + h + 1, :] = _sq_norm([o0, o1])
        parts.extend([o0.astype(F32), o1.astype(F32)])
    ka_ref[:, 0:KA_W] = jnp.concatenate(parts, axis=0).T.astype(BF16)
    ka_ref[:, KA_W:2 * KA_W] = one_lane
    p = proj(R_KB, KB_W).astype(BF16)
    for h in range(B_HEADS):
        kn_ref[KN_B + h:KN_B + h + 1, :] = _sq_norm([p[h * HEAD_DIM:(h + 1) * HEAD_DIM]])
    kb_ref[...] = p.astype(F32).T.astype(BF16)
    p = proj(R_KC, KC_W)
    kg = kg_ref[...]
    parts = []
    for h in range(C_KV):
        o0, o1 = [o.astype(BF16) for o in rope(qk_norm(p[h * HEAD_DIM:(h + 1) * HEAD_DIM], kg))]
        kn_ref[KN_C + h:KN_C + h + 1, :] = _sq_norm([o0, o1])
        parts.extend([o0.astype(F32), o1.astype(F32)])
    kc_ref[:, 0:KC_W] = jnp.concatenate(parts, axis=0).T.astype(BF16)
    kc_ref[:, KC_W:2 * KC_W] = one_lane
    ones_row = (lax.broadcasted_iota(jnp.int32, (BF16_ROWS, tm), 0) == 0).astype(BF16)
    for lo, ref in ((R_VA, va_ref), (R_VC, vc_ref)):
        p = proj(lo, KA_W)
        for g in range(A_KV):
            ref[g, 0:HEAD_DIM, :] = p[g * HEAD_DIM:(g + 1) * HEAD_DIM].astype(BF16)
            ref[g, HEAD_DIM:V_ROWS, :] = ones_row
    vb_ref[...] = proj(R_VB, KB_W).astype(BF16)


def _project(xa, mod, w_t, cos_t, sin_t, q_gain, k_gain, n_lat):
    bsz, s_tot, d = xa.shape
    tm = TOK_TILE
    n_lat_tiles = n_lat // tm
    sd = jax.ShapeDtypeStruct

    def dmajor(width):
        return pl.BlockSpec((None, width, tm), lambda b, i: (b, 0, i))

    def tmajor(width):
        return pl.BlockSpec((None, tm, width), lambda b, i: (b, i, 0))

    aug = pl.BlockSpec((None, A_KV, V_ROWS,tm), lambda b, i: (b, 0, 0, i))
    return pl.pallas_call(
        _proj_kernel,
        grid=(bsz, s_tot // tm),
        in_specs=[tmajor(d),
                  pl.BlockSpec((None, N_MOD, d), lambda b, i: (jnp.where(i >= n_lat_tiles, bsz, b), 0, 0)),
                  pl.BlockSpec((PROJ_WIDTH, d), lambda b, i: (0, 0)),
                  pl.BlockSpec((HALF, tm), lambda b, i: (0, i)),
                  pl.BlockSpec((HALF, tm), lambda b, i: (0, i)),
                  pl.BlockSpec((HEAD_DIM, 1), lambda b, i: (0, 0)),
                  pl.BlockSpec((HEAD_DIM, 1), lambda b, i: (0, 0))],
        out_specs=[dmajor(QA_W), dmajor(QB_W), dmajor(QC_W),
                   tmajor(2 * KA_W), tmajor(KB_W), tmajor(2 * KC_W),
                   aug, dmajor(KB_W), aug, dmajor(QN_ROWS), dmajor(KN_ROWS)],
        out_shape=[sd((bsz, QA_W, s_tot), BF16), sd((bsz, QB_W, s_tot), BF16), sd((bsz, QC_W, s_tot), BF16),
                   sd((bsz, s_tot, 2 * KA_W), BF16), sd((bsz, s_tot, KB_W), BF16), sd((bsz, s_tot, 2 * KC_W), BF16),
                   sd((bsz, A_KV, V_ROWS, s_tot), BF16), sd((bsz, KB_W, s_tot), BF16),
                   sd((bsz, A_KV, V_ROWS, s_tot), BF16),
                   sd((bsz, QN_ROWS, s_tot), F32), sd((bsz, KN_ROWS, s_tot), F32)],
        compiler_params=_params(("parallel", "parallel")),
    )(xa, mod, w_t, cos_t, sin_t, q_gain, k_gain)


def _padded_q(q_ref, h, g):
    blk = q_ref[h * HEAD_DIM:(h + 1) * HEAD_DIM, :]
    zeros = jnp.zeros_like(blk)
    return jnp.concatenate([blk, zeros] if g == 0 else [zeros, blk], axis=0)


def _attn_a_kernel(fast_ref, ksc_ref, sink_ref, q_ref, qn_ref, k_ref, v_ref, o_ref, ot_ref, *, n_lat, n_ctx):
    tq = q_ref.shape[1]
    span = tq + 2 * WINDOW
    start = pl.program_id(1) * tq
    ks = pl.multiple_of(jnp.clip(start - WINDOW, 0, n_lat - span), 128)
    kpos = ks + lax.broadcasted_iota(jnp.int32, (span, tq), 0)
    qpos = start + lax.broadcasted_iota(jnp.int32, (span, tq), 1)
    ok = (jnp.abs(qpos - kpos) <= WINDOW) & (qpos < n_lat)
    k_loc = k_ref[pl.ds(ks, span), :]
    k_ctx = k_ref[n_lat:n_lat + n_ctx, :]
    first_row = lax.broadcasted_iota(jnp.int32, (KA_W, tq), 0) == 0

    def head(h, g, bounded):
        v_loc = v_ref[g, :, pl.ds(ks, span)]
        v_ctx = v_ref[g, :, n_lat:n_lat + n_ctx]
        sink = sink_ref[h]
        if bounded:
            offset = jnp.sqrt(qn_ref[QN_A + h:QN_A + h + 1, :]) * ksc_ref[pl.program_id(0) * A_KV + g]
            qp = jnp.concatenate([_padded_q(q_ref, h, g), jnp.where(first_row, -offset, 0.0).astype(BF16)], axis=0)
        else:
            qp = jnp.concatenate([_padded_q(q_ref, h, g), jnp.zeros((KA_W, tq), BF16)], axis=0)
        s_loc = jnp.where(ok, jnp.dot(k_loc, qp, preferred_element_type=F32), NEG_INF)
        s_ctx = jnp.dot(k_ctx, qp, preferred_element_type=F32)
        if bounded:
            m = offset
        else:
            m = jnp.maximum(jnp.maximum(jnp.max(s_loc, axis=0, keepdims=True),
                                        jnp.max(s_ctx, axis=0, keepdims=True)), sink)
            s_loc, s_ctx = s_loc - m, s_ctx - m
        acc = (jnp.dot(v_loc, jnp.exp2(s_loc).astype(BF16), preferred_element_type=F32)
               + jnp.dot(v_ctx, jnp.exp2(s_ctx).astype(BF16), preferred_element_type=F32))
        denom = acc[HEAD_DIM:HEAD_DIM + 1, :] + jnp.exp2(sink - m)
        ot_ref[h * HEAD_DIM:(h + 1) * HEAD_DIM, :] = acc[0:HEAD_DIM, :] / denom

    for bounded in (True, False):
        @pl.when((fast_ref[0] == 1) == bounded)
        def _():
            for g in range(A_KV):
                for hh in range(GQA_GROUP):
                    head(g * GQA_GROUP + hh, g, bounded)

    o_ref[...] = ot_ref[...].T.astype(BF16)


def _attn_a(sink, q_t, q_norm2, k, k_norm2, v_t, n_lat):
    bsz, s_tot, _ = k.shape
    tq = TQ_A
    sink2 = sink.astype(F32) * LOG2_E
    k_scale = jnp.sqrt(jnp.max(k_norm2[:, KN_A:KN_A + A_KV, :], axis=-1)) * BOUND_SLACK
    bound = jnp.sqrt(jnp.max(q_norm2[:, QN_A:QN_A + A_HEADS, :])) * jnp.max(k_scale)
    fast = ((bound <= MAX_SCORE_BOUND) & (jnp.max(jnp.abs(sink2)) <= MAX_SCORE_BOUND)).astype(jnp.int32).reshape(1)
    kern = functools.partial(_attn_a_kernel, n_lat=n_lat, n_ctx=s_tot - n_lat)
    smem = pl.BlockSpec(memory_space=pltpu.SMEM)
    return pl.pallas_call(
        kern,
        grid=(bsz, s_tot // tq),
        in_specs=[smem, smem, smem,
                  pl.BlockSpec((None, QA_W, tq), lambda b, i: (b, 0, i)),
                  pl.BlockSpec((None, QN_ROWS, tq), lambda b, i: (b, 0, i)),
                  pl.BlockSpec((None, s_tot, 2 * KA_W), lambda b, i: (b, 0, 0)),
                  pl.BlockSpec((None, A_KV, V_ROWS, s_tot), lambda b, i: (b, 0, 0, 0))],
        out_specs=pl.BlockSpec((None, tq, QA_W), lambda b, i: (b, i, 0)),
        out_shape=jax.ShapeDtypeStruct((bsz, s_tot, QA_W), BF16),
        scratch_shapes=[pltpu.VMEM((QA_W, tq), F32)],
        compiler_params=_params(("parallel", "arbitrary")),
    )(fast, k_scale.reshape(-1), sink2, q_t, q_norm2, k, v_t)


def _nb_plan(n_lat, s_tot):
    rows = n_lat // GRID_W
    kh = min(NA_ROWS, rows)
    q_rows = TQ_B // GRID_W
    w_rows = kh + q_rows
    n_lat_blocks = n_lat // TQ_B
    n_blocks = s_tot // TQ_B
    starts, variants, keys, reps = [], [], {}, []
    for i in range(n_lat_blocks):
        w0 = int(np.clip(q_rows * i - kh // 2, 0, rows - w_rows))
        r0s = tuple(int(np.clip(q_rows * i + j - kh // 2, 0, rows - kh)) - w0 for j in range(q_rows))
        key = (q_rows * i - w0,) + r0s
        if key not in keys:
            keys[key] = len(reps)
            reps.append((i, w0))
        starts.append(w0 * GRID_W)
        variants.append(keys[key])
    masked = len(reps)
    starts.extend([0] * (n_blocks - n_lat_blocks))
    variants.extend([masked] * (n_blocks - n_lat_blocks))
    return rows, kh, w_rows, reps, np.asarray(starts, np.int32), np.asarray(variants, np.int32)


def _rel_bias_kernel(r_ref, e_ref, m_ref, o_ref):
    o_ref[...] = jnp.dot(r_ref[...], e_ref[...], preferred_element_type=F32,
                         precision=lax.Precision.HIGHEST) + m_ref[...]


def _rel_bias_blocks(rpb):
    depth, heads, n_dr, n_dc = rpb.shape
    kc = np.arange(GRID_W)[:, None]
    cq = np.arange(GRID_W)[None, :]
    dc = (kc - cq + NA_COLS - 1).reshape(-1)
    n_dc_pad = 32
    onehot = (np.arange(n_dc_pad)[:, None] == dc[None, :]).astype(np.float32)
    c0 = np.clip(cq - NA_COLS // 2, 0, GRID_W - NA_COLS)
    outside = ~((kc >= c0) & (kc < c0 + NA_COLS))
    col_mask = np.where(outside, NEG_INF, 0.0).astype(np.float32).reshape(1, -1)
    table = jnp.pad((rpb.astype(F32) * LOG2_E).reshape(depth * heads * n_dr, n_dc), ((0, 0), (0, n_dc_pad - n_dc)))
    blocks = pl.pallas_call(
        _rel_bias_kernel,
        out_shape=jax.ShapeDtypeStruct((depth * heads * n_dr, GRID_W * GRID_W), F32),
    )(table, jnp.asarray(onehot), jnp.asarray(col_mask))
    blocks = blocks.reshape(depth, heads, n_dr, GRID_W, GRID_W)
    return jnp.concatenate([blocks, jnp.full((depth, heads, 1, GRID_W, GRID_W), NEG_INF, F32)], axis=2)


def _nb_bias_tiles(rpb, n_lat, s_tot):
    rows, kh, w_rows, reps, starts, variants = _nb_plan(n_lat, s_tot)
    q_rows = TQ_B // GRID_W
    masked_block = 2 * NA_ROWS - 1
    dr_map = np.full((len(reps) + 1, w_rows, q_rows), masked_block, np.int32)
    for v, (i, w0) in enumerate(reps):
        for a in range(w_rows):
            for b in range(q_rows):
                r = q_rows * i + b
                r0 = int(np.clip(r - kh // 2, 0, rows - kh))
                if r0 <= w0 + a < r0 + kh:
                    dr_map[v, a, b] = w0 + a - r + NA_ROWS - 1
    blocks = _rel_bias_blocks(rpb)[:, :, dr_map]
    depth = rpb.shape[0]
    tiles = jnp.transpose(blocks, (0, 2, 3, 5, 1, 4, 6)).reshape(
        depth, len(reps) + 1, w_rows * GRID_W, B_HEADS * TQ_B)
    return tiles, jnp.asarray(starts), jnp.asarray(variants), w_rows * GRID_W


def _attn_b_kernel(start_ref, var_ref, fast_ref, ksc_ref, q_ref, qn_ref, k_ref, v_ref, bias_ref, o_ref,
                   qp_ref, ot_ref, m_ref, *, n_lat, n_ctx, n_keys):
    tq = q_ref.shape[1]
    ws = pl.multiple_of(start_ref[pl.program_id(1)], 128)
    qp_ref[...] = jnp.zeros_like(qp_ref)
    for h in range(B_HEADS):
        qp_ref[h * HEAD_DIM:(h + 1) * HEAD_DIM, h * tq:(h + 1) * tq] = q_ref[h * HEAD_DIM:(h + 1) * HEAD_DIM, :]
    qp = qp_ref[...]
    s_loc = jnp.dot(k_ref[pl.ds(ws, n_keys), :], qp, preferred_element_type=F32) + bias_ref[...]
    s_ctx = jnp.dot(k_ref[n_lat:n_lat + n_ctx, :], qp, preferred_element_type=F32)
    for bounded in (True, False):
        @pl.when((fast_ref[0] == 1) == bounded)
        def _():
            if bounded:
                m_ref[...] = jnp.concatenate(
                    [jnp.sqrt(qn_ref[QN_B + h:QN_B + h + 1, :]) * ksc_ref[pl.program_id(0) * B_HEADS + h]
                     for h in range(B_HEADS)], axis=1)
            else:
                m_ref[...] = jnp.maximum(jnp.max(s_loc, axis=0, keepdims=True), jnp.max(s_ctx, axis=0, keepdims=True))
    m = m_ref[...]
    p_loc = jnp.exp2(s_loc - m)
    p_ctx = jnp.exp2(s_ctx - m)
    denom = jnp.sum(p_loc, axis=0, keepdims=True) + jnp.sum(p_ctx, axis=0, keepdims=True)
    acc = (jnp.dot(v_ref[:, pl.ds(ws, n_keys)], p_loc.astype(BF16), preferred_element_type=F32)
           + jnp.dot(v_ref[:, n_lat:n_lat + n_ctx], p_ctx.astype(BF16), preferred_element_type=F32))
    for h in range(B_HEADS):
        ot_ref[h * HEAD_DIM:(h + 1) * HEAD_DIM, :] = (
            acc[h * HEAD_DIM:(h + 1) * HEAD_DIM, h * tq:(h + 1) * tq] / denom[:, h * tq:(h + 1) * tq])
    o_ref[...] = ot_ref[...].T.astype(BF16)


def _attn_b(q_t, q_norm2, k, k_norm2, v_t, bias_tiles, max_bias, starts, variants, n_keys, n_lat):
    bsz, s_tot, _ = k.shape
    tq = TQ_B
    k_scale = jnp.sqrt(jnp.max(k_norm2[:, KN_B:KN_B + B_HEADS, :], axis=-1))
    bound = jnp.sqrt(jnp.max(q_norm2[:, QN_B:QN_B + B_HEADS, :])) * jnp.max(k_scale) + max_bias
    fast = (bound <= MAX_SCORE_BOUND).astype(jnp.int32).reshape(1)
    kern = functools.partial(_attn_b_kernel, n_lat=n_lat, n_ctx=s_tot - n_lat, n_keys=n_keys)
    smem = pl.BlockSpec(memory_space=pltpu.SMEM)
    grid_spec = pltpu.PrefetchScalarGridSpec(
        num_scalar_prefetch=2,
        grid=(bsz, s_tot // tq),
        in_specs=[smem, smem,
                  pl.BlockSpec((None, QB_W, tq), lambda b, i, st, va: (b, 0, i)),
                  pl.BlockSpec((None, QN_ROWS, tq), lambda b, i, st, va: (b, 0, i)),
                  pl.BlockSpec((None, s_tot, KB_W), lambda b, i, st, va: (b, 0, 0)),
                  pl.BlockSpec((None, KB_W, s_tot), lambda b, i, st, va: (b, 0, 0)),
                  pl.BlockSpec((None, n_keys, B_HEADS * tq), lambda b, i, st, va: (va[i], 0, 0))],
        out_specs=pl.BlockSpec((None, tq, QB_W), lambda b, i, st, va: (b, i, 0)),
        scratch_shapes=[pltpu.VMEM((QB_W, B_HEADS * tq), BF16), pltpu.VMEM((QB_W, tq), F32),
                        pltpu.VMEM((1, B_HEADS * tq), F32)],
    )
    return pl.pallas_call(
        kern,
        grid_spec=grid_spec,
        out_shape=jax.ShapeDtypeStruct((bsz, s_tot, QB_W), BF16),
        compiler_params=_params(("parallel", "arbitrary")),
    )(starts, variants, fast, k_scale.reshape(-1), q_t, q_norm2, k, v_t, bias_tiles)


def _attn_c_kernel(fast_ref, ksc_ref, q_ref, qn_ref, k_ref, v_ref, o_ref, qp_ref, m_ref, acc_ref, ot_ref,
                   *, n_lat, n_ctx, tk):
    tq = q_ref.shape[1]
    is_ctx = pl.program_id(1) >= n_lat // tq
    n_chunks = jnp.where(is_ctx, 0, n_lat // tk - 1)
    fast = fast_ref[0] == 1
    first_row = lax.broadcasted_iota(jnp.int32, (KC_W, tq), 0) == 0
    for g in range(C_KV):
        k_scale = ksc_ref[pl.program_id(0) * C_KV + g]
        for hh in range(GQA_GROUP):
            h = g * GQA_GROUP + hh
            qp_ref[g, 0:KC_W, hh * tq:(hh + 1) * tq] = _padded_q(q_ref, h, g)
            offset = jnp.where(fast, -jnp.sqrt(qn_ref[h:h + 1, :]) * k_scale, 0.0)
            qp_ref[g, KC_W:2 * KC_W, hh * tq:(hh + 1) * tq] = jnp.where(first_row, offset, 0.0).astype(BF16)
    acc_ref[...] = jnp.zeros_like(acc_ref)

    def bounded_step(g, start, size):
        s = jnp.dot(k_ref[pl.ds(start, size), :], qp_ref[g], preferred_element_type=F32)
        p = jnp.exp2(s).astype(BF16)
        acc_ref[g] += jnp.dot(v_ref[g, :, pl.ds(start, size)], p, preferred_element_type=F32)

    def online_step(g, start, size):
        s = jnp.dot(k_ref[pl.ds(start, size), :], qp_ref[g], preferred_element_type=F32)
        m_prev = m_ref[g]
        m_new = jnp.maximum(m_prev, jnp.max(s, axis=0, keepdims=True))
        p = jnp.exp2(s - m_new).astype(BF16)
        pv = jnp.dot(v_ref[g, :, pl.ds(start, size)], p, preferred_element_type=F32)
        acc_ref[g] = acc_ref[g] * jnp.exp2(m_prev - m_new) + pv
        m_ref[g] = m_new

    def sweep(step):
        def body(c, carry):
            for g in range(C_KV):
                step(g, pl.multiple_of(c * tk, tk), tk)
            return carry

        lax.fori_loop(0, n_chunks, body, 0)

        @pl.when(is_ctx)
        def _():
            for g in range(C_KV):
                step(g, n_lat, n_ctx)

        @pl.when(jnp.logical_not(is_ctx))
        def _():
            for g in range(C_KV):
                step(g, n_lat - tk, tk + n_ctx)

    @pl.when(fast)
    def _():
        sweep(bounded_step)

    @pl.when(jnp.logical_not(fast))
    def _():
        m_ref[...] = jnp.full_like(m_ref, NEG_INF)
        sweep(online_step)

    for g in range(C_KV):
        acc = acc_ref[g]
        out = acc[0:HEAD_DIM, :] / acc[HEAD_DIM:HEAD_DIM + 1, :]
        for hh in range(GQA_GROUP):
            h = g * GQA_GROUP + hh
            ot_ref[h * HEAD_DIM:(h + 1) * HEAD_DIM, :] = out[:, hh * tq:(hh + 1) * tq]
    o_ref[...] = ot_ref[...].T.astype(BF16)


def _attn_c(q_t, q_norm2, k, k_norm2, v_t, n_lat):
    bsz, s_tot, _ = k.shape
    tq = TQ_C
    tk = min(TK_C, n_lat)
    k_scale = jnp.sqrt(jnp.max(k_norm2[:, KN_C:KN_C + C_KV, :], axis=-1)) * BOUND_SLACK
    bound = jnp.sqrt(jnp.max(q_norm2[:, QN_C:QN_C + C_HEADS, :])) * jnp.max(k_scale)
    fast = (bound <= MAX_SCORE_BOUND).astype(jnp.int32).reshape(1)
    kern = functools.partial(_attn_c_kernel, n_lat=n_lat, n_ctx=s_tot - n_lat, tk=tk)
    smem = pl.BlockSpec(memory_space=pltpu.SMEM)
    return pl.pallas_call(
        kern,
        grid=(bsz, s_tot // tq),
        in_specs=[smem, smem,
                  pl.BlockSpec((None, QC_W, tq), lambda b, i: (b, 0, i)),
                  pl.BlockSpec((None, QN_ROWS, tq), lambda b, i: (b, 0, i)),
                  pl.BlockSpec((None, s_tot, 2 * KC_W), lambda b, i: (b, 0, 0)),
                  pl.BlockSpec((None, C_KV, V_ROWS, s_tot), lambda b, i: (b, 0, 0, 0))],
        out_specs=pl.BlockSpec((None, tq, QC_W), lambda b, i: (b, i, 0)),
        out_shape=jax.ShapeDtypeStruct((bsz, s_tot, QC_W), BF16),
        scratch_shapes=[pltpu.VMEM((C_KV, 2 * KC_W, GQA_GROUP * tq), BF16),
                        pltpu.VMEM((C_KV, 1, GQA_GROUP * tq), F32),
                        pltpu.VMEM((C_KV, V_ROWS, GQA_GROUP * tq), F32),
                        pltpu.VMEM((QC_W, tq), F32)],
        compiler_params=_params(("parallel", "arbitrary")),
    )(fast, k_scale.reshape(-1), q_t, q_norm2, k, v_t)


def _store_token_tiles(ref, val, pitch=TILE_ROWS, first=0):
    n = val.shape[0]
    for c in range(TILE_ROWS):
        ref[pl.ds(first + c, n, stride=pitch), :] = val[:, c * 128:(c + 1) * 128]


def _load_token_tiles(ref, n, pitch=TILE_ROWS, first=0):
    return jnp.concatenate([ref[pl.ds(first + c, n, stride=pitch), :] for c in range(TILE_ROWS)], axis=1)


def _tile_copy(src_hbm, dst, idx_ref, j, sem, pitch):
    src = pl.multiple_of(idx_ref[0, 0, j] * pitch, pitch)
    return pltpu.make_async_copy(src_hbm.at[pl.ds(src, pitch), :],
                                 dst.at[pl.ds(pl.multiple_of(j * pitch, pitch), pitch), :], sem)


def _start_gather(src_hbm, dst, idx_ref, n, sem, pitch=TILE_ROWS):
    def issue(i, carry):
        for u in range(GATHER_UNROLL):
            _tile_copy(src_hbm, dst, idx_ref, i * GATHER_UNROLL + u, sem, pitch).start()
        return carry

    lax.fori_loop(0, n // GATHER_UNROLL, issue, 0)


def _wait_gather(src_hbm, dst, n, sem, pitch=TILE_ROWS):
    pltpu.make_async_copy(src_hbm.at[pl.ds(0, n * pitch), :], dst, sem).wait()


def _layer_norm(h, g, b):
    mu = jnp.mean(h, axis=-1, keepdims=True)
    hc = h - mu
    var = jnp.mean(hc * hc, axis=-1, keepdims=True)
    return hc * lax.rsqrt(var + LN_EPS) * g + b


def _post_kernel(oa_ref, ob_ref, oc_ref, x_ref, mod_ref, w_ref, g_ref, b_ref, wr_ref, rb_ref,
                 x1_ref, u2_ref, eid_ref, gate_ref, o_scr, ub_scr, *, alpha):
    tm = x_ref.shape[0]
    o_cat = jnp.concatenate([oa_ref[...], ob_ref[...], oc_ref[...]], axis=1)
    o_scr[...] = jnp.dot(o_cat, w_ref[...], preferred_element_type=F32)
    gate1 = 1.0 + mod_ref[2:3, :]
    scale2 = 1.0 + mod_ref[4:5, :]
    shift2 = mod_ref[3:4, :]
    ln_g = g_ref[...]
    ln_b = b_ref[...]

    def row_group(r, carry):
        r0 = pl.multiple_of(r * LN_ROWS, LN_ROWS)
        rows = pl.ds(r0, LN_ROWS)
        x1 = _layer_norm(alpha * x_ref[rows, :] + gate1 * o_scr[rows, :], ln_g, ln_b)
        x1_ref[rows, :] = x1
        u2 = x1 * scale2 + shift2
        ub_scr[rows, :] = u2.astype(BF16)
        for c in range(TILE_ROWS):
            u2_ref[pl.ds(r0 * TILE_ROWS + c, LN_ROWS, stride=TILE_ROWS), :] = u2[:, c * 128:(c + 1) * 128]
        return carry

    lax.fori_loop(0, tm // LN_ROWS, row_group, 0, unroll=LN_UNROLL)
    logits = lax.dot_general(wr_ref[...], ub_scr[...], (((1,), (1,)), ((), ())), preferred_element_type=F32)
    aff_all = jax.nn.sigmoid(logits)
    sel_all = aff_all + rb_ref[...]
    aff = [aff_all[e:e + 1, :] for e in range(N_EXPERTS)]
    sel = [sel_all[e:e + 1, :] for e in range(N_EXPERTS)]
    gsum = []
    for g in range(N_EXPERT_GROUPS):
        a, b, c, d = sel[g * PER_GROUP:(g + 1) * PER_GROUP]
        hi1, lo1, hi2, lo2 = jnp.maximum(a, b), jnp.minimum(a, b), jnp.maximum(c, d), jnp.minimum(c, d)
        gsum.append(jnp.maximum(hi1, hi2) + jnp.maximum(jnp.minimum(hi1, hi2), jnp.maximum(lo1, lo2)))
    best_g = jnp.zeros_like(gsum[0], dtype=jnp.int32)
    best_v = gsum[0]
    for g in range(1, N_EXPERT_GROUPS):
        better = gsum[g] > best_v
        best_g = jnp.where(better, g, best_g)
        best_v = jnp.where(better, gsum[g], best_v)
    s_loc, a_loc = [], []
    for j in range(PER_GROUP):
        sj, aj = sel[j], aff[j]
        for g in range(1, N_EXPERT_GROUPS):
            pick = best_g == g
            sj = jnp.where(pick, sel[g * PER_GROUP + j], sj)
            aj = jnp.where(pick, aff[g * PER_GROUP + j], aj)
        s_loc.append(sj)
        a_loc.append(aj)
    i1 = jnp.zeros_like(best_g)
    v1, g1 = s_loc[0], a_loc[0]
    for j in range(1, PER_GROUP):
        better = s_loc[j] > v1
        i1 = jnp.where(better, j, i1)
        v1 = jnp.where(better, s_loc[j], v1)
        g1 = jnp.where(better, a_loc[j], g1)
    i2 = jnp.full_like(best_g, -1)
    v2 = jnp.full_like(v1, -jnp.inf)
    g2 = jnp.zeros_like(g1)
    for j in range(PER_GROUP):
        better = (i1 != j) & ((s_loc[j] > v2) | (i2 < 0))
        i2 = jnp.where(better, j, i2)
        v2 = jnp.where(better, s_loc[j], v2)
        g2 = jnp.where(better, a_loc[j], g2)
    total = g1 + g2
    lo, hi = jnp.minimum(i1, i2), jnp.maximum(i1, i2)
    pair = jnp.where(lo == 0, hi - 1, jnp.where(lo == 1, hi + 1, PAIRS_PER_GROUP - 1))
    first_is_lo = i1 < i2
    g_lo = jnp.where(first_is_lo, g1, g2) / total
    g_hi = jnp.where(first_is_lo, g2, g1) / total
    eid_ref[...] = jnp.concatenate([best_g * PAIRS_PER_GROUP + pair,
                                    jnp.zeros((MOD_ROWS - 1, tm), jnp.int32)], axis=0)
    gate_ref[...] = jnp.concatenate([g_lo, g_hi, jnp.zeros((MOD_ROWS - TOP_K, tm), F32)], axis=0)


def _post(oa, ob, oc, xa, mod, w_out, ln_g, ln_b, wr_t, rb, n_lat, alpha):
    bsz, s_tot, d = xa.shape
    tm = TOK_TILE
    n_lat_tiles = n_lat // tm
    n_tiles = s_tot // tm
    sd = jax.ShapeDtypeStruct

    def tmajor(width):
        return pl.BlockSpec((None, tm, width), lambda b, i: (b, i, 0))

    def const(shape):
        return pl.BlockSpec(shape, lambda b, i: (0,) * len(shape))

    lanes = pl.BlockSpec((MOD_ROWS, tm), lambda b, i: (0, b * n_tiles + i))
    return pl.pallas_call(
        functools.partial(_post_kernel, alpha=alpha),
        grid=(bsz, n_tiles),
        in_specs=[tmajor(QA_W), tmajor(QB_W), tmajor(QC_W), tmajor(d),
                  pl.BlockSpec((None, N_MOD, d), lambda b, i: (jnp.where(i >= n_lat_tiles, bsz, b), 0, 0)),
                  const((d, d)), const((1, d)), const((1, d)), const((N_EXPERTS, d)), const((N_EXPERTS, 1))],
        out_specs=[tmajor(d), pl.BlockSpec((tm * TILE_ROWS, 128), lambda b, i: (b * n_tiles + i, 0)), lanes, lanes],
        out_shape=[sd((bsz, s_tot, d), F32), sd((bsz * s_tot * TILE_ROWS, 128), F32),
                   sd((MOD_ROWS, bsz * s_tot), jnp.int32), sd((MOD_ROWS, bsz * s_tot), F32)],
        scratch_shapes=[pltpu.VMEM((tm, d), F32), pltpu.VMEM((tm, d), BF16)],
        compiler_params=_params(("parallel", "parallel")),
    )(oa, ob, oc, xa, mod, w_out, ln_g, ln_b, wr_t, rb)


def _ffn_kernel(elo_ref, ehi_ref, nused_ref, tok_ref, tok_next_ref, u_hbm,
                w1lo_ref, w3lo_ref, w2lo_ref, w1hi_ref, w3hi_ref, w2hi_ref, y_ref,
                xbuf, w1lo, w3lo, w2lo, w1hi, w3hi, w2hi, sem):
    blk = y_ref.shape[0] // (TOP_K * TILE_ROWS)
    step = pl.program_id(0)
    n_used = nused_ref[0]
    slot = step % 2
    prev = jnp.maximum(step - 1, 0)

    @pl.when(step < n_used)
    def _():
        @pl.when(step == 0)
        def _():
            _start_gather(u_hbm, xbuf.at[0], tok_ref, blk, sem.at[0])

        @pl.when(step + 1 < n_used)
        def _():
            _start_gather(u_hbm, xbuf.at[1 - slot], tok_next_ref, blk, sem.at[1 - slot])

        @pl.when((step == 0) | (elo_ref[step] != elo_ref[prev]))
        def _():
            w1lo[...] = w1lo_ref[...].astype(BF16)
            w3lo[...] = w3lo_ref[...].astype(BF16)
            w2lo[...] = w2lo_ref[...].astype(BF16)

        @pl.when((step == 0) | (ehi_ref[step] != ehi_ref[prev]))
        def _():
            w1hi[...] = w1hi_ref[...].astype(BF16)
            w3hi[...] = w3hi_ref[...].astype(BF16)
            w2hi[...] = w2hi_ref[...].astype(BF16)

        _wait_gather(u_hbm, xbuf.at[slot], blk, sem.at[slot])
        xb = _load_token_tiles(xbuf.at[slot], blk).astype(BF16)
        for first, (w1b, w3b, w2b) in ((0, (w1lo, w3lo, w2lo)), (TILE_ROWS, (w1hi, w3hi, w2hi))):
            h1 = jnp.dot(xb, w1b[...], preferred_element_type=F32)
            h3 = jnp.dot(xb, w3b[...], preferred_element_type=F32)
            hid = (h1 * jax.nn.sigmoid(h1) * h3).astype(BF16)
            _store_token_tiles(y_ref, jnp.dot(hid, w2b[...], preferred_element_type=F32),
                               pitch=TOP_K * TILE_ROWS, first=first)

    @pl.when(step >= n_used)
    def _():
        y_ref[...] = jnp.zeros_like(y_ref)


def _expert_ffn(block_lo, block_hi, n_used, slot_tok, u2_tiles, w1, w3, w2, layer):
    d = w1.shape[2]
    n_blocks = block_lo.shape[0]
    blk = CLASS_BLOCK
    slot_tok = slot_tok.reshape(n_blocks, 1, blk)

    def weight(shape, which):
        return pl.BlockSpec((None, None) + shape, lambda i, lo, hi, nu: (layer, (lo, hi)[which][i], 0, 0))

    up, down = (d, D_EXPERT), (D_EXPERT, d)
    grid_spec = pltpu.PrefetchScalarGridSpec(
        num_scalar_prefetch=3,
        grid=(n_blocks,),
        in_specs=[pl.BlockSpec((1, 1, blk), lambda i, lo, hi, nu: (i, 0, 0), memory_space=pltpu.SMEM),
                  pl.BlockSpec((1, 1, blk), lambda i, lo, hi, nu: (jnp.minimum(i + 1, n_blocks - 1), 0, 0),
                               memory_space=pltpu.SMEM),
                  pl.BlockSpec(memory_space=pl.ANY),
                  weight(up, 0), weight(up, 0), weight(down, 0), weight(up, 1), weight(up, 1), weight(down, 1)],
        out_specs=pl.BlockSpec((blk * TOP_K * TILE_ROWS, 128), lambda i, lo, hi, nu: (i, 0)),
        scratch_shapes=[pltpu.VMEM((2, blk * TILE_ROWS, 128), F32)]
        + [pltpu.VMEM(s, BF16) for s in (up, up, down, up, up, down)] + [pltpu.SemaphoreType.DMA((2,))],
    )
    return pl.pallas_call(
        _ffn_kernel,
        grid_spec=grid_spec,
        out_shape=jax.ShapeDtypeStruct((n_blocks * blk * TOP_K * TILE_ROWS, 128), F32),
        compiler_params=_params(("arbitrary",)),
    )(block_lo, block_hi, n_used, slot_tok, slot_tok, u2_tiles, w1, w3, w2, w1, w3, w2)


def _comb_kernel(pos_ref, pos_next_ref, y_hbm, x1_ref, gate_ref, mod_ref, g_ref, b_ref, o_ref, ybuf, sem, *, alpha):
    tm = x1_ref.shape[0]
    pitch = TOP_K * TILE_ROWS
    step = pl.program_id(0)
    slot = step % 2

    @pl.when(step == 0)
    def _():
        _start_gather(y_hbm, ybuf.at[0], pos_ref, tm, sem.at[0], pitch)

    @pl.when(step + 1 < pl.num_programs(0))
    def _():
        _start_gather(y_hbm, ybuf.at[1 - slot], pos_next_ref, tm, sem.at[1 - slot], pitch)

    _wait_gather(y_hbm, ybuf.at[slot], tm, sem.at[slot], pitch)
    y_lo = _load_token_tiles(ybuf.at[slot], tm, pitch, 0)
    y_hi = _load_token_tiles(ybuf.at[slot], tm, pitch, TILE_ROWS)
    f = gate_ref[:, 0:1] * y_lo + gate_ref[:, 1:2] * y_hi
    o_ref[...] = _layer_norm(alpha * x1_ref[...] + (1.0 + mod_ref[5:6, :]) * f, g_ref[...], b_ref[...])


def _combine(pos, y_tiles, x1, gates, mod, ln_g, ln_b, n_lat, alpha):
    bsz, s_tot, d = x1.shape
    tm = TOK_TILE
    n_lat_tiles = n_lat // tm
    n_tiles = s_tot // tm
    n_steps = bsz * n_tiles

    def mod_row(t):
        return jnp.where(t % n_tiles >= n_lat_tiles, bsz, t // n_tiles)

    out = pl.pallas_call(
        functools.partial(_comb_kernel, alpha=alpha),
        grid=(n_steps,),
        in_specs=[pl.BlockSpec((1, 1, tm), lambda t: (t, 0, 0), memory_space=pltpu.SMEM),
                  pl.BlockSpec((1, 1, tm), lambda t: (jnp.minimum(t + 1, n_steps - 1), 0, 0),
                               memory_space=pltpu.SMEM),
                  pl.BlockSpec(memory_space=pl.ANY),
                  pl.BlockSpec((tm, d), lambda t: (t, 0)),
                  pl.BlockSpec((tm, TOP_K), lambda t: (t, 0)),
                  pl.BlockSpec((None, N_MOD, d), lambda t: (mod_row(t), 0, 0)),
                  pl.BlockSpec((1, d), lambda t: (0, 0)),
                  pl.BlockSpec((1, d), lambda t: (0, 0))],
        out_specs=pl.BlockSpec((tm, d), lambda t: (t, 0)),
        out_shape=jax.ShapeDtypeStruct((bsz * s_tot, d), F32),
        scratch_shapes=[pltpu.VMEM((2, TOP_K * tm * TILE_ROWS, 128), F32), pltpu.SemaphoreType.DMA((2,))],
        compiler_params=_params(("arbitrary",)),
    )(pos, pos, y_tiles, x1.reshape(bsz * s_tot, d), gates, mod, ln_g, ln_b)
    return out.reshape(bsz, s_tot, d)


def _deinterleave(n_heads):
    one = np.concatenate([np.arange(0, HEAD_DIM, 2), np.arange(1, HEAD_DIM, 2)])
    return np.concatenate([h * HEAD_DIM + one for h in range(n_heads)])


def _proj_weight(w_in_l):
    sizes = (QA_W, KA_W, KA_W, QB_W, KB_W, KB_W, QC_W, KC_W, KC_W)
    offs = np.concatenate([[0], np.cumsum(sizes)])
    qa, ka, va, qb, kb, vb, qc, kc, vc = [w_in_l[:, offs[i]:offs[i + 1]] for i in range(len(sizes))]
    scale = HEAD_DIM ** -0.5 * LOG2_E
    qa = qa[:, _deinterleave(A_HEADS)] * scale
    ka = ka[:, _deinterleave(A_KV)]
    qb = qb * scale
    qc = qc[:, _deinterleave(C_HEADS)]
    kc = kc[:, _deinterleave(C_KV)]
    return jnp.concatenate([qa, qb, qc, ka, kb, kc, va, vb, vc], axis=1).T.astype(BF16)


def _rope_tables(n_lat, n_ctx):
    t = np.arange(n_lat)
    inv_freq = ROPE_BASE ** (-np.arange(0, HALF, 2, dtype=np.float32) / HALF)
    ang = jnp.concatenate([jnp.asarray(t // GRID_W, F32)[None, :] * jnp.asarray(inv_freq)[:, None],
                           jnp.asarray(t % GRID_W, F32)[None, :] * jnp.asarray(inv_freq)[:, None]], axis=0)
    cos = jnp.concatenate([jnp.cos(ang), jnp.ones((HALF, n_ctx), F32)], axis=1)
    sin = jnp.concatenate([jnp.sin(ang), jnp.zeros((HALF, n_ctx), F32)], axis=1)
    return cos, sin


def _dispatch(cls, n_tok):
    onehot = (cls[:, None] == jnp.arange(N_CLASSES, dtype=jnp.int32)[None, :]).astype(jnp.int32)
    csum = jnp.cumsum(onehot, axis=0)
    counts = csum[-1]
    padded = (counts + CLASS_BLOCK - 1) // CLASS_BLOCK * CLASS_BLOCK
    pad_ends = jnp.cumsum(padded)
    dest = jnp.sum(onehot * (csum - 1 + (pad_ends - padded)[None, :]), axis=1)
    n_blocks = -(-n_tok // CLASS_BLOCK) + N_CLASSES
    slot_tok = jnp.zeros((n_blocks * CLASS_BLOCK,), jnp.int32).at[dest].set(jnp.arange(n_tok, dtype=jnp.int32))
    block_start = jnp.arange(n_blocks, dtype=jnp.int32) * CLASS_BLOCK
    block_cls = jnp.minimum(jnp.sum((block_start[:, None] >= pad_ends[None, :]).astype(jnp.int32), axis=1),
                            N_CLASSES - 1)
    group, pair = block_cls // PAIRS_PER_GROUP, block_cls % PAIRS_PER_GROUP
    lo = (pair >= 3).astype(jnp.int32) + (pair >= 5).astype(jnp.int32)
    hi = jnp.where(pair < 3, pair + 1, jnp.where(pair < 5, pair - 1, 3))
    n_used = (pad_ends[-1] // CLASS_BLOCK).astype(jnp.int32).reshape(1)
    return slot_tok, group * PER_GROUP + lo, group * PER_GROUP + hi, dest.astype(jnp.int32), n_used


def kernel(x, c, ctx, c_ctx, w_in, w_out, sink, rpb, q_gain, k_gain, w_ada, b_ada,
           ln1_g, ln1_b, ln2_g, ln2_b, w_router, router_bias, w1, w3, w2):
    bsz, n_lat, d = x.shape
    n_ctx = ctx.shape[1]
    s_tot = n_lat + n_ctx
    depth = w_in.shape[0]
    assert d == D_MODEL and n_ctx == TOK_TILE and n_lat % min(TK_C, n_lat) == 0
    assert n_lat % (2 * TOK_TILE) == 0 and bsz + 1 <= MOD_ROWS
    alpha = float((2 * depth) ** 0.25)
    n_tok = bsz * s_tot
    tm = TOK_TILE

    cond = jnp.zeros((MOD_ROWS, d), F32).at[:bsz].set(c).at[bsz].set(c_ctx)
    mods = _ada_table(cond, w_ada, b_ada).reshape(depth, MOD_ROWS, N_MOD, d)
    cos_t, sin_t = _rope_tables(n_lat, n_ctx)
    gain_perm = _deinterleave(1)
    wr_t = w_router.T.astype(BF16)
    rb = router_bias.reshape(N_EXPERTS, 1).astype(F32)

    bias_tiles, starts, variants, n_keys = _nb_bias_tiles(rpb, n_lat, s_tot)

    xa = jnp.concatenate([x, ctx], axis=1)
    for l in range(depth):
        mod = mods[l]
        qa, qb, qc, ka, kb, kc, va, vb, vc, qn, kn = _project(
            xa, mod, _proj_weight(w_in[l]), cos_t, sin_t,
            q_gain[l][gain_perm].reshape(HEAD_DIM, 1) * (HEAD_DIM ** -0.5 * LOG2_E),
            k_gain[l][gain_perm].reshape(HEAD_DIM, 1), n_lat)
        oa = _attn_a(sink[l], qa, qn, ka, kn, va, n_lat)
        ob = _attn_b(qb, qn, kb, kn, vb, bias_tiles[l], jnp.max(jnp.abs(rpb[l])) * LOG2_E,
                     starts, variants, n_keys, n_lat)
        oc = _attn_c(qc, qn, kc, kn, vc, n_lat)
        x1, u2, eid8, gate8 = _post(oa, ob, oc, xa, mod, w_out[l].astype(BF16),
                                    ln1_g[l].reshape(1, d), ln1_b[l].reshape(1, d), wr_t, rb, n_lat, alpha)
        slot_tok, block_lo, block_hi, dest, n_used = _dispatch(eid8[0], n_tok)
        y = _expert_ffn(block_lo, block_hi, n_used, slot_tok, u2, w1, w3, w2, l)
        xa = _combine(dest.reshape(n_tok // tm, 1, tm), y, x1, gate8[:TOP_K].T, mod,
                      ln2_g[l].reshape(1, d), ln2_b[l].reshape(1, d), n_lat, alpha)
    return xa[:, :n_lat]
```

```python
import functools

import numpy as np
import jax
import jax.numpy as jnp
from jax import lax
from jax.experimental import pallas as pl
from jax.experimental.pallas import tpu as pltpu

F32 = jnp.float32
BF16 = jnp.bfloat16

D_MODEL = 1024
HEAD_DIM = 64
HALF = HEAD_DIM // 2
GRID_W = 64
A_HEADS, A_KV = 6, 2
WINDOW = 128
B_HEADS = 4
NA_ROWS, NA_COLS = 8, 16
C_HEADS, C_KV = 6, 2
GQA_GROUP = 3
ROPE_BASE = 10000.0
N_EXPERTS = 16
N_EXPERT_GROUPS = 4
PER_GROUP = N_EXPERTS // N_EXPERT_GROUPS
TOP_K = 2
D_EXPERT = 512
PAIRS_PER_GROUP = PER_GROUP * (PER_GROUP - 1) // 2
N_CLASSES = N_EXPERT_GROUPS * PAIRS_PER_GROUP
CLASS_BLOCK = 256
N_MOD = 6
LN_EPS = 1e-6
QK_EPS = 1e-6
NEG_INF = -1e30

QA_W, QB_W, QC_W = A_HEADS * HEAD_DIM, B_HEADS * HEAD_DIM, C_HEADS * HEAD_DIM
KA_W, KB_W, KC_W = A_KV * HEAD_DIM, B_HEADS * HEAD_DIM, C_KV * HEAD_DIM
R_QA = 0
R_QB = R_QA + QA_W
R_QC = R_QB + QB_W
R_KA = R_QC + QC_W
R_KB = R_KA + KA_W
R_KC = R_KB + KB_W
R_VA = R_KC + KC_W
R_VB = R_VA + KA_W
R_VC = R_VB + KB_W
PROJ_WIDTH = R_VC + KC_W

V7X_VMEM_LIMIT = 56 * 1024 * 1024
TOK_TILE = 256
TQ_A = 256
TQ_B = 2 * GRID_W
TQ_C = 256
TK_C = 2048
MOD_ROWS = 8
LOG2_E = 1.4426950408889634
QN_C, QN_A, QN_B, QN_ROWS = 0, C_HEADS, C_HEADS + A_HEADS, C_HEADS + A_HEADS + B_HEADS
KN_C, KN_A, KN_B, KN_ROWS = 0, C_KV, C_KV + A_KV, C_KV + A_KV + B_HEADS
MAX_SCORE_BOUND = 60.0
BOUND_SLACK = 1.004
TILE_ROWS = 8
BF16_ROWS = 16
GATHER_UNROLL = 8
LN_ROWS = 64
LN_UNROLL = 4
V_ROWS = HEAD_DIM + BF16_ROWS


def _params(sem):
    return pltpu.CompilerParams(dimension_semantics=sem, vmem_limit_bytes=V7X_VMEM_LIMIT)


def _ada_kernel(c_ref, w_ref, b_ref, o_ref):
    c = c_ref[...]
    act = c * jax.nn.sigmoid(c)
    o_ref[...] = jnp.dot(act, w_ref[...], preferred_element_type=F32,
                         precision=lax.Precision.HIGHEST) + b_ref[...]


def _ada_table(cond, w_ada, b_ada):
    depth, d, n = w_ada.shape
    tn = n // 4
    return pl.pallas_call(
        _ada_kernel,
        grid=(depth, n // tn),
        in_specs=[pl.BlockSpec((MOD_ROWS, d), lambda l, j: (0, 0)),
                  pl.BlockSpec((None, d, tn), lambda l, j: (l, 0, j)),
                  pl.BlockSpec((None, 1, tn), lambda l, j: (l, 0, j))],
        out_specs=pl.BlockSpec((None, MOD_ROWS, tn), lambda l, j: (l, 0, j)),
        out_shape=jax.ShapeDtypeStruct((depth, MOD_ROWS, n), F32),
        compiler_params=_params(("parallel", "parallel")),
    )(cond, w_ada, b_ada.reshape(depth, 1, n))


def _sq_norm(parts):
    return sum(jnp.sum(jnp.square(p.astype(F32)), axis=0, keepdims=True) for p in parts)


def _proj_kernel(x_ref, mod_ref, w_ref, cos_ref, sin_ref, qg_ref, kg_ref,
                 qa_ref, qb_ref, qc_ref, ka_ref, kb_ref, kc_ref, va_ref, vb_ref, vc_ref, qn_ref, kn_ref):
    tm = x_ref.shape[0]
    u = (x_ref[...] * (1.0 + mod_ref[1:2, :]) + mod_ref[0:1, :]).astype(BF16)

    def proj(lo, width):
        return lax.dot_general(w_ref[lo:lo + width, :], u, (((1,), (1,)), ((), ())),
                               preferred_element_type=F32)

    cos = cos_ref[...]
    sin = sin_ref[...]

    def rope(blk):
        x0, x1 = blk[:HALF], blk[HALF:]
        return x0 * cos - x1 * sin, x0 * sin + x1 * cos

    def qk_norm(blk, gain):
        ms = jnp.mean(blk * blk, axis=0, keepdims=True)
        return blk * lax.rsqrt(ms + QK_EPS) * gain

    p = proj(R_QA, QA_W)
    for h in range(A_HEADS):
        o0, o1 = [o.astype(BF16) for o in rope(p[h * HEAD_DIM:(h + 1) * HEAD_DIM])]
        qa_ref[h * HEAD_DIM:h * HEAD_DIM + HALF, :] = o0
        qa_ref[h * HEAD_DIM + HALF:(h + 1) * HEAD_DIM, :] = o1
        qn_ref[QN_A + h:QN_A + h + 1, :] = _sq_norm([o0, o1])
    p = proj(R_QB, QB_W).astype(BF16)
    qb_ref[...] = p
    for h in range(B_HEADS):
        qn_ref[QN_B + h:QN_B + h + 1, :] = _sq_norm([p[h * HEAD_DIM:(h + 1) * HEAD_DIM]])
    p = proj(R_QC, QC_W)
    qg = qg_ref[...]
    for h in range(C_HEADS):
        o0, o1 = [o.astype(BF16) for o in rope(qk_norm(p[h * HEAD_DIM:(h + 1) * HEAD_DIM], qg))]
        qc_ref[h * HEAD_DIM:h * HEAD_DIM + HALF, :] = o0
        qc_ref[h * HEAD_DIM + HALF:(h + 1) * HEAD_DIM, :] = o1
        qn_ref[QN_C + h:QN_C + h + 1, :] = _sq_norm([o0, o1])
    one_lane = (lax.broadcasted_iota(jnp.int32, (tm, KA_W), 1) == 0).astype(BF16)
    p = proj(R_KA, KA_W)
    parts = []
    for h in range(A_KV):
        o0, o1 = [o.astype(BF16) for o in rope(p[h * HEAD_DIM:(h + 1) * HEAD_DIM])]
        kn_ref[KN_A + h:KN_A + h + 1, :] = _sq_norm([o0, o1])
        parts.extend([o0.astype(F32), o1.astype(F32)])
    ka_ref[:, 0:KA_W] = jnp.concatenate(parts, axis=0).T.astype(BF16)
    ka_ref[:, KA_W:2 * KA_W] = one_lane
    p = proj(R_KB, KB_W).astype(BF16)
    for h in range(B_HEADS):
        kn_ref[KN_B + h:KN_B + h + 1, :] = _sq_norm([p[h * HEAD_DIM:(h + 1) * HEAD_DIM]])
    kb_ref[...] = p.astype(F32).T.astype(BF16)
    p = proj(R_KC, KC_W)
    kg = kg_ref[...]
    parts = []
    for h in range(C_KV):
        o0, o1 = [o.astype(BF16) for o in rope(qk_norm(p[h * HEAD_DIM:(h + 1) * HEAD_DIM], kg))]
        kn_ref[KN_C + h:KN_C + h + 1, :] = _sq_norm([o0, o1])
        parts.extend([o0.astype(F32), o1.astype(F32)])
    kc_ref[:, 0:KC_W] = jnp.concatenate(parts, axis=0).T.astype(BF16)
    kc_ref[:, KC_W:2 * KC_W] = one_lane
    ones_row = (lax.broadcasted_iota(jnp.int32, (BF16_ROWS, tm), 0) == 0).astype(BF16)
    for lo, ref in ((R_VA, va_ref), (R_VC, vc_ref)):
        p = proj(lo, KA_W)
        for g in range(A_KV):
            ref[g, 0:HEAD_DIM, :] = p[g * HEAD_DIM:(g + 1) * HEAD_DIM].astype(BF16)
            ref[g, HEAD_DIM:V_ROWS, :] = ones_row
    vb_ref[...] = proj(R_VB, KB_W).astype(BF16)


def _project(xa, mod, w_t, cos_t, sin_t, q_gain, k_gain, n_lat):
    bsz, s_tot, d = xa.shape
    tm = TOK_TILE
    n_lat_tiles = n_lat // tm
    sd = jax.ShapeDtypeStruct

    def dmajor(width):
        return pl.BlockSpec((None, width, tm), lambda b, i: (b, 0, i))

    def tmajor(width):
        return pl.BlockSpec((None, tm, width), lambda b, i: (b, i, 0))

    aug = pl.BlockSpec((None, A_KV, V_ROWS,tm), lambda b, i: (b, 0, 0, i))
    return pl.pallas_call(
        _proj_kernel,
        grid=(bsz, s_tot // tm),
        in_specs=[tmajor(d),
                  pl.BlockSpec((None, N_MOD, d), lambda b, i: (jnp.where(i >= n_lat_tiles, bsz, b), 0, 0)),
                  pl.BlockSpec((PROJ_WIDTH, d), lambda b, i: (0, 0)),
                  pl.BlockSpec((HALF, tm), lambda b, i: (0, i)),
                  pl.BlockSpec((HALF, tm), lambda b, i: (0, i)),
                  pl.BlockSpec((HEAD_DIM, 1), lambda b, i: (0, 0)),
                  pl.BlockSpec((HEAD_DIM, 1), lambda b, i: (0, 0))],
        out_specs=[dmajor(QA_W), dmajor(QB_W), dmajor(QC_W),
                   tmajor(2 * KA_W), tmajor(KB_W), tmajor(2 * KC_W),
                   aug, dmajor(KB_W), aug, dmajor(QN_ROWS), dmajor(KN_ROWS)],
        out_shape=[sd((bsz, QA_W, s_tot), BF16), sd((bsz, QB_W, s_tot), BF16), sd((bsz, QC_W, s_tot), BF16),
                   sd((bsz, s_tot, 2 * KA_W), BF16), sd((bsz, s_tot, KB_W), BF16), sd((bsz, s_tot, 2 * KC_W), BF16),
                   sd((bsz, A_KV, V_ROWS, s_tot), BF16), sd((bsz, KB_W, s_tot), BF16),
                   sd((bsz, A_KV, V_ROWS, s_tot), BF16),
                   sd((bsz, QN_ROWS, s_tot), F32), sd((bsz, KN_ROWS, s_tot), F32)],
        compiler_params=_params(("parallel", "parallel")),
    )(xa, mod, w_t, cos_t, sin_t, q_gain, k_gain)


def _padded_q(q_ref, h, g):
    blk = q_ref[h * HEAD_DIM:(h + 1) * HEAD_DIM, :]
    zeros = jnp.zeros_like(blk)
    return jnp.concatenate([blk, zeros] if g == 0 else [zeros, blk], axis=0)


def _attn_a_kernel(fast_ref, ksc_ref, sink_ref, q_ref, qn_ref, k_ref, v_ref, o_ref, ot_ref, *, n_lat, n_ctx):
    tq = q_ref.shape[1]
    span = tq + 2 * WINDOW
    start = pl.program_id(1) * tq
    ks = pl.multiple_of(jnp.clip(start - WINDOW, 0, n_lat - span), 128)
    kpos = ks + lax.broadcasted_iota(jnp.int32, (span, tq), 0)
    qpos = start + lax.broadcasted_iota(jnp.int32, (span, tq), 1)
    ok = (jnp.abs(qpos - kpos) <= WINDOW) & (qpos < n_lat)
    k_loc = k_ref[pl.ds(ks, span), :]
    k_ctx = k_ref[n_lat:n_lat + n_ctx, :]
    first_row = lax.broadcasted_iota(jnp.int32, (KA_W, tq), 0) == 0

    def head(h, g, bounded):
        v_loc = v_ref[g, :, pl.ds(ks, span)]
        v_ctx = v_ref[g, :, n_lat:n_lat + n_ctx]
        sink = sink_ref[h]
        if bounded:
            offset = jnp.sqrt(qn_ref[QN_A + h:QN_A + h + 1, :]) * ksc_ref[pl.program_id(0) * A_KV + g]
            qp = jnp.concatenate([_padded_q(q_ref, h, g), jnp.where(first_row, -offset, 0.0).astype(BF16)], axis=0)
        else:
            qp = jnp.concatenate([_padded_q(q_ref, h, g), jnp.zeros((KA_W, tq), BF16)], axis=0)
        s_loc = jnp.where(ok, jnp.dot(k_loc, qp, preferred_element_type=F32), NEG_INF)
        s_ctx = jnp.dot(k_ctx, qp, preferred_element_type=F32)
        if bounded:
            m = offset
        else:
            m = jnp.maximum(jnp.maximum(jnp.max(s_loc, axis=0, keepdims=True),
                                        jnp.max(s_ctx, axis=0, keepdims=True)), sink)
            s_loc, s_ctx = s_loc - m, s_ctx - m
        acc = (jnp.dot(v_loc, jnp.exp2(s_loc).astype(BF16), preferred_element_type=F32)
               + jnp.dot(v_ctx, jnp.exp2(s_ctx).astype(BF16), preferred_element_type=F32))
        denom = acc[HEAD_DIM:HEAD_DIM + 1, :] + jnp.exp2(sink - m)
        ot_ref[h * HEAD_DIM:(h + 1) * HEAD_DIM, :] = acc[0:HEAD_DIM, :] / denom

    for bounded in (True, False):
        @pl.when((fast_ref[0] == 1) == bounded)
        def _():
            for g in range(A_KV):
                for hh in range(GQA_GROUP):
                    head(g * GQA_GROUP + hh, g, bounded)

    o_ref[...] = ot_ref[...].T.astype(BF16)


def _attn_a(sink, q_t, q_norm2, k, k_norm2, v_t, n_lat):
    bsz, s_tot, _ = k.shape
    tq = TQ_A
    sink2 = sink.astype(F32) * LOG2_E
    k_scale = jnp.sqrt(jnp.max(k_norm2[:, KN_A:KN_A + A_KV, :], axis=-1)) * BOUND_SLACK
    bound = jnp.sqrt(jnp.max(q_norm2[:, QN_A:QN_A + A_HEADS, :])) * jnp.max(k_scale)
    fast = ((bound <= MAX_SCORE_BOUND) & (jnp.max(jnp.abs(sink2)) <= MAX_SCORE_BOUND)).astype(jnp.int32).reshape(1)
    kern = functools.partial(_attn_a_kernel, n_lat=n_lat, n_ctx=s_tot - n_lat)
    smem = pl.BlockSpec(memory_space=pltpu.SMEM)
    return pl.pallas_call(
        kern,
        grid=(bsz, s_tot // tq),
        in_specs=[smem, smem, smem,
                  pl.BlockSpec((None, QA_W, tq), lambda b, i: (b, 0, i)),
                  pl.BlockSpec((None, QN_ROWS, tq), lambda b, i: (b, 0, i)),
                  pl.BlockSpec((None, s_tot, 2 * KA_W), lambda b, i: (b, 0, 0)),
                  pl.BlockSpec((None, A_KV, V_ROWS, s_tot), lambda b, i: (b, 0, 0, 0))],
        out_specs=pl.BlockSpec((None, tq, QA_W), lambda b, i: (b, i, 0)),
        out_shape=jax.ShapeDtypeStruct((bsz, s_tot, QA_W), BF16),
        scratch_shapes=[pltpu.VMEM((QA_W, tq), F32)],
        compiler_params=_params(("parallel", "arbitrary")),
    )(fast, k_scale.reshape(-1), sink2, q_t, q_norm2, k, v_t)


def _nb_plan(n_lat, s_tot):
    rows = n_lat // GRID_W
    kh = min(NA_ROWS, rows)
    q_rows = TQ_B // GRID_W
    w_rows = kh + q_rows
    n_lat_blocks = n_lat // TQ_B
    n_blocks = s_tot // TQ_B
    starts, variants, keys, reps = [], [], {}, []
    for i in range(n_lat_blocks):
        w0 = int(np.clip(q_rows * i - kh // 2, 0, rows - w_rows))
        r0s = tuple(int(np.clip(q_rows * i + j - kh // 2, 0, rows - kh)) - w0 for j in range(q_rows))
        key = (q_rows * i - w0,) + r0s
        if key not in keys:
            keys[key] = len(reps)
            reps.append((i, w0))
        starts.append(w0 * GRID_W)
        variants.append(keys[key])
    masked = len(reps)
    starts.extend([0] * (n_blocks - n_lat_blocks))
    variants.extend([masked] * (n_blocks - n_lat_blocks))
    return rows, kh, w_rows, reps, np.asarray(starts, np.int32), np.asarray(variants, np.int32)


def _rel_bias_kernel(r_ref, e_ref, m_ref, o_ref):
    o_ref[...] = jnp.dot(r_ref[...], e_ref[...], preferred_element_type=F32,
                         precision=lax.Precision.HIGHEST) + m_ref[...]


def _rel_bias_blocks(rpb):
    depth, heads, n_dr, n_dc = rpb.shape
    kc = np.arange(GRID_W)[:, None]
    cq = np.arange(GRID_W)[None, :]
    dc = (kc - cq + NA_COLS - 1).reshape(-1)
    n_dc_pad = 32
    onehot = (np.arange(n_dc_pad)[:, None] == dc[None, :]).astype(np.float32)
    c0 = np.clip(cq - NA_COLS // 2, 0, GRID_W - NA_COLS)
    outside = ~((kc >= c0) & (kc < c0 + NA_COLS))
    col_mask = np.where(outside, NEG_INF, 0.0).astype(np.float32).reshape(1, -1)
    table = jnp.pad((rpb.astype(F32) * LOG2_E).reshape(depth * heads * n_dr, n_dc), ((0, 0), (0, n_dc_pad - n_dc)))
    blocks = pl.pallas_call(
        _rel_bias_kernel,
        out_shape=jax.ShapeDtypeStruct((depth * heads * n_dr, GRID_W * GRID_W), F32),
    )(table, jnp.asarray(onehot), jnp.asarray(col_mask))
    blocks = blocks.reshape(depth, heads, n_dr, GRID_W, GRID_W)
    return jnp.concatenate([blocks, jnp.full((depth, heads, 1, GRID_W, GRID_W), NEG_INF, F32)], axis=2)


def _nb_bias_tiles(rpb, n_lat, s_tot):
    rows, kh, w_rows, reps, starts, variants = _nb_plan(n_lat, s_tot)
    q_rows = TQ_B // GRID_W
    masked_block = 2 * NA_ROWS - 1
    dr_map = np.full((len(reps) + 1, w_rows, q_rows), masked_block, np.int32)
    for v, (i, w0) in enumerate(reps):
        for a in range(w_rows):
            for b in range(q_rows):
                r = q_rows * i + b
                r0 = int(np.clip(r - kh // 2, 0, rows - kh))
                if r0 <= w0 + a < r0 + kh:
                    dr_map[v, a, b] = w0 + a - r + NA_ROWS - 1
    blocks = _rel_bias_blocks(rpb)[:, :, dr_map]
    depth = rpb.shape[0]
    tiles = jnp.transpose(blocks, (0, 2, 3, 5, 1, 4, 6)).reshape(
        depth, len(reps) + 1, w_rows * GRID_W, B_HEADS * TQ_B)
    return tiles, jnp.asarray(starts), jnp.asarray(variants), w_rows * GRID_W


def _attn_b_kernel(start_ref, var_ref, fast_ref, ksc_ref, q_ref, qn_ref, k_ref, v_ref, bias_ref, o_ref,
                   qp_ref, ot_ref, m_ref, *, n_lat, n_ctx, n_keys):
    tq = q_ref.shape[1]
    ws = pl.multiple_of(start_ref[pl.program_id(1)], 128)
    qp_ref[...] = jnp.zeros_like(qp_ref)
    for h in range(B_HEADS):
        qp_ref[h * HEAD_DIM:(h + 1) * HEAD_DIM, h * tq:(h + 1) * tq] = q_ref[h * HEAD_DIM:(h + 1) * HEAD_DIM, :]
    qp = qp_ref[...]
    s_loc = jnp.dot(k_ref[pl.ds(ws, n_keys), :], qp, preferred_element_type=F32) + bias_ref[...]
    s_ctx = jnp.dot(k_ref[n_lat:n_lat + n_ctx, :], qp, preferred_element_type=F32)
    for bounded in (True, False):
        @pl.when((fast_ref[0] == 1) == bounded)
        def _():
            if bounded:
                m_ref[...] = jnp.concatenate(
                    [jnp.sqrt(qn_ref[QN_B + h:QN_B + h + 1, :]) * ksc_ref[pl.program_id(0) * B_HEADS + h]
                     for h in range(B_HEADS)], axis=1)
            else:
                m_ref[...] = jnp.maximum(jnp.max(s_loc, axis=0, keepdims=True), jnp.max(s_ctx, axis=0, keepdims=True))
    m = m_ref[...]
    p_loc = jnp.exp2(s_loc - m)
    p_ctx = jnp.exp2(s_ctx - m)
    denom = jnp.sum(p_loc, axis=0, keepdims=True) + jnp.sum(p_ctx, axis=0, keepdims=True)
    acc = (jnp.dot(v_ref[:, pl.ds(ws, n_keys)], p_loc.astype(BF16), preferred_element_type=F32)
           + jnp.dot(v_ref[:, n_lat:n_lat + n_ctx], p_ctx.astype(BF16), preferred_element_type=F32))
    for h in range(B_HEADS):
        ot_ref[h * HEAD_DIM:(h + 1) * HEAD_DIM, :] = (
            acc[h * HEAD_DIM:(h + 1) * HEAD_DIM, h * tq:(h + 1) * tq] / denom[:, h * tq:(h + 1) * tq])
    o_ref[...] = ot_ref[...].T.astype(BF16)


def _attn_b(q_t, q_norm2, k, k_norm2, v_t, bias_tiles, max_bias, starts, variants, n_keys, n_lat):
    bsz, s_tot, _ = k.shape
    tq = TQ_B
    k_scale = jnp.sqrt(jnp.max(k_norm2[:, KN_B:KN_B + B_HEADS, :], axis=-1))
    bound = jnp.sqrt(jnp.max(q_norm2[:, QN_B:QN_B + B_HEADS, :])) * jnp.max(k_scale) + max_bias
    fast = (bound <= MAX_SCORE_BOUND).astype(jnp.int32).reshape(1)
    kern = functools.partial(_attn_b_kernel, n_lat=n_lat, n_ctx=s_tot - n_lat, n_keys=n_keys)
    smem = pl.BlockSpec(memory_space=pltpu.SMEM)
    grid_spec = pltpu.PrefetchScalarGridSpec(
        num_scalar_prefetch=2,
        grid=(bsz, s_tot // tq),
        in_specs=[smem, smem,
                  pl.BlockSpec((None, QB_W, tq), lambda b, i, st, va: (b, 0, i)),
                  pl.BlockSpec((None, QN_ROWS, tq), lambda b, i, st, va: (b, 0, i)),
                  pl.BlockSpec((None, s_tot, KB_W), lambda b, i, st, va: (b, 0, 0)),
                  pl.BlockSpec((None, KB_W, s_tot), lambda b, i, st, va: (b, 0, 0)),
                  pl.BlockSpec((None, n_keys, B_HEADS * tq), lambda b, i, st, va: (va[i], 0, 0))],
        out_specs=pl.BlockSpec((None, tq, QB_W), lambda b, i, st, va: (b, i, 0)),
        scratch_shapes=[pltpu.VMEM((QB_W, B_HEADS * tq), BF16), pltpu.VMEM((QB_W, tq), F32),
                        pltpu.VMEM((1, B_HEADS * tq), F32)],
    )
    return pl.pallas_call(
        kern,
        grid_spec=grid_spec,
        out_shape=jax.ShapeDtypeStruct((bsz, s_tot, QB_W), BF16),
        compiler_params=_params(("parallel", "arbitrary")),
    )(starts, variants, fast, k_scale.reshape(-1), q_t, q_norm2, k, v_t, bias_tiles)


def _attn_c_kernel(fast_ref, ksc_ref, q_ref, qn_ref, k_ref, v_ref, o_ref, qp_ref, m_ref, acc_ref, ot_ref,
                   *, n_lat, n_ctx, tk):
    tq = q_ref.shape[1]
    is_ctx = pl.program_id(1) >= n_lat // tq
    n_chunks = jnp.where(is_ctx, 0, n_lat // tk - 1)
    fast = fast_ref[0] == 1
    first_row = lax.broadcasted_iota(jnp.int32, (KC_W, tq), 0) == 0
    for g in range(C_KV):
        k_scale = ksc_ref[pl.program_id(0) * C_KV + g]
        for hh in range(GQA_GROUP):
            h = g * GQA_GROUP + hh
            qp_ref[g, 0:KC_W, hh * tq:(hh + 1) * tq] = _padded_q(q_ref, h, g)
            offset = jnp.where(fast, -jnp.sqrt(qn_ref[h:h + 1, :]) * k_scale, 0.0)
            qp_ref[g, KC_W:2 * KC_W, hh * tq:(hh + 1) * tq] = jnp.where(first_row, offset, 0.0).astype(BF16)
    acc_ref[...] = jnp.zeros_like(acc_ref)

    def bounded_step(g, start, size):
        s = jnp.dot(k_ref[pl.ds(start, size), :], qp_ref[g], preferred_element_type=F32)
        p = jnp.exp2(s).astype(BF16)
        acc_ref[g] += jnp.dot(v_ref[g, :, pl.ds(start, size)], p, preferred_element_type=F32)

    def online_step(g, start, size):
        s = jnp.dot(k_ref[pl.ds(start, size), :], qp_ref[g], preferred_element_type=F32)
        m_prev = m_ref[g]
        m_new = jnp.maximum(m_prev, jnp.max(s, axis=0, keepdims=True))
        p = jnp.exp2(s - m_new).astype(BF16)
        pv = jnp.dot(v_ref[g, :, pl.ds(start, size)], p, preferred_element_type=F32)
        acc_ref[g] = acc_ref[g] * jnp.exp2(m_prev - m_new) + pv
        m_ref[g] = m_new

    def sweep(step):
        def body(c, carry):
            for g in range(C_KV):
                step(g, pl.multiple_of(c * tk, tk), tk)
            return carry

        lax.fori_loop(0, n_chunks, body, 0)

        @pl.when(is_ctx)
        def _():
            for g in range(C_KV):
                step(g, n_lat, n_ctx)

        @pl.when(jnp.logical_not(is_ctx))
        def _():
            for g in range(C_KV):
                step(g, n_lat - tk, tk + n_ctx)

    @pl.when(fast)
    def _():
        sweep(bounded_step)

    @pl.when(jnp.logical_not(fast))
    def _():
        m_ref[...] = jnp.full_like(m_ref, NEG_INF)
        sweep(online_step)

    for g in range(C_KV):
        acc = acc_ref[g]
        out = acc[0:HEAD_DIM, :] / acc[HEAD_DIM:HEAD_DIM + 1, :]
        for hh in range(GQA_GROUP):
            h = g * GQA_GROUP + hh
            ot_ref[h * HEAD_DIM:(h + 1) * HEAD_DIM, :] = out[:, hh * tq:(hh + 1) * tq]
    o_ref[...] = ot_ref[...].T.astype(BF16)


def _attn_c(q_t, q_norm2, k, k_norm2, v_t, n_lat):
    bsz, s_tot, _ = k.shape
    tq = TQ_C
    tk = min(TK_C, n_lat)
    k_scale = jnp.sqrt(jnp.max(k_norm2[:, KN_C:KN_C + C_KV, :], axis=-1)) * BOUND_SLACK
    bound = jnp.sqrt(jnp.max(q_norm2[:, QN_C:QN_C + C_HEADS, :])) * jnp.max(k_scale)
    fast = (bound <= MAX_SCORE_BOUND).astype(jnp.int32).reshape(1)
    kern = functools.partial(_attn_c_kernel, n_lat=n_lat, n_ctx=s_tot - n_lat, tk=tk)
    smem = pl.BlockSpec(memory_space=pltpu.SMEM)
    return pl.pallas_call(
        kern,
        grid=(bsz, s_tot // tq),
        in_specs=[smem, smem,
                  pl.BlockSpec((None, QC_W, tq), lambda b, i: (b, 0, i)),
                  pl.BlockSpec((None, QN_ROWS, tq), lambda b, i: (b, 0, i)),
                  pl.BlockSpec((None, s_tot, 2 * KC_W), lambda b, i: (b, 0, 0)),
                  pl.BlockSpec((None, C_KV, V_ROWS, s_tot), lambda b, i: (b, 0, 0, 0))],
        out_specs=pl.BlockSpec((None, tq, QC_W), lambda b, i: (b, i, 0)),
        out_shape=jax.ShapeDtypeStruct((bsz, s_tot, QC_W), BF16),
        scratch_shapes=[pltpu.VMEM((C_KV, 2 * KC_W, GQA_GROUP * tq), BF16),
                        pltpu.VMEM((C_KV, 1, GQA_GROUP * tq), F32),
                        pltpu.VMEM((C_KV, V_ROWS, GQA_GROUP * tq), F32),
                        pltpu.VMEM((QC_W, tq), F32)],
        compiler_params=_params(("parallel", "arbitrary")),
    )(fast, k_scale.reshape(-1), q_t, q_norm2, k, v_t)


def _store_token_tiles(ref, val, pitch=TILE_ROWS, first=0):
    n = val.shape[0]
    for c in range(TILE_ROWS):
        ref[pl.ds(first + c, n, stride=pitch), :] = val[:, c * 128:(c + 1) * 128]


def _load_token_tiles(ref, n, pitch=TILE_ROWS, first=0):
    return jnp.concatenate([ref[pl.ds(first + c, n, stride=pitch), :] for c in range(TILE_ROWS)], axis=1)


def _tile_copy(src_hbm, dst, idx_ref, j, sem, pitch):
    src = pl.multiple_of(idx_ref[0, 0, j] * pitch, pitch)
    return pltpu.make_async_copy(src_hbm.at[pl.ds(src, pitch), :],
                                 dst.at[pl.ds(pl.multiple_of(j * pitch, pitch), pitch), :], sem)


def _start_gather(src_hbm, dst, idx_ref, n, sem, pitch=TILE_ROWS):
    def issue(i, carry):
        for u in range(GATHER_UNROLL):
            _tile_copy(src_hbm, dst, idx_ref, i * GATHER_UNROLL + u, sem, pitch).start()
        return carry

    lax.fori_loop(0, n // GATHER_UNROLL, issue, 0)


def _wait_gather(src_hbm, dst, n, sem, pitch=TILE_ROWS):
    pltpu.make_async_copy(src_hbm.at[pl.ds(0, n * pitch), :], dst, sem).wait()


def _layer_norm(h, g, b):
    mu = jnp.mean(h, axis=-1, keepdims=True)
    hc = h - mu
    var = jnp.mean(hc * hc, axis=-1, keepdims=True)
    return hc * lax.rsqrt(var + LN_EPS) * g + b


def _post_kernel(oa_ref, ob_ref, oc_ref, x_ref, mod_ref, w_ref, g_ref, b_ref, wr_ref, rb_ref,
                 x1_ref, u2_ref, eid_ref, gate_ref, o_scr, ub_scr, *, alpha):
    tm = x_ref.shape[0]
    o_cat = jnp.concatenate([oa_ref[...], ob_ref[...], oc_ref[...]], axis=1)
    o_scr[...] = jnp.dot(o_cat, w_ref[...], preferred_element_type=F32)
    gate1 = 1.0 + mod_ref[2:3, :]
    scale2 = 1.0 + mod_ref[4:5, :]
    shift2 = mod_ref[3:4, :]
    ln_g = g_ref[...]
    ln_b = b_ref[...]

    def row_group(r, carry):
        r0 = pl.multiple_of(r * LN_ROWS, LN_ROWS)
        rows = pl.ds(r0, LN_ROWS)
        x1 = _layer_norm(alpha * x_ref[rows, :] + gate1 * o_scr[rows, :], ln_g, ln_b)
        x1_ref[rows, :] = x1
        u2 = x1 * scale2 + shift2
        ub_scr[rows, :] = u2.astype(BF16)
        for c in range(TILE_ROWS):
            u2_ref[pl.ds(r0 * TILE_ROWS + c, LN_ROWS, stride=TILE_ROWS), :] = u2[:, c * 128:(c + 1) * 128]
        return carry

    lax.fori_loop(0, tm // LN_ROWS, row_group, 0, unroll=LN_UNROLL)
    logits = lax.dot_general(wr_ref[...], ub_scr[...], (((1,), (1,)), ((), ())), preferred_element_type=F32)
    aff_all = jax.nn.sigmoid(logits)
    sel_all = aff_all + rb_ref[...]
    aff = [aff_all[e:e + 1, :] for e in range(N_EXPERTS)]
    sel = [sel_all[e:e + 1, :] for e in range(N_EXPERTS)]
    gsum = []
    for g in range(N_EXPERT_GROUPS):
        a, b, c, d = sel[g * PER_GROUP:(g + 1) * PER_GROUP]
        hi1, lo1, hi2, lo2 = jnp.maximum(a, b), jnp.minimum(a, b), jnp.maximum(c, d), jnp.minimum(c, d)
        gsum.append(jnp.maximum(hi1, hi2) + jnp.maximum(jnp.minimum(hi1, hi2), jnp.maximum(lo1, lo2)))
    best_g = jnp.zeros_like(gsum[0], dtype=jnp.int32)
    best_v = gsum[0]
    for g in range(1, N_EXPERT_GROUPS):
        better = gsum[g] > best_v
        best_g = jnp.where(better, g, best_g)
        best_v = jnp.where(better, gsum[g], best_v)
    s_loc, a_loc = [], []
    for j in range(PER_GROUP):
        sj, aj = sel[j], aff[j]
        for g in range(1, N_EXPERT_GROUPS):
            pick = best_g == g
            sj = jnp.where(pick, sel[g * PER_GROUP + j], sj)
            aj = jnp.where(pick, aff[g * PER_GROUP + j], aj)
        s_loc.append(sj)
        a_loc.append(aj)
    i1 = jnp.zeros_like(best_g)
    v1, g1 = s_loc[0], a_loc[0]
    for j in range(1, PER_GROUP):
        better = s_loc[j] > v1
        i1 = jnp.where(better, j, i1)
        v1 = jnp.where(better, s_loc[j], v1)
        g1 = jnp.where(better, a_loc[j], g1)
    i2 = jnp.full_like(best_g, -1)
    v2 = jnp.full_like(v1, -jnp.inf)
    g2 = jnp.zeros_like(g1)
    for j in range(PER_GROUP):
        better = (i1 != j) & ((s_loc[j] > v2) | (i2 < 0))
        i2 = jnp.where(better, j, i2)
        v2 = jnp.where(better, s_loc[j], v2)
        g2 = jnp.where(better, a_loc[j], g2)
    total = g1 + g2
    lo, hi = jnp.minimum(i1, i2), jnp.maximum(i1, i2)
    pair = jnp.where(lo == 0, hi - 1, jnp.where(lo == 1, hi + 1, PAIRS_PER_GROUP - 1))
    first_is_lo = i1 < i2
    g_lo = jnp.where(first_is_lo, g1, g2) / total
    g_hi = jnp.where(first_is_lo, g2, g1) / total
    eid_ref[...] = jnp.concatenate([best_g * PAIRS_PER_GROUP + pair,
                                    jnp.zeros((MOD_ROWS - 1, tm), jnp.int32)], axis=0)
    gate_ref[...] = jnp.concatenate([g_lo, g_hi, jnp.zeros((MOD_ROWS - TOP_K, tm), F32)], axis=0)


def _post(oa, ob, oc, xa, mod, w_out, ln_g, ln_b, wr_t, rb, n_lat, alpha):
    bsz, s_tot, d = xa.shape
    tm = TOK_TILE
    n_lat_tiles = n_lat // tm
    n_tiles = s_tot // tm
    sd = jax.ShapeDtypeStruct

    def tmajor(width):
        return pl.BlockSpec((None, tm, width), lambda b, i: (b, i, 0))

    def const(shape):
        return pl.BlockSpec(shape, lambda b, i: (0,) * len(shape))

    lanes = pl.BlockSpec((MOD_ROWS, tm), lambda b, i: (0, b * n_tiles + i))
    return pl.pallas_call(
        functools.partial(_post_kernel, alpha=alpha),
        grid=(bsz, n_tiles),
        in_specs=[tmajor(QA_W), tmajor(QB_W), tmajor(QC_W), tmajor(d),
                  pl.BlockSpec((None, N_MOD, d), lambda b, i: (jnp.where(i >= n_lat_tiles, bsz, b), 0, 0)),
                  const((d, d)), const((1, d)), const((1, d)), const((N_EXPERTS, d)), const((N_EXPERTS, 1))],
        out_specs=[tmajor(d), pl.BlockSpec((tm * TILE_ROWS, 128), lambda b, i: (b * n_tiles + i, 0)), lanes, lanes],
        out_shape=[sd((bsz, s_tot, d), F32), sd((bsz * s_tot * TILE_ROWS, 128), F32),
                   sd((MOD_ROWS, bsz * s_tot), jnp.int32), sd((MOD_ROWS, bsz * s_tot), F32)],
        scratch_shapes=[pltpu.VMEM((tm, d), F32), pltpu.VMEM((tm, d), BF16)],
        compiler_params=_params(("parallel", "parallel")),
    )(oa, ob, oc, xa, mod, w_out, ln_g, ln_b, wr_t, rb)


def _ffn_kernel(elo_ref, ehi_ref, nused_ref, tok_ref, tok_next_ref, u_hbm,
                w1lo_ref, w3lo_ref, w2lo_ref, w1hi_ref, w3hi_ref, w2hi_ref, y_ref,
                xbuf, w1lo, w3lo, w2lo, w1hi, w3hi, w2hi, sem):
    blk = y_ref.shape[0] // (TOP_K * TILE_ROWS)
    step = pl.program_id(0)
    n_used = nused_ref[0]
    slot = step % 2
    prev = jnp.maximum(step - 1, 0)

    @pl.when(step < n_used)
    def _():
        @pl.when(step == 0)
        def _():
            _start_gather(u_hbm, xbuf.at[0], tok_ref, blk, sem.at[0])

        @pl.when(step + 1 < n_used)
        def _():
            _start_gather(u_hbm, xbuf.at[1 - slot], tok_next_ref, blk, sem.at[1 - slot])

        @pl.when((step == 0) | (elo_ref[step] != elo_ref[prev]))
        def _():
            w1lo[...] = w1lo_ref[...].astype(BF16)
            w3lo[...] = w3lo_ref[...].astype(BF16)
            w2lo[...] = w2lo_ref[...].astype(BF16)

        @pl.when((step == 0) | (ehi_ref[step] != ehi_ref[prev]))
        def _():
            w1hi[...] = w1hi_ref[...].astype(BF16)
            w3hi[...] = w3hi_ref[...].astype(BF16)
            w2hi[...] = w2hi_ref[...].astype(BF16)

        _wait_gather(u_hbm, xbuf.at[slot], blk, sem.at[slot])
        xb = _load_token_tiles(xbuf.at[slot], blk).astype(BF16)
        for first, (w1b, w3b, w2b) in ((0, (w1lo, w3lo, w2lo)), (TILE_ROWS, (w1hi, w3hi, w2hi))):
            h1 = jnp.dot(xb, w1b[...], preferred_element_type=F32)
            h3 = jnp.dot(xb, w3b[...], preferred_element_type=F32)
            hid = (h1 * jax.nn.sigmoid(h1) * h3).astype(BF16)
            _store_token_tiles(y_ref, jnp.dot(hid, w2b[...], preferred_element_type=F32),
                               pitch=TOP_K * TILE_ROWS, first=first)

    @pl.when(step >= n_used)
    def _():
        y_ref[...] = jnp.zeros_like(y_ref)


def _expert_ffn(block_lo, block_hi, n_used, slot_tok, u2_tiles, w1, w3, w2, layer):
    d = w1.shape[2]
    n_blocks = block_lo.shape[0]
    blk = CLASS_BLOCK
    slot_tok = slot_tok.reshape(n_blocks, 1, blk)

    def weight(shape, which):
        return pl.BlockSpec((None, None) + shape, lambda i, lo, hi, nu: (layer, (lo, hi)[which][i], 0, 0))

    up, down = (d, D_EXPERT), (D_EXPERT, d)
    grid_spec = pltpu.PrefetchScalarGridSpec(
        num_scalar_prefetch=3,
        grid=(n_blocks,),
        in_specs=[pl.BlockSpec((1, 1, blk), lambda i, lo, hi, nu: (i, 0, 0), memory_space=pltpu.SMEM),
                  pl.BlockSpec((1, 1, blk), lambda i, lo, hi, nu: (jnp.minimum(i + 1, n_blocks - 1), 0, 0),
                               memory_space=pltpu.SMEM),
                  pl.BlockSpec(memory_space=pl.ANY),
                  weight(up, 0), weight(up, 0), weight(down, 0), weight(up, 1), weight(up, 1), weight(down, 1)],
        out_specs=pl.BlockSpec((blk * TOP_K * TILE_ROWS, 128), lambda i, lo, hi, nu: (i, 0)),
        scratch_shapes=[pltpu.VMEM((2, blk * TILE_ROWS, 128), F32)]
        + [pltpu.VMEM(s, BF16) for s in (up, up, down, up, up, down)] + [pltpu.SemaphoreType.DMA((2,))],
    )
    return pl.pallas_call(
        _ffn_kernel,
        grid_spec=grid_spec,
        out_shape=jax.ShapeDtypeStruct((n_blocks * blk * TOP_K * TILE_ROWS, 128), F32),
        compiler_params=_params(("arbitrary",)),
    )(block_lo, block_hi, n_used, slot_tok, slot_tok, u2_tiles, w1, w3, w2, w1, w3, w2)


def _comb_kernel(pos_ref, pos_next_ref, y_hbm, x1_ref, gate_ref, mod_ref, g_ref, b_ref, o_ref, ybuf, sem,
                 *, alpha, latent_tiles):
    tm = x1_ref.shape[0]
    pitch = TOP_K * TILE_ROWS
    step = pl.program_id(0)
    slot = step % 2

    @pl.when(step == 0)
    def _():
        _start_gather(y_hbm, ybuf.at[0], pos_ref, tm, sem.at[0], pitch)

    @pl.when(step + 1 < pl.num_programs(0))
    def _():
        _start_gather(y_hbm, ybuf.at[1 - slot], pos_next_ref, tm, sem.at[1 - slot], pitch)

    _wait_gather(y_hbm, ybuf.at[slot], tm, sem.at[slot], pitch)
    y_lo = _load_token_tiles(ybuf.at[slot], tm, pitch, 0)
    y_hi = _load_token_tiles(ybuf.at[slot], tm, pitch, TILE_ROWS)
    f = gate_ref[:, 0:1] * y_lo + gate_ref[:, 1:2] * y_hi

    def finish():
        o_ref[...] = _layer_norm(alpha * x1_ref[...] + (1.0 + mod_ref[5:6, :]) * f, g_ref[...], b_ref[...])

    if latent_tiles is None:
        finish()
    else:
        n_tiles, n_lat_tiles = latent_tiles
        pl.when(step % n_tiles < n_lat_tiles)(finish)


def _combine(pos, y_tiles, x1, gates, mod, ln_g, ln_b, n_lat, alpha, latent_only):
    bsz, s_tot, d = x1.shape
    tm = TOK_TILE
    n_lat_tiles = n_lat // tm
    n_tiles = s_tot // tm
    n_steps = bsz * n_tiles
    out_tokens = n_lat if latent_only else s_tot

    def mod_row(t):
        return jnp.where(t % n_tiles >= n_lat_tiles, bsz, t // n_tiles)

    def out_block(t):
        if not latent_only:
            return t
        return (t // n_tiles) * n_lat_tiles + jnp.minimum(t % n_tiles, n_lat_tiles - 1)

    out = pl.pallas_call(
        functools.partial(_comb_kernel, alpha=alpha, latent_tiles=(n_tiles, n_lat_tiles) if latent_only else None),
        grid=(n_steps,),
        in_specs=[pl.BlockSpec((1, 1, tm), lambda t: (t, 0, 0), memory_space=pltpu.SMEM),
                  pl.BlockSpec((1, 1, tm), lambda t: (jnp.minimum(t + 1, n_steps - 1), 0, 0),
                               memory_space=pltpu.SMEM),
                  pl.BlockSpec(memory_space=pl.ANY),
                  pl.BlockSpec((tm, d), lambda t: (t, 0)),
                  pl.BlockSpec((tm, TOP_K), lambda t: (t, 0)),
                  pl.BlockSpec((None, N_MOD, d), lambda t: (mod_row(t), 0, 0)),
                  pl.BlockSpec((1, d), lambda t: (0, 0)),
                  pl.BlockSpec((1, d), lambda t: (0, 0))],
        out_specs=pl.BlockSpec((tm, d), lambda t: (out_block(t), 0)),
        out_shape=jax.ShapeDtypeStruct((bsz * out_tokens, d), F32),
        scratch_shapes=[pltpu.VMEM((2, TOP_K * tm * TILE_ROWS, 128), F32), pltpu.SemaphoreType.DMA((2,))],
        compiler_params=_params(("arbitrary",)),
    )(pos, pos, y_tiles, x1.reshape(bsz * s_tot, d), gates, mod, ln_g, ln_b)
    return out.reshape(bsz, out_tokens, d)


def _deinterleave(n_heads):
    one = np.concatenate([np.arange(0, HEAD_DIM, 2), np.arange(1, HEAD_DIM, 2)])
    return np.concatenate([h * HEAD_DIM + one for h in range(n_heads)])


def _proj_weight(w_in_l):
    sizes = (QA_W, KA_W, KA_W, QB_W, KB_W, KB_W, QC_W, KC_W, KC_W)
    offs = np.concatenate([[0], np.cumsum(sizes)])
    qa, ka, va, qb, kb, vb, qc, kc, vc = [w_in_l[:, offs[i]:offs[i + 1]] for i in range(len(sizes))]
    scale = HEAD_DIM ** -0.5 * LOG2_E
    qa = qa[:, _deinterleave(A_HEADS)] * scale
    ka = ka[:, _deinterleave(A_KV)]
    qb = qb * scale
    qc = qc[:, _deinterleave(C_HEADS)]
    kc = kc[:, _deinterleave(C_KV)]
    return jnp.concatenate([qa, qb, qc, ka, kb, kc, va, vb, vc], axis=1).T.astype(BF16)


def _rope_tables(n_lat, n_ctx):
    t = np.arange(n_lat)
    inv_freq = ROPE_BASE ** (-np.arange(0, HALF, 2, dtype=np.float32) / HALF)
    ang = jnp.concatenate([jnp.asarray(t // GRID_W, F32)[None, :] * jnp.asarray(inv_freq)[:, None],
                           jnp.asarray(t % GRID_W, F32)[None, :] * jnp.asarray(inv_freq)[:, None]], axis=0)
    cos = jnp.concatenate([jnp.cos(ang), jnp.ones((HALF, n_ctx), F32)], axis=1)
    sin = jnp.concatenate([jnp.sin(ang), jnp.zeros((HALF, n_ctx), F32)], axis=1)
    return cos, sin


def _dispatch(cls, n_tok):
    onehot = (cls[:, None] == jnp.arange(N_CLASSES, dtype=jnp.int32)[None, :]).astype(jnp.int32)
    csum = jnp.cumsum(onehot, axis=0)
    counts = csum[-1]
    padded = (counts + CLASS_BLOCK - 1) // CLASS_BLOCK * CLASS_BLOCK
    pad_ends = jnp.cumsum(padded)
    dest = jnp.sum(onehot * (csum - 1 + (pad_ends - padded)[None, :]), axis=1)
    n_blocks = -(-n_tok // CLASS_BLOCK) + N_CLASSES
    slot_tok = jnp.zeros((n_blocks * CLASS_BLOCK,), jnp.int32).at[dest].set(jnp.arange(n_tok, dtype=jnp.int32))
    block_start = jnp.arange(n_blocks, dtype=jnp.int32) * CLASS_BLOCK
    block_cls = jnp.minimum(jnp.sum((block_start[:, None] >= pad_ends[None, :]).astype(jnp.int32), axis=1),
                            N_CLASSES - 1)
    group, pair = block_cls // PAIRS_PER_GROUP, block_cls % PAIRS_PER_GROUP
    lo = (pair >= 3).astype(jnp.int32) + (pair >= 5).astype(jnp.int32)
    hi = jnp.where(pair < 3, pair + 1, jnp.where(pair < 5, pair - 1, 3))
    n_used = (pad_ends[-1] // CLASS_BLOCK).astype(jnp.int32).reshape(1)
    return slot_tok, group * PER_GROUP + lo, group * PER_GROUP + hi, dest.astype(jnp.int32), n_used


def kernel(x, c, ctx, c_ctx, w_in, w_out, sink, rpb, q_gain, k_gain, w_ada, b_ada,
           ln1_g, ln1_b, ln2_g, ln2_b, w_router, router_bias, w1, w3, w2):
    bsz, n_lat, d = x.shape
    n_ctx = ctx.shape[1]
    s_tot = n_lat + n_ctx
    depth = w_in.shape[0]
    assert d == D_MODEL and n_ctx == TOK_TILE and n_lat % min(TK_C, n_lat) == 0
    assert n_lat % (2 * TOK_TILE) == 0 and bsz + 1 <= MOD_ROWS
    alpha = float((2 * depth) ** 0.25)
    n_tok = bsz * s_tot
    tm = TOK_TILE

    cond = jnp.zeros((MOD_ROWS, d), F32).at[:bsz].set(c).at[bsz].set(c_ctx)
    mods = _ada_table(cond, w_ada, b_ada).reshape(depth, MOD_ROWS, N_MOD, d)
    cos_t, sin_t = _rope_tables(n_lat, n_ctx)
    gain_perm = _deinterleave(1)
    wr_t = w_router.T.astype(BF16)
    rb = router_bias.reshape(N_EXPERTS, 1).astype(F32)

    bias_tiles, starts, variants, n_keys = _nb_bias_tiles(rpb, n_lat, s_tot)

    xa = jnp.concatenate([x, ctx], axis=1)
    for l in range(depth):
        mod = mods[l]
        qa, qb, qc, ka, kb, kc, va, vb, vc, qn, kn = _project(
            xa, mod, _proj_weight(w_in[l]), cos_t, sin_t,
            q_gain[l][gain_perm].reshape(HEAD_DIM, 1) * (HEAD_DIM ** -0.5 * LOG2_E),
            k_gain[l][gain_perm].reshape(HEAD_DIM, 1), n_lat)
        oa = _attn_a(sink[l], qa, qn, ka, kn, va, n_lat)
        ob = _attn_b(qb, qn, kb, kn, vb, bias_tiles[l], jnp.max(jnp.abs(rpb[l])) * LOG2_E,
                     starts, variants, n_keys, n_lat)
        oc = _attn_c(qc, qn, kc, kn, vc, n_lat)
        x1, u2, eid8, gate8 = _post(oa, ob, oc, xa, mod, w_out[l].astype(BF16),
                                    ln1_g[l].reshape(1, d), ln1_b[l].reshape(1, d), wr_t, rb, n_lat, alpha)
        slot_tok, block_lo, block_hi, dest, n_used = _dispatch(eid8[0], n_tok)
        y = _expert_ffn(block_lo, block_hi, n_used, slot_tok, u2, w1, w3, w2, l)
        xa = _combine(dest.reshape(n_tok // tm, 1, tm), y, x1, gate8[:TOP_K].T, mod,
                      ln2_g[l].reshape(1, d), ln2_b[l].reshape(1, d), n_lat, alpha, latent_only=l == depth - 1)
    return xa
```

```python
import functools

import numpy as np
import jax
import jax.numpy as jnp
from jax import lax
from jax.experimental import pallas as pl
from jax.experimental.pallas import tpu as pltpu

F32 = jnp.float32
BF16 = jnp.bfloat16

D_MODEL = 1024
HEAD_DIM = 64
HALF = HEAD_DIM // 2
GRID_W = 64
A_HEADS, A_KV = 6, 2
WINDOW = 128
B_HEADS = 4
NA_ROWS, NA_COLS = 8, 16
C_HEADS, C_KV = 6, 2
GQA_GROUP = 3
ROPE_BASE = 10000.0
N_EXPERTS = 16
N_EXPERT_GROUPS = 4
PER_GROUP = N_EXPERTS // N_EXPERT_GROUPS
TOP_K = 2
D_EXPERT = 512
PAIRS_PER_GROUP = PER_GROUP * (PER_GROUP - 1) // 2
N_CLASSES = N_EXPERT_GROUPS * PAIRS_PER_GROUP
CLASS_BLOCK = 256
N_MOD = 6
LN_EPS = 1e-6
QK_EPS = 1e-6
NEG_INF = -1e30

QA_W, QB_W, QC_W = A_HEADS * HEAD_DIM, B_HEADS * HEAD_DIM, C_HEADS * HEAD_DIM
KA_W, KB_W, KC_W = A_KV * HEAD_DIM, B_HEADS * HEAD_DIM, C_KV * HEAD_DIM
R_QA = 0
R_QB = R_QA + QA_W
R_QC = R_QB + QB_W
R_KA = R_QC + QC_W
R_KB = R_KA + KA_W
R_KC = R_KB + KB_W
R_VA = R_KC + KC_W
R_VB = R_VA + KA_W
R_VC = R_VB + KB_W
PROJ_WIDTH = R_VC + KC_W

V7X_VMEM_LIMIT = 56 * 1024 * 1024
TOK_TILE = 256
TQ_A = 256
TQ_B = 2 * GRID_W
TQ_C = 256
TK_C = 2048
MOD_ROWS = 8
LOG2_E = 1.4426950408889634
QN_C, QN_A, QN_B, QN_ROWS = 0, C_HEADS, C_HEADS + A_HEADS, C_HEADS + A_HEADS + B_HEADS
KN_C, KN_A, KN_B, KN_ROWS = 0, C_KV, C_KV + A_KV, C_KV + A_KV + B_HEADS
MAX_SCORE_BOUND = 60.0
BOUND_SLACK = 1.004
TILE_ROWS = 8
BF16_ROWS = 16
GATHER_UNROLL = 8
LN_ROWS = 64
LN_UNROLL = 4
V_ROWS = HEAD_DIM + BF16_ROWS


def _params(sem):
    return pltpu.CompilerParams(dimension_semantics=sem, vmem_limit_bytes=V7X_VMEM_LIMIT)


def _ada_kernel(c_ref, w_ref, b_ref, o_ref):
    c = c_ref[...]
    act = c * jax.nn.sigmoid(c)
    o_ref[...] = jnp.dot(act, w_ref[...], preferred_element_type=F32,
                         precision=lax.Precision.HIGHEST) + b_ref[...]


def _ada_table(cond, w_ada, b_ada):
    depth, d, n = w_ada.shape
    tn = n // 4
    return pl.pallas_call(
        _ada_kernel,
        grid=(depth, n // tn),
        in_specs=[pl.BlockSpec((MOD_ROWS, d), lambda l, j: (0, 0)),
                  pl.BlockSpec((None, d, tn), lambda l, j: (l, 0, j)),
                  pl.BlockSpec((None, 1, tn), lambda l, j: (l, 0, j))],
        out_specs=pl.BlockSpec((None, MOD_ROWS, tn), lambda l, j: (l, 0, j)),
        out_shape=jax.ShapeDtypeStruct((depth, MOD_ROWS, n), F32),
        compiler_params=_params(("parallel", "parallel")),
    )(cond, w_ada, b_ada.reshape(depth, 1, n))


def _sq_norm(parts):
    return sum(jnp.sum(jnp.square(p.astype(F32)), axis=0, keepdims=True) for p in parts)


def _proj_kernel(x_ref, mod_ref, w_ref, cos_ref, sin_ref, qg_ref, kg_ref,
                 qa_ref, qb_ref, qc_ref, ka_ref, kb_ref, kc_ref, va_ref, vb_ref, vc_ref, qn_ref, kn_ref):
    tm = x_ref.shape[0]
    u = (x_ref[...] * (1.0 + mod_ref[1:2, :]) + mod_ref[0:1, :]).astype(BF16)

    def proj(lo, width):
        return lax.dot_general(w_ref[lo:lo + width, :], u, (((1,), (1,)), ((), ())),
                               preferred_element_type=F32)

    cos = cos_ref[...]
    sin = sin_ref[...]

    def rope(blk):
        x0, x1 = blk[:HALF], blk[HALF:]
        return x0 * cos - x1 * sin, x0 * sin + x1 * cos

    def qk_norm(blk, gain):
        ms = jnp.mean(blk * blk, axis=0, keepdims=True)
        return blk * lax.rsqrt(ms + QK_EPS) * gain

    p = proj(R_QA, QA_W)
    for h in range(A_HEADS):
        o0, o1 = [o.astype(BF16) for o in rope(p[h * HEAD_DIM:(h + 1) * HEAD_DIM])]
        qa_ref[h * HEAD_DIM:h * HEAD_DIM + HALF, :] = o0
        qa_ref[h * HEAD_DIM + HALF:(h + 1) * HEAD_DIM, :] = o1
        qn_ref[QN_A + h:QN_A + h + 1, :] = _sq_norm([o0, o1])
    p = proj(R_QB, QB_W).astype(BF16)
    qb_ref[...] = p
    for h in range(B_HEADS):
        qn_ref[QN_B + h:QN_B + h + 1, :] = _sq_norm([p[h * HEAD_DIM:(h + 1) * HEAD_DIM]])
    p = proj(R_QC, QC_W)
    qg = qg_ref[...]
    for h in range(C_HEADS):
        o0, o1 = [o.astype(BF16) for o in rope(qk_norm(p[h * HEAD_DIM:(h + 1) * HEAD_DIM], qg))]
        qc_ref[h * HEAD_DIM:h * HEAD_DIM + HALF, :] = o0
        qc_ref[h * HEAD_DIM + HALF:(h + 1) * HEAD_DIM, :] = o1
        qn_ref[QN_C + h:QN_C + h + 1, :] = _sq_norm([o0, o1])
    one_lane = (lax.broadcasted_iota(jnp.int32, (tm, KA_W), 1) == 0).astype(BF16)
    p = proj(R_KA, KA_W)
    parts = []
    for h in range(A_KV):
        o0, o1 = [o.astype(BF16) for o in rope(p[h * HEAD_DIM:(h + 1) * HEAD_DIM])]
        kn_ref[KN_A + h:KN_A + h + 1, :] = _sq_norm([o0, o1])
        parts.extend([o0.astype(F32), o1.astype(F32)])
    ka_ref[:, 0:KA_W] = jnp.concatenate(parts, axis=0).T.astype(BF16)
    ka_ref[:, KA_W:2 * KA_W] = one_lane
    p = proj(R_KB, KB_W).astype(BF16)
    for h in range(B_HEADS):
        kn_ref[KN_B + h:KN_B + h + 1, :] = _sq_norm([p[h * HEAD_DIM:(h + 1) * HEAD_DIM]])
    kb_ref[...] = p.astype(F32).T.astype(BF16)
    p = proj(R_KC, KC_W)
    kg = kg_ref[...]
    parts = []
    for h in range(C_KV):
        o0, o1 = [o.astype(BF16) for o in rope(qk_norm(p[h * HEAD_DIM:(h + 1) * HEAD_DIM], kg))]
        kn_ref[KN_C + h:KN_C + h + 1, :] = _sq_norm([o0, o1])
        parts.extend([o0.astype(F32), o1.astype(F32)])
    kc_ref[:, 0:KC_W] = jnp.concatenate(parts, axis=0).T.astype(BF16)
    kc_ref[:, KC_W:2 * KC_W] = one_lane
    ones_row = (lax.broadcasted_iota(jnp.int32, (BF16_ROWS, tm), 0) == 0).astype(BF16)
    for lo, ref in ((R_VA, va_ref), (R_VC, vc_ref)):
        p = proj(lo, KA_W)
        for g in range(A_KV):
            ref[g, 0:HEAD_DIM, :] = p[g * HEAD_DIM:(g + 1) * HEAD_DIM].astype(BF16)
            ref[g, HEAD_DIM:V_ROWS, :] = ones_row
    vb_ref[...] = proj(R_VB, KB_W).astype(BF16)


def _project(xa, mod, w_t, cos_t, sin_t, q_gain, k_gain, n_lat):
    bsz, s_tot, d = xa.shape
    tm = TOK_TILE
    n_lat_tiles = n_lat // tm
    sd = jax.ShapeDtypeStruct

    def dmajor(width):
        return pl.BlockSpec((None, width, tm), lambda b, i: (b, 0, i))

    def tmajor(width):
        return pl.BlockSpec((None, tm, width), lambda b, i: (b, i, 0))

    aug = pl.BlockSpec((None, A_KV, V_ROWS,tm), lambda b, i: (b, 0, 0, i))
    return pl.pallas_call(
        _proj_kernel,
        grid=(bsz, s_tot // tm),
        in_specs=[tmajor(d),
                  pl.BlockSpec((None, N_MOD, d), lambda b, i: (jnp.where(i >= n_lat_tiles, bsz, b), 0, 0)),
                  pl.BlockSpec((PROJ_WIDTH, d), lambda b, i: (0, 0)),
                  pl.BlockSpec((HALF, tm), lambda b, i: (0, i)),
                  pl.BlockSpec((HALF, tm), lambda b, i: (0, i)),
                  pl.BlockSpec((HEAD_DIM, 1), lambda b, i: (0, 0)),
                  pl.BlockSpec((HEAD_DIM, 1), lambda b, i: (0, 0))],
        out_specs=[dmajor(QA_W), dmajor(QB_W), dmajor(QC_W),
                   tmajor(2 * KA_W), tmajor(KB_W), tmajor(2 * KC_W),
                   aug, dmajor(KB_W), aug, dmajor(QN_ROWS), dmajor(KN_ROWS)],
        out_shape=[sd((bsz, QA_W, s_tot), BF16), sd((bsz, QB_W, s_tot), BF16), sd((bsz, QC_W, s_tot), BF16),
                   sd((bsz, s_tot, 2 * KA_W), BF16), sd((bsz, s_tot, KB_W), BF16), sd((bsz, s_tot, 2 * KC_W), BF16),
                   sd((bsz, A_KV, V_ROWS, s_tot), BF16), sd((bsz, KB_W, s_tot), BF16),
                   sd((bsz, A_KV, V_ROWS, s_tot), BF16),
                   sd((bsz, QN_ROWS, s_tot), F32), sd((bsz, KN_ROWS, s_tot), F32)],
        compiler_params=_params(("parallel", "parallel")),
    )(xa, mod, w_t, cos_t, sin_t, q_gain, k_gain)


def _padded_q(q_ref, h, g):
    blk = q_ref[h * HEAD_DIM:(h + 1) * HEAD_DIM, :]
    zeros = jnp.zeros_like(blk)
    return jnp.concatenate([blk, zeros] if g == 0 else [zeros, blk], axis=0)


def _attn_a_kernel(fast_ref, ksc_ref, sink_ref, q_ref, qn_ref, k_ref, v_ref, o_ref, ot_ref, *, n_lat, n_ctx):
    tq = q_ref.shape[1]
    span = tq + 2 * WINDOW
    start = pl.program_id(1) * tq
    ks = pl.multiple_of(jnp.clip(start - WINDOW, 0, n_lat - span), 128)
    kpos = ks + lax.broadcasted_iota(jnp.int32, (span, tq), 0)
    qpos = start + lax.broadcasted_iota(jnp.int32, (span, tq), 1)
    ok = (jnp.abs(qpos - kpos) <= WINDOW) & (qpos < n_lat)
    k_loc = k_ref[pl.ds(ks, span), :]
    k_ctx = k_ref[n_lat:n_lat + n_ctx, :]
    first_row = lax.broadcasted_iota(jnp.int32, (KA_W, tq), 0) == 0

    def head(h, g, bounded):
        v_loc = v_ref[g, :, pl.ds(ks, span)]
        v_ctx = v_ref[g, :, n_lat:n_lat + n_ctx]
        sink = sink_ref[h]
        if bounded:
            offset = jnp.sqrt(qn_ref[QN_A + h:QN_A + h + 1, :]) * ksc_ref[pl.program_id(0) * A_KV + g]
            qp = jnp.concatenate([_padded_q(q_ref, h, g), jnp.where(first_row, -offset, 0.0).astype(BF16)], axis=0)
        else:
            qp = jnp.concatenate([_padded_q(q_ref, h, g), jnp.zeros((KA_W, tq), BF16)], axis=0)
        s_loc = jnp.where(ok, jnp.dot(k_loc, qp, preferred_element_type=F32), NEG_INF)
        s_ctx = jnp.dot(k_ctx, qp, preferred_element_type=F32)
        if bounded:
            m = offset
        else:
            m = jnp.maximum(jnp.maximum(jnp.max(s_loc, axis=0, keepdims=True),
                                        jnp.max(s_ctx, axis=0, keepdims=True)), sink)
            s_loc, s_ctx = s_loc - m, s_ctx - m
        acc = (jnp.dot(v_loc, jnp.exp2(s_loc).astype(BF16), preferred_element_type=F32)
               + jnp.dot(v_ctx, jnp.exp2(s_ctx).astype(BF16), preferred_element_type=F32))
        denom = acc[HEAD_DIM:HEAD_DIM + 1, :] + jnp.exp2(sink - m)
        ot_ref[h * HEAD_DIM:(h + 1) * HEAD_DIM, :] = acc[0:HEAD_DIM, :] / denom

    def bounded_group(g):
        heads = [g * GQA_GROUP + hh for hh in range(GQA_GROUP)]
        k_scale = ksc_ref[pl.program_id(0) * A_KV + g]
        offsets = [jnp.sqrt(qn_ref[QN_A + h:QN_A + h + 1, :]) * k_scale for h in heads]
        qp = jnp.concatenate(
            [jnp.concatenate([_padded_q(q_ref, h, g), jnp.where(first_row, -off, 0.0).astype(BF16)], axis=0)
             for h, off in zip(heads, offsets)], axis=1)
        ok3 = jnp.concatenate([ok] * GQA_GROUP, axis=1)
        p_loc = jnp.exp2(jnp.where(ok3, jnp.dot(k_loc, qp, preferred_element_type=F32), NEG_INF)).astype(BF16)
        p_ctx = jnp.exp2(jnp.dot(k_ctx, qp, preferred_element_type=F32)).astype(BF16)
        acc = (jnp.dot(v_ref[g, :, pl.ds(ks, span)], p_loc, preferred_element_type=F32)
               + jnp.dot(v_ref[g, :, n_lat:n_lat + n_ctx], p_ctx, preferred_element_type=F32))
        for hh, (h, off) in enumerate(zip(heads, offsets)):
            lanes = slice(hh * tq, (hh + 1) * tq)
            denom = acc[HEAD_DIM:HEAD_DIM + 1, lanes] + jnp.exp2(sink_ref[h] - off)
            ot_ref[h * HEAD_DIM:(h + 1) * HEAD_DIM, :] = acc[0:HEAD_DIM, lanes] / denom

    @pl.when(fast_ref[0] == 1)
    def _():
        for g in range(A_KV):
            bounded_group(g)

    @pl.when(fast_ref[0] != 1)
    def _():
        for g in range(A_KV):
            for hh in range(GQA_GROUP):
                head(g * GQA_GROUP + hh, g, False)

    o_ref[...] = ot_ref[...].T.astype(BF16)


def _attn_a(sink, q_t, q_norm2, k, k_norm2, v_t, n_lat):
    bsz, s_tot, _ = k.shape
    tq = TQ_A
    sink2 = sink.astype(F32) * LOG2_E
    k_scale = jnp.sqrt(jnp.max(k_norm2[:, KN_A:KN_A + A_KV, :], axis=-1)) * BOUND_SLACK
    bound = jnp.sqrt(jnp.max(q_norm2[:, QN_A:QN_A + A_HEADS, :])) * jnp.max(k_scale)
    fast = ((bound <= MAX_SCORE_BOUND) & (jnp.max(jnp.abs(sink2)) <= MAX_SCORE_BOUND)).astype(jnp.int32).reshape(1)
    kern = functools.partial(_attn_a_kernel, n_lat=n_lat, n_ctx=s_tot - n_lat)
    smem = pl.BlockSpec(memory_space=pltpu.SMEM)
    return pl.pallas_call(
        kern,
        grid=(bsz, s_tot // tq),
        in_specs=[smem, smem, smem,
                  pl.BlockSpec((None, QA_W, tq), lambda b, i: (b, 0, i)),
                  pl.BlockSpec((None, QN_ROWS, tq), lambda b, i: (b, 0, i)),
                  pl.BlockSpec((None, s_tot, 2 * KA_W), lambda b, i: (b, 0, 0)),
                  pl.BlockSpec((None, A_KV, V_ROWS, s_tot), lambda b, i: (b, 0, 0, 0))],
        out_specs=pl.BlockSpec((None, tq, QA_W), lambda b, i: (b, i, 0)),
        out_shape=jax.ShapeDtypeStruct((bsz, s_tot, QA_W), BF16),
        scratch_shapes=[pltpu.VMEM((QA_W, tq), F32)],
        compiler_params=_params(("parallel", "arbitrary")),
    )(fast, k_scale.reshape(-1), sink2, q_t, q_norm2, k, v_t)


def _nb_plan(n_lat, s_tot):
    rows = n_lat // GRID_W
    kh = min(NA_ROWS, rows)
    q_rows = TQ_B // GRID_W
    w_rows = kh + q_rows
    n_lat_blocks = n_lat // TQ_B
    n_blocks = s_tot // TQ_B
    starts, variants, keys, reps = [], [], {}, []
    for i in range(n_lat_blocks):
        w0 = int(np.clip(q_rows * i - kh // 2, 0, rows - w_rows))
        r0s = tuple(int(np.clip(q_rows * i + j - kh // 2, 0, rows - kh)) - w0 for j in range(q_rows))
        key = (q_rows * i - w0,) + r0s
        if key not in keys:
            keys[key] = len(reps)
            reps.append((i, w0))
        starts.append(w0 * GRID_W)
        variants.append(keys[key])
    masked = len(reps)
    starts.extend([0] * (n_blocks - n_lat_blocks))
    variants.extend([masked] * (n_blocks - n_lat_blocks))
    return rows, kh, w_rows, reps, np.asarray(starts, np.int32), np.asarray(variants, np.int32)


def _rel_bias_kernel(r_ref, e_ref, m_ref, o_ref):
    o_ref[...] = jnp.dot(r_ref[...], e_ref[...], preferred_element_type=F32,
                         precision=lax.Precision.HIGHEST) + m_ref[...]


def _rel_bias_blocks(rpb):
    depth, heads, n_dr, n_dc = rpb.shape
    kc = np.arange(GRID_W)[:, None]
    cq = np.arange(GRID_W)[None, :]
    dc = (kc - cq + NA_COLS - 1).reshape(-1)
    n_dc_pad = 32
    onehot = (np.arange(n_dc_pad)[:, None] == dc[None, :]).astype(np.float32)
    c0 = np.clip(cq - NA_COLS // 2, 0, GRID_W - NA_COLS)
    outside = ~((kc >= c0) & (kc < c0 + NA_COLS))
    col_mask = np.where(outside, NEG_INF, 0.0).astype(np.float32).reshape(1, -1)
    table = jnp.pad((rpb.astype(F32) * LOG2_E).reshape(depth * heads * n_dr, n_dc), ((0, 0), (0, n_dc_pad - n_dc)))
    blocks = pl.pallas_call(
        _rel_bias_kernel,
        out_shape=jax.ShapeDtypeStruct((depth * heads * n_dr, GRID_W * GRID_W), F32),
    )(table, jnp.asarray(onehot), jnp.asarray(col_mask))
    blocks = blocks.reshape(depth, heads, n_dr, GRID_W, GRID_W)
    return jnp.concatenate([blocks, jnp.full((depth, heads, 1, GRID_W, GRID_W), NEG_INF, F32)], axis=2)


def _nb_bias_tiles(rpb, n_lat, s_tot):
    rows, kh, w_rows, reps, starts, variants = _nb_plan(n_lat, s_tot)
    q_rows = TQ_B // GRID_W
    masked_block = 2 * NA_ROWS - 1
    dr_map = np.full((len(reps) + 1, w_rows, q_rows), masked_block, np.int32)
    for v, (i, w0) in enumerate(reps):
        for a in range(w_rows):
            for b in range(q_rows):
                r = q_rows * i + b
                r0 = int(np.clip(r - kh // 2, 0, rows - kh))
                if r0 <= w0 + a < r0 + kh:
                    dr_map[v, a, b] = w0 + a - r + NA_ROWS - 1
    blocks = _rel_bias_blocks(rpb)[:, :, dr_map]
    depth = rpb.shape[0]
    tiles = jnp.transpose(blocks, (0, 2, 3, 5, 1, 4, 6)).reshape(
        depth, len(reps) + 1, w_rows * GRID_W, B_HEADS * TQ_B)
    return tiles, jnp.asarray(starts), jnp.asarray(variants), w_rows * GRID_W


def _attn_b_kernel(start_ref, var_ref, fast_ref, ksc_ref, q_ref, qn_ref, k_ref, v_ref, bias_ref, o_ref,
                   qp_ref, ot_ref, m_ref, *, n_lat, n_ctx, n_keys):
    tq = q_ref.shape[1]
    ws = pl.multiple_of(start_ref[pl.program_id(1)], 128)
    qp_ref[...] = jnp.zeros_like(qp_ref)
    for h in range(B_HEADS):
        qp_ref[h * HEAD_DIM:(h + 1) * HEAD_DIM, h * tq:(h + 1) * tq] = q_ref[h * HEAD_DIM:(h + 1) * HEAD_DIM, :]
    qp = qp_ref[...]
    s_loc = jnp.dot(k_ref[pl.ds(ws, n_keys), :], qp, preferred_element_type=F32) + bias_ref[...]
    s_ctx = jnp.dot(k_ref[n_lat:n_lat + n_ctx, :], qp, preferred_element_type=F32)
    for bounded in (True, False):
        @pl.when((fast_ref[0] == 1) == bounded)
        def _():
            if bounded:
                m_ref[...] = jnp.concatenate(
                    [jnp.sqrt(qn_ref[QN_B + h:QN_B + h + 1, :]) * ksc_ref[pl.program_id(0) * B_HEADS + h]
                     for h in range(B_HEADS)], axis=1)
            else:
                m_ref[...] = jnp.maximum(jnp.max(s_loc, axis=0, keepdims=True), jnp.max(s_ctx, axis=0, keepdims=True))
    m = m_ref[...]
    p_loc = jnp.exp2(s_loc - m)
    p_ctx = jnp.exp2(s_ctx - m)
    denom = jnp.sum(p_loc, axis=0, keepdims=True) + jnp.sum(p_ctx, axis=0, keepdims=True)
    acc = (jnp.dot(v_ref[:, pl.ds(ws, n_keys)], p_loc.astype(BF16), preferred_element_type=F32)
           + jnp.dot(v_ref[:, n_lat:n_lat + n_ctx], p_ctx.astype(BF16), preferred_element_type=F32))
    for h in range(B_HEADS):
        ot_ref[h * HEAD_DIM:(h + 1) * HEAD_DIM, :] = (
            acc[h * HEAD_DIM:(h + 1) * HEAD_DIM, h * tq:(h + 1) * tq] / denom[:, h * tq:(h + 1) * tq])
    o_ref[...] = ot_ref[...].T.astype(BF16)


def _attn_b(q_t, q_norm2, k, k_norm2, v_t, bias_tiles, max_bias, starts, variants, n_keys, n_lat):
    bsz, s_tot, _ = k.shape
    tq = TQ_B
    k_scale = jnp.sqrt(jnp.max(k_norm2[:, KN_B:KN_B + B_HEADS, :], axis=-1))
    bound = jnp.sqrt(jnp.max(q_norm2[:, QN_B:QN_B + B_HEADS, :])) * jnp.max(k_scale) + max_bias
    fast = (bound <= MAX_SCORE_BOUND).astype(jnp.int32).reshape(1)
    kern = functools.partial(_attn_b_kernel, n_lat=n_lat, n_ctx=s_tot - n_lat, n_keys=n_keys)
    smem = pl.BlockSpec(memory_space=pltpu.SMEM)
    grid_spec = pltpu.PrefetchScalarGridSpec(
        num_scalar_prefetch=2,
        grid=(bsz, s_tot // tq),
        in_specs=[smem, smem,
                  pl.BlockSpec((None, QB_W, tq), lambda b, i, st, va: (b, 0, i)),
                  pl.BlockSpec((None, QN_ROWS, tq), lambda b, i, st, va: (b, 0, i)),
                  pl.BlockSpec((None, s_tot, KB_W), lambda b, i, st, va: (b, 0, 0)),
                  pl.BlockSpec((None, KB_W, s_tot), lambda b, i, st, va: (b, 0, 0)),
                  pl.BlockSpec((None, n_keys, B_HEADS * tq), lambda b, i, st, va: (va[i], 0, 0))],
        out_specs=pl.BlockSpec((None, tq, QB_W), lambda b, i, st, va: (b, i, 0)),
        scratch_shapes=[pltpu.VMEM((QB_W, B_HEADS * tq), BF16), pltpu.VMEM((QB_W, tq), F32),
                        pltpu.VMEM((1, B_HEADS * tq), F32)],
    )
    return pl.pallas_call(
        kern,
        grid_spec=grid_spec,
        out_shape=jax.ShapeDtypeStruct((bsz, s_tot, QB_W), BF16),
        compiler_params=_params(("parallel", "arbitrary")),
    )(starts, variants, fast, k_scale.reshape(-1), q_t, q_norm2, k, v_t, bias_tiles)


def _attn_c_kernel(fast_ref, ksc_ref, q_ref, qn_ref, k_ref, v_ref, o_ref, qp_ref, m_ref, acc_ref, ot_ref,
                   *, n_lat, n_ctx, tk):
    tq = q_ref.shape[1]
    is_ctx = pl.program_id(1) >= n_lat // tq
    n_chunks = jnp.where(is_ctx, 0, n_lat // tk - 1)
    fast = fast_ref[0] == 1
    first_row = lax.broadcasted_iota(jnp.int32, (KC_W, tq), 0) == 0
    for g in range(C_KV):
        k_scale = ksc_ref[pl.program_id(0) * C_KV + g]
        for hh in range(GQA_GROUP):
            h = g * GQA_GROUP + hh
            qp_ref[g, 0:KC_W, hh * tq:(hh + 1) * tq] = _padded_q(q_ref, h, g)
            offset = jnp.where(fast, -jnp.sqrt(qn_ref[h:h + 1, :]) * k_scale, 0.0)
            qp_ref[g, KC_W:2 * KC_W, hh * tq:(hh + 1) * tq] = jnp.where(first_row, offset, 0.0).astype(BF16)
    acc_ref[...] = jnp.zeros_like(acc_ref)

    def bounded_step(g, start, size):
        s = jnp.dot(k_ref[pl.ds(start, size), :], qp_ref[g], preferred_element_type=F32)
        p = jnp.exp2(s).astype(BF16)
        acc_ref[g] += jnp.dot(v_ref[g, :, pl.ds(start, size)], p, preferred_element_type=F32)

    def online_step(g, start, size):
        s = jnp.dot(k_ref[pl.ds(start, size), :], qp_ref[g], preferred_element_type=F32)
        m_prev = m_ref[g]
        m_new = jnp.maximum(m_prev, jnp.max(s, axis=0, keepdims=True))
        p = jnp.exp2(s - m_new).astype(BF16)
        pv = jnp.dot(v_ref[g, :, pl.ds(start, size)], p, preferred_element_type=F32)
        acc_ref[g] = acc_ref[g] * jnp.exp2(m_prev - m_new) + pv
        m_ref[g] = m_new

    def sweep(step):
        def body(c, carry):
            for g in range(C_KV):
                step(g, pl.multiple_of(c * tk, tk), tk)
            return carry

        lax.fori_loop(0, n_chunks, body, 0)

        @pl.when(is_ctx)
        def _():
            for g in range(C_KV):
                step(g, n_lat, n_ctx)

        @pl.when(jnp.logical_not(is_ctx))
        def _():
            for g in range(C_KV):
                step(g, n_lat - tk, tk + n_ctx)

    @pl.when(fast)
    def _():
        sweep(bounded_step)

    @pl.when(jnp.logical_not(fast))
    def _():
        m_ref[...] = jnp.full_like(m_ref, NEG_INF)
        sweep(online_step)

    for g in range(C_KV):
        acc = acc_ref[g]
        out = acc[0:HEAD_DIM, :] / acc[HEAD_DIM:HEAD_DIM + 1, :]
        for hh in range(GQA_GROUP):
            h = g * GQA_GROUP + hh
            ot_ref[h * HEAD_DIM:(h + 1) * HEAD_DIM, :] = out[:, hh * tq:(hh + 1) * tq]
    o_ref[...] = ot_ref[...].T.astype(BF16)


def _attn_c(q_t, q_norm2, k, k_norm2, v_t, n_lat):
    bsz, s_tot, _ = k.shape
    tq = TQ_C
    tk = min(TK_C, n_lat)
    k_scale = jnp.sqrt(jnp.max(k_norm2[:, KN_C:KN_C + C_KV, :], axis=-1)) * BOUND_SLACK
    bound = jnp.sqrt(jnp.max(q_norm2[:, QN_C:QN_C + C_HEADS, :])) * jnp.max(k_scale)
    fast = (bound <= MAX_SCORE_BOUND).astype(jnp.int32).reshape(1)
    kern = functools.partial(_attn_c_kernel, n_lat=n_lat, n_ctx=s_tot - n_lat, tk=tk)
    smem = pl.BlockSpec(memory_space=pltpu.SMEM)
    return pl.pallas_call(
        kern,
        grid=(bsz, s_tot // tq),
        in_specs=[smem, smem,
                  pl.BlockSpec((None, QC_W, tq), lambda b, i: (b, 0, i)),
                  pl.BlockSpec((None, QN_ROWS, tq), lambda b, i: (b, 0, i)),
                  pl.BlockSpec((None, s_tot, 2 * KC_W), lambda b, i: (b, 0, 0)),
                  pl.BlockSpec((None, C_KV, V_ROWS, s_tot), lambda b, i: (b, 0, 0, 0))],
        out_specs=pl.BlockSpec((None, tq, QC_W), lambda b, i: (b, i, 0)),
        out_shape=jax.ShapeDtypeStruct((bsz, s_tot, QC_W), BF16),
        scratch_shapes=[pltpu.VMEM((C_KV, 2 * KC_W, GQA_GROUP * tq), BF16),
                        pltpu.VMEM((C_KV, 1, GQA_GROUP * tq), F32),
                        pltpu.VMEM((C_KV, V_ROWS, GQA_GROUP * tq), F32),
                        pltpu.VMEM((QC_W, tq), F32)],
        compiler_params=_params(("parallel", "arbitrary")),
    )(fast, k_scale.reshape(-1), q_t, q_norm2, k, v_t)


def _store_token_tiles(ref, val, pitch=TILE_ROWS, first=0):
    n = val.shape[0]
    for c in range(TILE_ROWS):
        ref[pl.ds(first + c, n, stride=pitch), :] = val[:, c * 128:(c + 1) * 128]


def _load_token_tiles(ref, n, pitch=TILE_ROWS, first=0):
    return jnp.concatenate([ref[pl.ds(first + c, n, stride=pitch), :] for c in range(TILE_ROWS)], axis=1)


def _tile_copy(src_hbm, dst, idx_ref, j, sem, pitch):
    src = pl.multiple_of(idx_ref[0, 0, j] * pitch, pitch)
    return pltpu.make_async_copy(src_hbm.at[pl.ds(src, pitch), :],
                                 dst.at[pl.ds(pl.multiple_of(j * pitch, pitch), pitch), :], sem)


def _start_gather(src_hbm, dst, idx_ref, n, sem, pitch=TILE_ROWS):
    def issue(i, carry):
        for u in range(GATHER_UNROLL):
            _tile_copy(src_hbm, dst, idx_ref, i * GATHER_UNROLL + u, sem, pitch).start()
        return carry

    lax.fori_loop(0, n // GATHER_UNROLL, issue, 0)


def _wait_gather(src_hbm, dst, n, sem, pitch=TILE_ROWS):
    pltpu.make_async_copy(src_hbm.at[pl.ds(0, n * pitch), :], dst, sem).wait()


def _layer_norm(h, g, b):
    mu = jnp.mean(h, axis=-1, keepdims=True)
    hc = h - mu
    var = jnp.mean(hc * hc, axis=-1, keepdims=True)
    return hc * lax.rsqrt(var + LN_EPS) * g + b


def _post_kernel(oa_ref, ob_ref, oc_ref, x_ref, mod_ref, w_ref, g_ref, b_ref, wr_ref, rb_ref,
                 x1_ref, u2_ref, eid_ref, gate_ref, o_scr, ub_scr, *, alpha):
    tm = x_ref.shape[0]
    o_cat = jnp.concatenate([oa_ref[...], ob_ref[...], oc_ref[...]], axis=1)
    o_scr[...] = jnp.dot(o_cat, w_ref[...], preferred_element_type=F32)
    gate1 = 1.0 + mod_ref[2:3, :]
    scale2 = 1.0 + mod_ref[4:5, :]
    shift2 = mod_ref[3:4, :]
    ln_g = g_ref[...]
    ln_b = b_ref[...]

    def row_group(r, carry):
        r0 = pl.multiple_of(r * LN_ROWS, LN_ROWS)
        rows = pl.ds(r0, LN_ROWS)
        x1 = _layer_norm(alpha * x_ref[rows, :] + gate1 * o_scr[rows, :], ln_g, ln_b)
        x1_ref[rows, :] = x1
        u2 = x1 * scale2 + shift2
        ub_scr[rows, :] = u2.astype(BF16)
        for c in range(TILE_ROWS):
            u2_ref[pl.ds(r0 * TILE_ROWS + c, LN_ROWS, stride=TILE_ROWS), :] = u2[:, c * 128:(c + 1) * 128]
        return carry

    lax.fori_loop(0, tm // LN_ROWS, row_group, 0, unroll=LN_UNROLL)
    logits = lax.dot_general(wr_ref[...], ub_scr[...], (((1,), (1,)), ((), ())), preferred_element_type=F32)
    aff_all = jax.nn.sigmoid(logits)
    sel_all = aff_all + rb_ref[...]
    aff = [aff_all[e:e + 1, :] for e in range(N_EXPERTS)]
    sel = [sel_all[e:e + 1, :] for e in range(N_EXPERTS)]
    gsum = []
    for g in range(N_EXPERT_GROUPS):
        a, b, c, d = sel[g * PER_GROUP:(g + 1) * PER_GROUP]
        hi1, lo1, hi2, lo2 = jnp.maximum(a, b), jnp.minimum(a, b), jnp.maximum(c, d), jnp.minimum(c, d)
        gsum.append(jnp.maximum(hi1, hi2) + jnp.maximum(jnp.minimum(hi1, hi2), jnp.maximum(lo1, lo2)))
    best_g = jnp.zeros_like(gsum[0], dtype=jnp.int32)
    best_v = gsum[0]
    for g in range(1, N_EXPERT_GROUPS):
        better = gsum[g] > best_v
        best_g = jnp.where(better, g, best_g)
        best_v = jnp.where(better, gsum[g], best_v)
    s_loc, a_loc = [], []
    for j in range(PER_GROUP):
        sj, aj = sel[j], aff[j]
        for g in range(1, N_EXPERT_GROUPS):
            pick = best_g == g
            sj = jnp.where(pick, sel[g * PER_GROUP + j], sj)
            aj = jnp.where(pick, aff[g * PER_GROUP + j], aj)
        s_loc.append(sj)
        a_loc.append(aj)
    i1 = jnp.zeros_like(best_g)
    v1, g1 = s_loc[0], a_loc[0]
    for j in range(1, PER_GROUP):
        better = s_loc[j] > v1
        i1 = jnp.where(better, j, i1)
        v1 = jnp.where(better, s_loc[j], v1)
        g1 = jnp.where(better, a_loc[j], g1)
    i2 = jnp.full_like(best_g, -1)
    v2 = jnp.full_like(v1, -jnp.inf)
    g2 = jnp.zeros_like(g1)
    for j in range(PER_GROUP):
        better = (i1 != j) & ((s_loc[j] > v2) | (i2 < 0))
        i2 = jnp.where(better, j, i2)
        v2 = jnp.where(better, s_loc[j], v2)
        g2 = jnp.where(better, a_loc[j], g2)
    total = g1 + g2
    lo, hi = jnp.minimum(i1, i2), jnp.maximum(i1, i2)
    pair = jnp.where(lo == 0, hi - 1, jnp.where(lo == 1, hi + 1, PAIRS_PER_GROUP - 1))
    first_is_lo = i1 < i2
    g_lo = jnp.where(first_is_lo, g1, g2) / total
    g_hi = jnp.where(first_is_lo, g2, g1) / total
    eid_ref[...] = jnp.concatenate([best_g * PAIRS_PER_GROUP + pair,
                                    jnp.zeros((MOD_ROWS - 1, tm), jnp.int32)], axis=0)
    gate_ref[...] = jnp.concatenate([g_lo, g_hi, jnp.zeros((MOD_ROWS - TOP_K, tm), F32)], axis=0)


def _post(oa, ob, oc, xa, mod, w_out, ln_g, ln_b, wr_t, rb, n_lat, alpha):
    bsz, s_tot, d = xa.shape
    tm = TOK_TILE
    n_lat_tiles = n_lat // tm
    n_tiles = s_tot // tm
    sd = jax.ShapeDtypeStruct

    def tmajor(width):
        return pl.BlockSpec((None, tm, width), lambda b, i: (b, i, 0))

    def const(shape):
        return pl.BlockSpec(shape, lambda b, i: (0,) * len(shape))

    lanes = pl.BlockSpec((MOD_ROWS, tm), lambda b, i: (0, b * n_tiles + i))
    return pl.pallas_call(
        functools.partial(_post_kernel, alpha=alpha),
        grid=(bsz, n_tiles),
        in_specs=[tmajor(QA_W), tmajor(QB_W), tmajor(QC_W), tmajor(d),
                  pl.BlockSpec((None, N_MOD, d), lambda b, i: (jnp.where(i >= n_lat_tiles, bsz, b), 0, 0)),
                  const((d, d)), const((1, d)), const((1, d)), const((N_EXPERTS, d)), const((N_EXPERTS, 1))],
        out_specs=[tmajor(d), pl.BlockSpec((tm * TILE_ROWS, 128), lambda b, i: (b * n_tiles + i, 0)), lanes, lanes],
        out_shape=[sd((bsz, s_tot, d), F32), sd((bsz * s_tot * TILE_ROWS, 128), F32),
                   sd((MOD_ROWS, bsz * s_tot), jnp.int32), sd((MOD_ROWS, bsz * s_tot), F32)],
        scratch_shapes=[pltpu.VMEM((tm, d), F32), pltpu.VMEM((tm, d), BF16)],
        compiler_params=_params(("parallel", "parallel")),
    )(oa, ob, oc, xa, mod, w_out, ln_g, ln_b, wr_t, rb)


def _ffn_kernel(elo_ref, ehi_ref, nused_ref, tok_ref, tok_next_ref, u_hbm,
                w1lo_ref, w3lo_ref, w2lo_ref, w1hi_ref, w3hi_ref, w2hi_ref, y_ref,
                xbuf, w1lo, w3lo, w2lo, w1hi, w3hi, w2hi, sem):
    blk = y_ref.shape[0] // (TOP_K * TILE_ROWS)
    step = pl.program_id(0)
    n_used = nused_ref[0]
    slot = step % 2
    prev = jnp.maximum(step - 1, 0)

    @pl.when(step < n_used)
    def _():
        @pl.when(step == 0)
        def _():
            _start_gather(u_hbm, xbuf.at[0], tok_ref, blk, sem.at[0])

        @pl.when(step + 1 < n_used)
        def _():
            _start_gather(u_hbm, xbuf.at[1 - slot], tok_next_ref, blk, sem.at[1 - slot])

        @pl.when((step == 0) | (elo_ref[step] != elo_ref[prev]))
        def _():
            w1lo[...] = w1lo_ref[...].astype(BF16)
            w3lo[...] = w3lo_ref[...].astype(BF16)
            w2lo[...] = w2lo_ref[...].astype(BF16)

        @pl.when((step == 0) | (ehi_ref[step] != ehi_ref[prev]))
        def _():
            w1hi[...] = w1hi_ref[...].astype(BF16)
            w3hi[...] = w3hi_ref[...].astype(BF16)
            w2hi[...] = w2hi_ref[...].astype(BF16)

        _wait_gather(u_hbm, xbuf.at[slot], blk, sem.at[slot])
        xb = _load_token_tiles(xbuf.at[slot], blk).astype(BF16)
        for first, (w1b, w3b, w2b) in ((0, (w1lo, w3lo, w2lo)), (TILE_ROWS, (w1hi, w3hi, w2hi))):
            h1 = jnp.dot(xb, w1b[...], preferred_element_type=F32)
            h3 = jnp.dot(xb, w3b[...], preferred_element_type=F32)
            hid = (h1 * jax.nn.sigmoid(h1) * h3).astype(BF16)
            _store_token_tiles(y_ref, jnp.dot(hid, w2b[...], preferred_element_type=F32),
                               pitch=TOP_K * TILE_ROWS, first=first)

    @pl.when(step >= n_used)
    def _():
        y_ref[...] = jnp.zeros_like(y_ref)


def _expert_ffn(block_lo, block_hi, n_used, slot_tok, u2_tiles, w1, w3, w2, layer):
    d = w1.shape[2]
    n_blocks = block_lo.shape[0]
    blk = CLASS_BLOCK
    slot_tok = slot_tok.reshape(n_blocks, 1, blk)

    def weight(shape, which):
        return pl.BlockSpec((None, None) + shape, lambda i, lo, hi, nu: (layer, (lo, hi)[which][i], 0, 0))

    up, down = (d, D_EXPERT), (D_EXPERT, d)
    grid_spec = pltpu.PrefetchScalarGridSpec(
        num_scalar_prefetch=3,
        grid=(n_blocks,),
        in_specs=[pl.BlockSpec((1, 1, blk), lambda i, lo, hi, nu: (i, 0, 0), memory_space=pltpu.SMEM),
                  pl.BlockSpec((1, 1, blk), lambda i, lo, hi, nu: (jnp.minimum(i + 1, n_blocks - 1), 0, 0),
                               memory_space=pltpu.SMEM),
                  pl.BlockSpec(memory_space=pl.ANY),
                  weight(up, 0), weight(up, 0), weight(down, 0), weight(up, 1), weight(up, 1), weight(down, 1)],
        out_specs=pl.BlockSpec((blk * TOP_K * TILE_ROWS, 128), lambda i, lo, hi, nu: (i, 0)),
        scratch_shapes=[pltpu.VMEM((2, blk * TILE_ROWS, 128), F32)]
        + [pltpu.VMEM(s, BF16) for s in (up, up, down, up, up, down)] + [pltpu.SemaphoreType.DMA((2,))],
    )
    return pl.pallas_call(
        _ffn_kernel,
        grid_spec=grid_spec,
        out_shape=jax.ShapeDtypeStruct((n_blocks * blk * TOP_K * TILE_ROWS, 128), F32),
        compiler_params=_params(("arbitrary",)),
    )(block_lo, block_hi, n_used, slot_tok, slot_tok, u2_tiles, w1, w3, w2, w1, w3, w2)


def _comb_kernel(pos_ref, pos_next_ref, y_hbm, x1_ref, gate_ref, mod_ref, g_ref, b_ref, o_ref, ybuf, sem,
                 *, alpha, latent_tiles):
    tm = x1_ref.shape[0]
    pitch = TOP_K * TILE_ROWS
    step = pl.program_id(0)
    slot = step % 2

    @pl.when(step == 0)
    def _():
        _start_gather(y_hbm, ybuf.at[0], pos_ref, tm, sem.at[0], pitch)

    @pl.when(step + 1 < pl.num_programs(0))
    def _():
        _start_gather(y_hbm, ybuf.at[1 - slot], pos_next_ref, tm, sem.at[1 - slot], pitch)

    _wait_gather(y_hbm, ybuf.at[slot], tm, sem.at[slot], pitch)
    y_lo = _load_token_tiles(ybuf.at[slot], tm, pitch, 0)
    y_hi = _load_token_tiles(ybuf.at[slot], tm, pitch, TILE_ROWS)
    f = gate_ref[:, 0:1] * y_lo + gate_ref[:, 1:2] * y_hi

    def finish():
        o_ref[...] = _layer_norm(alpha * x1_ref[...] + (1.0 + mod_ref[5:6, :]) * f, g_ref[...], b_ref[...])

    if latent_tiles is None:
        finish()
    else:
        n_tiles, n_lat_tiles = latent_tiles
        pl.when(step % n_tiles < n_lat_tiles)(finish)


def _combine(pos, y_tiles, x1, gates, mod, ln_g, ln_b, n_lat, alpha, latent_only):
    bsz, s_tot, d = x1.shape
    tm = TOK_TILE
    n_lat_tiles = n_lat // tm
    n_tiles = s_tot // tm
    n_steps = bsz * n_tiles
    out_tokens = n_lat if latent_only else s_tot

    def mod_row(t):
        return jnp.where(t % n_tiles >= n_lat_tiles, bsz, t // n_tiles)

    def out_block(t):
        if not latent_only:
            return t
        return (t // n_tiles) * n_lat_tiles + jnp.minimum(t % n_tiles, n_lat_tiles - 1)

    out = pl.pallas_call(
        functools.partial(_comb_kernel, alpha=alpha, latent_tiles=(n_tiles, n_lat_tiles) if latent_only else None),
        grid=(n_steps,),
        in_specs=[pl.BlockSpec((1, 1, tm), lambda t: (t, 0, 0), memory_space=pltpu.SMEM),
                  pl.BlockSpec((1, 1, tm), lambda t: (jnp.minimum(t + 1, n_steps - 1), 0, 0),
                               memory_space=pltpu.SMEM),
                  pl.BlockSpec(memory_space=pl.ANY),
                  pl.BlockSpec((tm, d), lambda t: (t, 0)),
                  pl.BlockSpec((tm, TOP_K), lambda t: (t, 0)),
                  pl.BlockSpec((None, N_MOD, d), lambda t: (mod_row(t), 0, 0)),
                  pl.BlockSpec((1, d), lambda t: (0, 0)),
                  pl.BlockSpec((1, d), lambda t: (0, 0))],
        out_specs=pl.BlockSpec((tm, d), lambda t: (out_block(t), 0)),
        out_shape=jax.ShapeDtypeStruct((bsz * out_tokens, d), F32),
        scratch_shapes=[pltpu.VMEM((2, TOP_K * tm * TILE_ROWS, 128), F32), pltpu.SemaphoreType.DMA((2,))],
        compiler_params=_params(("arbitrary",)),
    )(pos, pos, y_tiles, x1.reshape(bsz * s_tot, d), gates, mod, ln_g, ln_b)
    return out.reshape(bsz, out_tokens, d)


def _deinterleave(n_heads):
    one = np.concatenate([np.arange(0, HEAD_DIM, 2), np.arange(1, HEAD_DIM, 2)])
    return np.concatenate([h * HEAD_DIM + one for h in range(n_heads)])


def _proj_weight(w_in_l):
    sizes = (QA_W, KA_W, KA_W, QB_W, KB_W, KB_W, QC_W, KC_W, KC_W)
    offs = np.concatenate([[0], np.cumsum(sizes)])
    qa, ka, va, qb, kb, vb, qc, kc, vc = [w_in_l[:, offs[i]:offs[i + 1]] for i in range(len(sizes))]
    scale = HEAD_DIM ** -0.5 * LOG2_E
    qa = qa[:, _deinterleave(A_HEADS)] * scale
    ka = ka[:, _deinterleave(A_KV)]
    qb = qb * scale
    qc = qc[:, _deinterleave(C_HEADS)]
    kc = kc[:, _deinterleave(C_KV)]
    return jnp.concatenate([qa, qb, qc, ka, kb, kc, va, vb, vc], axis=1).T.astype(BF16)


def _rope_tables(n_lat, n_ctx):
    t = np.arange(n_lat)
    inv_freq = ROPE_BASE ** (-np.arange(0, HALF, 2, dtype=np.float32) / HALF)
    ang = jnp.concatenate([jnp.asarray(t // GRID_W, F32)[None, :] * jnp.asarray(inv_freq)[:, None],
                           jnp.asarray(t % GRID_W, F32)[None, :] * jnp.asarray(inv_freq)[:, None]], axis=0)
    cos = jnp.concatenate([jnp.cos(ang), jnp.ones((HALF, n_ctx), F32)], axis=1)
    sin = jnp.concatenate([jnp.sin(ang), jnp.zeros((HALF, n_ctx), F32)], axis=1)
    return cos, sin


def _dispatch(cls, n_tok):
    onehot = (cls[:, None] == jnp.arange(N_CLASSES, dtype=jnp.int32)[None, :]).astype(jnp.int32)
    csum = jnp.cumsum(onehot, axis=0)
    counts = csum[-1]
    padded = (counts + CLASS_BLOCK - 1) // CLASS_BLOCK * CLASS_BLOCK
    pad_ends = jnp.cumsum(padded)
    dest = jnp.sum(onehot * (csum - 1 + (pad_ends - padded)[None, :]), axis=1)
    n_blocks = -(-n_tok // CLASS_BLOCK) + N_CLASSES
    slot_tok = jnp.zeros((n_blocks * CLASS_BLOCK,), jnp.int32).at[dest].set(jnp.arange(n_tok, dtype=jnp.int32))
    block_start = jnp.arange(n_blocks, dtype=jnp.int32) * CLASS_BLOCK
    block_cls = jnp.minimum(jnp.sum((block_start[:, None] >= pad_ends[None, :]).astype(jnp.int32), axis=1),
                            N_CLASSES - 1)
    group, pair = block_cls // PAIRS_PER_GROUP, block_cls % PAIRS_PER_GROUP
    lo = (pair >= 3).astype(jnp.int32) + (pair >= 5).astype(jnp.int32)
    hi = jnp.where(pair < 3, pair + 1, jnp.where(pair < 5, pair - 1, 3))
    n_used = (pad_ends[-1] // CLASS_BLOCK).astype(jnp.int32).reshape(1)
    return slot_tok, group * PER_GROUP + lo, group * PER_GROUP + hi, dest.astype(jnp.int32), n_used


def kernel(x, c, ctx, c_ctx, w_in, w_out, sink, rpb, q_gain, k_gain, w_ada, b_ada,
           ln1_g, ln1_b, ln2_g, ln2_b, w_router, router_bias, w1, w3, w2):
    bsz, n_lat, d = x.shape
    n_ctx = ctx.shape[1]
    s_tot = n_lat + n_ctx
    depth = w_in.shape[0]
    assert d == D_MODEL and n_ctx == TOK_TILE and n_lat % min(TK_C, n_lat) == 0
    assert n_lat % (2 * TOK_TILE) == 0 and bsz + 1 <= MOD_ROWS
    alpha = float((2 * depth) ** 0.25)
    n_tok = bsz * s_tot
    tm = TOK_TILE

    cond = jnp.zeros((MOD_ROWS, d), F32).at[:bsz].set(c).at[bsz].set(c_ctx)
    mods = _ada_table(cond, w_ada, b_ada).reshape(depth, MOD_ROWS, N_MOD, d)
    cos_t, sin_t = _rope_tables(n_lat, n_ctx)
    gain_perm = _deinterleave(1)
    wr_t = w_router.T.astype(BF16)
    rb = router_bias.reshape(N_EXPERTS, 1).astype(F32)

    bias_tiles, starts, variants, n_keys = _nb_bias_tiles(rpb, n_lat, s_tot)

    xa = jnp.concatenate([x, ctx], axis=1)
    for l in range(depth):
        mod = mods[l]
        qa, qb, qc, ka, kb, kc, va, vb, vc, qn, kn = _project(
            xa, mod, _proj_weight(w_in[l]), cos_t, sin_t,
            q_gain[l][gain_perm].reshape(HEAD_DIM, 1) * (HEAD_DIM ** -0.5 * LOG2_E),
            k_gain[l][gain_perm].reshape(HEAD_DIM, 1), n_lat)
        oa = _attn_a(sink[l], qa, qn, ka, kn, va, n_lat)
        ob = _attn_b(qb, qn, kb, kn, vb, bias_tiles[l], jnp.max(jnp.abs(rpb[l])) * LOG2_E,
                     starts, variants, n_keys, n_lat)
        oc = _attn_c(qc, qn, kc, kn, vc, n_lat)
        x1, u2, eid8, gate8 = _post(oa, ob, oc, xa, mod, w_out[l].astype(BF16),
                                    ln1_g[l].reshape(1, d), ln1_b[l].reshape(1, d), wr_t, rb, n_lat, alpha)
        slot_tok, block_lo, block_hi, dest, n_used = _dispatch(eid8[0], n_tok)
        y = _expert_ffn(block_lo, block_hi, n_used, slot_tok, u2, w1, w3, w2, l)
        xa = _combine(dest.reshape(n_tok // tm, 1, tm), y, x1, gate8[:TOP_K].T, mod,
                      ln2_g[l].reshape(1, d), ln2_b[l].reshape(1, d), n_lat, alpha, latent_only=l == depth - 1)
    return xa
```

```python
import functools

import numpy as np
import jax
import jax.numpy as jnp
from jax import lax
from jax.experimental import pallas as pl
from jax.experimental.pallas import tpu as pltpu

F32 = jnp.float32
BF16 = jnp.bfloat16

D_MODEL = 1024
HEAD_DIM = 64
HALF = HEAD_DIM // 2
GRID_W = 64
A_HEADS, A_KV = 6, 2
WINDOW = 128
B_HEADS = 4
NA_ROWS, NA_COLS = 8, 16
C_HEADS, C_KV = 6, 2
GQA_GROUP = 3
ROPE_BASE = 10000.0
N_EXPERTS = 16
N_EXPERT_GROUPS = 4
PER_GROUP = N_EXPERTS // N_EXPERT_GROUPS
TOP_K = 2
D_EXPERT = 512
PAIRS_PER_GROUP = PER_GROUP * (PER_GROUP - 1) // 2
N_CLASSES = N_EXPERT_GROUPS * PAIRS_PER_GROUP
CLASS_BLOCK = 256
N_MOD = 6
LN_EPS = 1e-6
QK_EPS = 1e-6
NEG_INF = -1e30

QA_W, QB_W, QC_W = A_HEADS * HEAD_DIM, B_HEADS * HEAD_DIM, C_HEADS * HEAD_DIM
KA_W, KB_W, KC_W = A_KV * HEAD_DIM, B_HEADS * HEAD_DIM, C_KV * HEAD_DIM
R_QA = 0
R_QB = R_QA + QA_W
R_QC = R_QB + QB_W
R_KA = R_QC + QC_W
R_KB = R_KA + KA_W
R_KC = R_KB + KB_W
R_VA = R_KC + KC_W
R_VB = R_VA + KA_W
R_VC = R_VB + KB_W
PROJ_WIDTH = R_VC + KC_W

V7X_VMEM_LIMIT = 56 * 1024 * 1024
TOK_TILE = 256
TQ_A = 256
TQ_B = 2 * GRID_W
TQ_C = 256
TK_C = 2048
MOD_ROWS = 8
LOG2_E = 1.4426950408889634
QN_C, QN_A, QN_B, QN_ROWS = 0, C_HEADS, C_HEADS + A_HEADS, C_HEADS + A_HEADS + B_HEADS
KN_C, KN_A, KN_B, KN_ROWS = 0, C_KV, C_KV + A_KV, C_KV + A_KV + B_HEADS
MAX_SCORE_BOUND = 60.0
BOUND_SLACK = 1.004
TILE_ROWS = 8
BF16_ROWS = 16
GATHER_UNROLL = 8
LN_ROWS = 64
LN_UNROLL = 4
V_ROWS = HEAD_DIM + BF16_ROWS


def _params(sem):
    return pltpu.CompilerParams(dimension_semantics=sem, vmem_limit_bytes=V7X_VMEM_LIMIT)


def _ada_kernel(c_ref, w_ref, b_ref, o_ref):
    c = c_ref[...]
    act = c * jax.nn.sigmoid(c)
    o_ref[...] = jnp.dot(act, w_ref[...], preferred_element_type=F32,
                         precision=lax.Precision.HIGHEST) + b_ref[...]


def _ada_table(cond, w_ada, b_ada):
    depth, d, n = w_ada.shape
    tn = n // 4
    return pl.pallas_call(
        _ada_kernel,
        grid=(depth, n // tn),
        in_specs=[pl.BlockSpec((MOD_ROWS, d), lambda l, j: (0, 0)),
                  pl.BlockSpec((None, d, tn), lambda l, j: (l, 0, j)),
                  pl.BlockSpec((None, 1, tn), lambda l, j: (l, 0, j))],
        out_specs=pl.BlockSpec((None, MOD_ROWS, tn), lambda l, j: (l, 0, j)),
        out_shape=jax.ShapeDtypeStruct((depth, MOD_ROWS, n), F32),
        compiler_params=_params(("parallel", "parallel")),
    )(cond, w_ada, b_ada.reshape(depth, 1, n))


def _sq_norm(parts):
    return sum(jnp.sum(jnp.square(p.astype(F32)), axis=0, keepdims=True) for p in parts)


def _proj_kernel(x_ref, mod_ref, w_ref, cos_ref, sin_ref, qg_ref, kg_ref,
                 qa_ref, qb_ref, qc_ref, ka_ref, kb_ref, kc_ref, va_ref, vb_ref, vc_ref, qn_ref, kn_ref):
    tm = x_ref.shape[0]
    u = (x_ref[...] * (1.0 + mod_ref[1:2, :]) + mod_ref[0:1, :]).astype(BF16)

    def proj(lo, width):
        return lax.dot_general(w_ref[lo:lo + width, :], u, (((1,), (1,)), ((), ())),
                               preferred_element_type=F32)

    cos = cos_ref[...]
    sin = sin_ref[...]

    def rope(blk):
        x0, x1 = blk[:HALF], blk[HALF:]
        return x0 * cos - x1 * sin, x0 * sin + x1 * cos

    def qk_norm(blk, gain):
        ms = jnp.mean(blk * blk, axis=0, keepdims=True)
        return blk * lax.rsqrt(ms + QK_EPS) * gain

    p = proj(R_QA, QA_W)
    for h in range(A_HEADS):
        o0, o1 = [o.astype(BF16) for o in rope(p[h * HEAD_DIM:(h + 1) * HEAD_DIM])]
        qa_ref[h * HEAD_DIM:h * HEAD_DIM + HALF, :] = o0
        qa_ref[h * HEAD_DIM + HALF:(h + 1) * HEAD_DIM, :] = o1
        qn_ref[QN_A + h:QN_A + h + 1, :] = _sq_norm([o0, o1])
    p = proj(R_QB, QB_W).astype(BF16)
    qb_ref[...] = p
    for h in range(B_HEADS):
        qn_ref[QN_B + h:QN_B + h + 1, :] = _sq_norm([p[h * HEAD_DIM:(h + 1) * HEAD_DIM]])
    p = proj(R_QC, QC_W)
    qg = qg_ref[...]
    for h in range(C_HEADS):
        o0, o1 = [o.astype(BF16) for o in rope(qk_norm(p[h * HEAD_DIM:(h + 1) * HEAD_DIM], qg))]
        qc_ref[h * HEAD_DIM:h * HEAD_DIM + HALF, :] = o0
        qc_ref[h * HEAD_DIM + HALF:(h + 1) * HEAD_DIM, :] = o1
        qn_ref[QN_C + h:QN_C + h + 1, :] = _sq_norm([o0, o1])
    one_lane = (lax.broadcasted_iota(jnp.int32, (tm, KA_W), 1) == 0).astype(BF16)
    p = proj(R_KA, KA_W)
    parts = []
    for h in range(A_KV):
        o0, o1 = [o.astype(BF16) for o in rope(p[h * HEAD_DIM:(h + 1) * HEAD_DIM])]
        kn_ref[KN_A + h:KN_A + h + 1, :] = _sq_norm([o0, o1])
        parts.extend([o0.astype(F32), o1.astype(F32)])
    ka_ref[:, 0:KA_W] = jnp.concatenate(parts, axis=0).T.astype(BF16)
    ka_ref[:, KA_W:2 * KA_W] = one_lane
    p = proj(R_KB, KB_W).astype(BF16)
    for h in range(B_HEADS):
        kn_ref[KN_B + h:KN_B + h + 1, :] = _sq_norm([p[h * HEAD_DIM:(h + 1) * HEAD_DIM]])
    kb_ref[...] = p.astype(F32).T.astype(BF16)
    p = proj(R_KC, KC_W)
    kg = kg_ref[...]
    parts = []
    for h in range(C_KV):
        o0, o1 = [o.astype(BF16) for o in rope(qk_norm(p[h * HEAD_DIM:(h + 1) * HEAD_DIM], kg))]
        kn_ref[KN_C + h:KN_C + h + 1, :] = _sq_norm([o0, o1])
        parts.extend([o0.astype(F32), o1.astype(F32)])
    kc_ref[:, 0:KC_W] = jnp.concatenate(parts, axis=0).T.astype(BF16)
    kc_ref[:, KC_W:2 * KC_W] = one_lane
    ones_row = (lax.broadcasted_iota(jnp.int32, (BF16_ROWS, tm), 0) == 0).astype(BF16)
    for lo, ref in ((R_VA, va_ref), (R_VC, vc_ref)):
        p = proj(lo, KA_W)
        for g in range(A_KV):
            ref[g, 0:HEAD_DIM, :] = p[g * HEAD_DIM:(g + 1) * HEAD_DIM].astype(BF16)
            ref[g, HEAD_DIM:V_ROWS, :] = ones_row
    vb_ref[...] = proj(R_VB, KB_W).astype(BF16)


def _project(xa, mod, w_t, cos_t, sin_t, q_gain, k_gain, n_lat):
    bsz, s_tot, d = xa.shape
    tm = TOK_TILE
    n_lat_tiles = n_lat // tm
    sd = jax.ShapeDtypeStruct

    def dmajor(width):
        return pl.BlockSpec((None, width, tm), lambda b, i: (b, 0, i))

    def tmajor(width):
        return pl.BlockSpec((None, tm, width), lambda b, i: (b, i, 0))

    aug = pl.BlockSpec((None, A_KV, V_ROWS,tm), lambda b, i: (b, 0, 0, i))
    return pl.pallas_call(
        _proj_kernel,
        grid=(bsz, s_tot // tm),
        in_specs=[tmajor(d),
                  pl.BlockSpec((None, N_MOD, d), lambda b, i: (jnp.where(i >= n_lat_tiles, bsz, b), 0, 0)),
                  pl.BlockSpec((PROJ_WIDTH, d), lambda b, i: (0, 0)),
                  pl.BlockSpec((HALF, tm), lambda b, i: (0, i)),
                  pl.BlockSpec((HALF, tm), lambda b, i: (0, i)),
                  pl.BlockSpec((HEAD_DIM, 1), lambda b, i: (0, 0)),
                  pl.BlockSpec((HEAD_DIM, 1), lambda b, i: (0, 0))],
        out_specs=[dmajor(QA_W), dmajor(QB_W), dmajor(QC_W),
                   tmajor(2 * KA_W), tmajor(KB_W), tmajor(2 * KC_W),
                   aug, dmajor(KB_W), aug, dmajor(QN_ROWS), dmajor(KN_ROWS)],
        out_shape=[sd((bsz, QA_W, s_tot), BF16), sd((bsz, QB_W, s_tot), BF16), sd((bsz, QC_W, s_tot), BF16),
                   sd((bsz, s_tot, 2 * KA_W), BF16), sd((bsz, s_tot, KB_W), BF16), sd((bsz, s_tot, 2 * KC_W), BF16),
                   sd((bsz, A_KV, V_ROWS, s_tot), BF16), sd((bsz, KB_W, s_tot), BF16),
                   sd((bsz, A_KV, V_ROWS, s_tot), BF16),
                   sd((bsz, QN_ROWS, s_tot), F32), sd((bsz, KN_ROWS, s_tot), F32)],
        compiler_params=_params(("parallel", "parallel")),
    )(xa, mod, w_t, cos_t, sin_t, q_gain, k_gain)


def _padded_q(q_ref, h, g):
    blk = q_ref[h * HEAD_DIM:(h + 1) * HEAD_DIM, :]
    zeros = jnp.zeros_like(blk)
    return jnp.concatenate([blk, zeros] if g == 0 else [zeros, blk], axis=0)


def _attn_a_kernel(fast_ref, ksc_ref, sink_ref, q_ref, qn_ref, k_ref, v_ref, o_ref, ot_ref, *, n_lat, n_ctx):
    tq = q_ref.shape[1]
    span = tq + 2 * WINDOW
    start = pl.program_id(1) * tq
    ks = pl.multiple_of(jnp.clip(start - WINDOW, 0, n_lat - span), 128)
    kpos = ks + lax.broadcasted_iota(jnp.int32, (span, tq), 0)
    qpos = start + lax.broadcasted_iota(jnp.int32, (span, tq), 1)
    ok = (jnp.abs(qpos - kpos) <= WINDOW) & (qpos < n_lat)
    k_loc = k_ref[pl.ds(ks, span), :]
    k_ctx = k_ref[n_lat:n_lat + n_ctx, :]
    first_row = lax.broadcasted_iota(jnp.int32, (KA_W, tq), 0) == 0

    def head(h, g, bounded):
        v_loc = v_ref[g, :, pl.ds(ks, span)]
        v_ctx = v_ref[g, :, n_lat:n_lat + n_ctx]
        sink = sink_ref[h]
        if bounded:
            offset = jnp.sqrt(qn_ref[QN_A + h:QN_A + h + 1, :]) * ksc_ref[pl.program_id(0) * A_KV + g]
            qp = jnp.concatenate([_padded_q(q_ref, h, g), jnp.where(first_row, -offset, 0.0).astype(BF16)], axis=0)
        else:
            qp = jnp.concatenate([_padded_q(q_ref, h, g), jnp.zeros((KA_W, tq), BF16)], axis=0)
        s_loc = jnp.where(ok, jnp.dot(k_loc, qp, preferred_element_type=F32), NEG_INF)
        s_ctx = jnp.dot(k_ctx, qp, preferred_element_type=F32)
        if bounded:
            m = offset
        else:
            m = jnp.maximum(jnp.maximum(jnp.max(s_loc, axis=0, keepdims=True),
                                        jnp.max(s_ctx, axis=0, keepdims=True)), sink)
            s_loc, s_ctx = s_loc - m, s_ctx - m
        acc = (jnp.dot(v_loc, jnp.exp2(s_loc).astype(BF16), preferred_element_type=F32)
               + jnp.dot(v_ctx, jnp.exp2(s_ctx).astype(BF16), preferred_element_type=F32))
        denom = acc[HEAD_DIM:HEAD_DIM + 1, :] + jnp.exp2(sink - m)
        ot_ref[h * HEAD_DIM:(h + 1) * HEAD_DIM, :] = acc[0:HEAD_DIM, :] / denom

    def bounded_group(g):
        heads = [g * GQA_GROUP + hh for hh in range(GQA_GROUP)]
        k_scale = ksc_ref[pl.program_id(0) * A_KV + g]
        offsets = [jnp.sqrt(qn_ref[QN_A + h:QN_A + h + 1, :]) * k_scale for h in heads]
        qp = jnp.concatenate(
            [jnp.concatenate([_padded_q(q_ref, h, g), jnp.where(first_row, -off, 0.0).astype(BF16)], axis=0)
             for h, off in zip(heads, offsets)], axis=1)
        ok_all = jnp.concatenate([jnp.concatenate([ok] * GQA_GROUP, axis=1),
                                  jnp.ones((n_ctx, GQA_GROUP * tq), jnp.bool_)], axis=0)
        k_all = jnp.concatenate([k_loc, k_ctx], axis=0)
        v_all = jnp.concatenate([v_ref[g, :, pl.ds(ks, span)], v_ref[g, :, n_lat:n_lat + n_ctx]], axis=1)
        p = jnp.exp2(jnp.where(ok_all, jnp.dot(k_all, qp, preferred_element_type=F32), NEG_INF)).astype(BF16)
        acc = jnp.dot(v_all, p, preferred_element_type=F32)
        for hh, (h, off) in enumerate(zip(heads, offsets)):
            lanes = slice(hh * tq, (hh + 1) * tq)
            denom = acc[HEAD_DIM:HEAD_DIM + 1, lanes] + jnp.exp2(sink_ref[h] - off)
            ot_ref[h * HEAD_DIM:(h + 1) * HEAD_DIM, :] = acc[0:HEAD_DIM, lanes] / denom

    @pl.when(fast_ref[0] == 1)
    def _():
        for g in range(A_KV):
            bounded_group(g)

    @pl.when(fast_ref[0] != 1)
    def _():
        for g in range(A_KV):
            for hh in range(GQA_GROUP):
                head(g * GQA_GROUP + hh, g, False)

    o_ref[...] = ot_ref[...].T.astype(BF16)


def _attn_a(sink, q_t, q_norm2, k, k_norm2, v_t, n_lat):
    bsz, s_tot, _ = k.shape
    tq = TQ_A
    sink2 = sink.astype(F32) * LOG2_E
    k_scale = jnp.sqrt(jnp.max(k_norm2[:, KN_A:KN_A + A_KV, :], axis=-1)) * BOUND_SLACK
    bound = jnp.sqrt(jnp.max(q_norm2[:, QN_A:QN_A + A_HEADS, :])) * jnp.max(k_scale)
    fast = ((bound <= MAX_SCORE_BOUND) & (jnp.max(jnp.abs(sink2)) <= MAX_SCORE_BOUND)).astype(jnp.int32).reshape(1)
    kern = functools.partial(_attn_a_kernel, n_lat=n_lat, n_ctx=s_tot - n_lat)
    smem = pl.BlockSpec(memory_space=pltpu.SMEM)
    return pl.pallas_call(
        kern,
        grid=(bsz, s_tot // tq),
        in_specs=[smem, smem, smem,
                  pl.BlockSpec((None, QA_W, tq), lambda b, i: (b, 0, i)),
                  pl.BlockSpec((None, QN_ROWS, tq), lambda b, i: (b, 0, i)),
                  pl.BlockSpec((None, s_tot, 2 * KA_W), lambda b, i: (b, 0, 0)),
                  pl.BlockSpec((None, A_KV, V_ROWS, s_tot), lambda b, i: (b, 0, 0, 0))],
        out_specs=pl.BlockSpec((None, tq, QA_W), lambda b, i: (b, i, 0)),
        out_shape=jax.ShapeDtypeStruct((bsz, s_tot, QA_W), BF16),
        scratch_shapes=[pltpu.VMEM((QA_W, tq), F32)],
        compiler_params=_params(("parallel", "arbitrary")),
    )(fast, k_scale.reshape(-1), sink2, q_t, q_norm2, k, v_t)


def _nb_plan(n_lat, s_tot):
    rows = n_lat // GRID_W
    kh = min(NA_ROWS, rows)
    q_rows = TQ_B // GRID_W
    w_rows = kh + q_rows
    n_lat_blocks = n_lat // TQ_B
    n_blocks = s_tot // TQ_B
    starts, variants, keys, reps = [], [], {}, []
    for i in range(n_lat_blocks):
        w0 = int(np.clip(q_rows * i - kh // 2, 0, rows - w_rows))
        r0s = tuple(int(np.clip(q_rows * i + j - kh // 2, 0, rows - kh)) - w0 for j in range(q_rows))
        key = (q_rows * i - w0,) + r0s
        if key not in keys:
            keys[key] = len(reps)
            reps.append((i, w0))
        starts.append(w0 * GRID_W)
        variants.append(keys[key])
    masked = len(reps)
    starts.extend([0] * (n_blocks - n_lat_blocks))
    variants.extend([masked] * (n_blocks - n_lat_blocks))
    return rows, kh, w_rows, reps, np.asarray(starts, np.int32), np.asarray(variants, np.int32)


def _rel_bias_kernel(r_ref, e_ref, m_ref, o_ref):
    o_ref[...] = jnp.dot(r_ref[...], e_ref[...], preferred_element_type=F32,
                         precision=lax.Precision.HIGHEST) + m_ref[...]


def _rel_bias_blocks(rpb):
    depth, heads, n_dr, n_dc = rpb.shape
    kc = np.arange(GRID_W)[:, None]
    cq = np.arange(GRID_W)[None, :]
    dc = (kc - cq + NA_COLS - 1).reshape(-1)
    n_dc_pad = 32
    onehot = (np.arange(n_dc_pad)[:, None] == dc[None, :]).astype(np.float32)
    c0 = np.clip(cq - NA_COLS // 2, 0, GRID_W - NA_COLS)
    outside = ~((kc >= c0) & (kc < c0 + NA_COLS))
    col_mask = np.where(outside, NEG_INF, 0.0).astype(np.float32).reshape(1, -1)
    table = jnp.pad((rpb.astype(F32) * LOG2_E).reshape(depth * heads * n_dr, n_dc), ((0, 0), (0, n_dc_pad - n_dc)))
    blocks = pl.pallas_call(
        _rel_bias_kernel,
        out_shape=jax.ShapeDtypeStruct((depth * heads * n_dr, GRID_W * GRID_W), F32),
    )(table, jnp.asarray(onehot), jnp.asarray(col_mask))
    blocks = blocks.reshape(depth, heads, n_dr, GRID_W, GRID_W)
    return jnp.concatenate([blocks, jnp.full((depth, heads, 1, GRID_W, GRID_W), NEG_INF, F32)], axis=2)


def _nb_bias_tiles(rpb, n_lat, s_tot):
    rows, kh, w_rows, reps, starts, variants = _nb_plan(n_lat, s_tot)
    q_rows = TQ_B // GRID_W
    masked_block = 2 * NA_ROWS - 1
    dr_map = np.full((len(reps) + 1, w_rows, q_rows), masked_block, np.int32)
    for v, (i, w0) in enumerate(reps):
        for a in range(w_rows):
            for b in range(q_rows):
                r = q_rows * i + b
                r0 = int(np.clip(r - kh // 2, 0, rows - kh))
                if r0 <= w0 + a < r0 + kh:
                    dr_map[v, a, b] = w0 + a - r + NA_ROWS - 1
    blocks = _rel_bias_blocks(rpb)[:, :, dr_map]
    depth = rpb.shape[0]
    tiles = jnp.transpose(blocks, (0, 2, 3, 5, 1, 4, 6)).reshape(
        depth, len(reps) + 1, w_rows * GRID_W, B_HEADS * TQ_B)
    return tiles, jnp.asarray(starts), jnp.asarray(variants), w_rows * GRID_W


def _attn_b_kernel(start_ref, var_ref, fast_ref, ksc_ref, q_ref, qn_ref, k_ref, v_ref, bias_ref, o_ref,
                   qp_ref, ot_ref, m_ref, *, n_lat, n_ctx, n_keys):
    tq = q_ref.shape[1]
    ws = pl.multiple_of(start_ref[pl.program_id(1)], 128)
    qp_ref[...] = jnp.zeros_like(qp_ref)
    for h in range(B_HEADS):
        qp_ref[h * HEAD_DIM:(h + 1) * HEAD_DIM, h * tq:(h + 1) * tq] = q_ref[h * HEAD_DIM:(h + 1) * HEAD_DIM, :]
    qp = qp_ref[...]
    s_loc = jnp.dot(k_ref[pl.ds(ws, n_keys), :], qp, preferred_element_type=F32) + bias_ref[...]
    s_ctx = jnp.dot(k_ref[n_lat:n_lat + n_ctx, :], qp, preferred_element_type=F32)
    for bounded in (True, False):
        @pl.when((fast_ref[0] == 1) == bounded)
        def _():
            if bounded:
                m_ref[...] = jnp.concatenate(
                    [jnp.sqrt(qn_ref[QN_B + h:QN_B + h + 1, :]) * ksc_ref[pl.program_id(0) * B_HEADS + h]
                     for h in range(B_HEADS)], axis=1)
            else:
                m_ref[...] = jnp.maximum(jnp.max(s_loc, axis=0, keepdims=True), jnp.max(s_ctx, axis=0, keepdims=True))
    m = m_ref[...]
    p_loc = jnp.exp2(s_loc - m)
    p_ctx = jnp.exp2(s_ctx - m)
    denom = jnp.sum(p_loc, axis=0, keepdims=True) + jnp.sum(p_ctx, axis=0, keepdims=True)
    acc = (jnp.dot(v_ref[:, pl.ds(ws, n_keys)], p_loc.astype(BF16), preferred_element_type=F32)
           + jnp.dot(v_ref[:, n_lat:n_lat + n_ctx], p_ctx.astype(BF16), preferred_element_type=F32))
    for h in range(B_HEADS):
        ot_ref[h * HEAD_DIM:(h + 1) * HEAD_DIM, :] = (
            acc[h * HEAD_DIM:(h + 1) * HEAD_DIM, h * tq:(h + 1) * tq] / denom[:, h * tq:(h + 1) * tq])
    o_ref[...] = ot_ref[...].T.astype(BF16)


def _attn_b(q_t, q_norm2, k, k_norm2, v_t, bias_tiles, max_bias, starts, variants, n_keys, n_lat):
    bsz, s_tot, _ = k.shape
    tq = TQ_B
    k_scale = jnp.sqrt(jnp.max(k_norm2[:, KN_B:KN_B + B_HEADS, :], axis=-1))
    bound = jnp.sqrt(jnp.max(q_norm2[:, QN_B:QN_B + B_HEADS, :])) * jnp.max(k_scale) + max_bias
    fast = (bound <= MAX_SCORE_BOUND).astype(jnp.int32).reshape(1)
    kern = functools.partial(_attn_b_kernel, n_lat=n_lat, n_ctx=s_tot - n_lat, n_keys=n_keys)
    smem = pl.BlockSpec(memory_space=pltpu.SMEM)
    grid_spec = pltpu.PrefetchScalarGridSpec(
        num_scalar_prefetch=2,
        grid=(bsz, s_tot // tq),
        in_specs=[smem, smem,
                  pl.BlockSpec((None, QB_W, tq), lambda b, i, st, va: (b, 0, i)),
                  pl.BlockSpec((None, QN_ROWS, tq), lambda b, i, st, va: (b, 0, i)),
                  pl.BlockSpec((None, s_tot, KB_W), lambda b, i, st, va: (b, 0, 0)),
                  pl.BlockSpec((None, KB_W, s_tot), lambda b, i, st, va: (b, 0, 0)),
                  pl.BlockSpec((None, n_keys, B_HEADS * tq), lambda b, i, st, va: (va[i], 0, 0))],
        out_specs=pl.BlockSpec((None, tq, QB_W), lambda b, i, st, va: (b, i, 0)),
        scratch_shapes=[pltpu.VMEM((QB_W, B_HEADS * tq), BF16), pltpu.VMEM((QB_W, tq), F32),
                        pltpu.VMEM((1, B_HEADS * tq), F32)],
    )
    return pl.pallas_call(
        kern,
        grid_spec=grid_spec,
        out_shape=jax.ShapeDtypeStruct((bsz, s_tot, QB_W), BF16),
        compiler_params=_params(("parallel", "arbitrary")),
    )(starts, variants, fast, k_scale.reshape(-1), q_t, q_norm2, k, v_t, bias_tiles)


def _attn_c_kernel(fast_ref, ksc_ref, q_ref, qn_ref, k_ref, v_ref, o_ref, qp_ref, m_ref, acc_ref, ot_ref,
                   *, n_lat, n_ctx, tk):
    tq = q_ref.shape[1]
    is_ctx = pl.program_id(1) >= n_lat // tq
    n_chunks = jnp.where(is_ctx, 0, n_lat // tk - 1)
    fast = fast_ref[0] == 1
    first_row = lax.broadcasted_iota(jnp.int32, (KC_W, tq), 0) == 0
    for g in range(C_KV):
        k_scale = ksc_ref[pl.program_id(0) * C_KV + g]
        for hh in range(GQA_GROUP):
            h = g * GQA_GROUP + hh
            qp_ref[g, 0:KC_W, hh * tq:(hh + 1) * tq] = _padded_q(q_ref, h, g)
            offset = jnp.where(fast, -jnp.sqrt(qn_ref[h:h + 1, :]) * k_scale, 0.0)
            qp_ref[g, KC_W:2 * KC_W, hh * tq:(hh + 1) * tq] = jnp.where(first_row, offset, 0.0).astype(BF16)
    acc_ref[...] = jnp.zeros_like(acc_ref)

    def bounded_step(g, start, size):
        s = jnp.dot(k_ref[pl.ds(start, size), :], qp_ref[g], preferred_element_type=F32)
        p = jnp.exp2(s).astype(BF16)
        acc_ref[g] += jnp.dot(v_ref[g, :, pl.ds(start, size)], p, preferred_element_type=F32)

    def online_step(g, start, size):
        s = jnp.dot(k_ref[pl.ds(start, size), :], qp_ref[g], preferred_element_type=F32)
        m_prev = m_ref[g]
        m_new = jnp.maximum(m_prev, jnp.max(s, axis=0, keepdims=True))
        p = jnp.exp2(s - m_new).astype(BF16)
        pv = jnp.dot(v_ref[g, :, pl.ds(start, size)], p, preferred_element_type=F32)
        acc_ref[g] = acc_ref[g] * jnp.exp2(m_prev - m_new) + pv
        m_ref[g] = m_new

    def sweep(step):
        def body(c, carry):
            for g in range(C_KV):
                step(g, pl.multiple_of(c * tk, tk), tk)
            return carry

        lax.fori_loop(0, n_chunks, body, 0)

        @pl.when(is_ctx)
        def _():
            for g in range(C_KV):
                step(g, n_lat, n_ctx)

        @pl.when(jnp.logical_not(is_ctx))
        def _():
            for g in range(C_KV):
                step(g, n_lat - tk, tk + n_ctx)

    @pl.when(fast)
    def _():
        sweep(bounded_step)

    @pl.when(jnp.logical_not(fast))
    def _():
        m_ref[...] = jnp.full_like(m_ref, NEG_INF)
        sweep(online_step)

    for g in range(C_KV):
        acc = acc_ref[g]
        out = acc[0:HEAD_DIM, :] / acc[HEAD_DIM:HEAD_DIM + 1, :]
        for hh in range(GQA_GROUP):
            h = g * GQA_GROUP + hh
            ot_ref[h * HEAD_DIM:(h + 1) * HEAD_DIM, :] = out[:, hh * tq:(hh + 1) * tq]
    o_ref[...] = ot_ref[...].T.astype(BF16)


def _attn_c(q_t, q_norm2, k, k_norm2, v_t, n_lat):
    bsz, s_tot, _ = k.shape
    tq = TQ_C
    tk = min(TK_C, n_lat)
    k_scale = jnp.sqrt(jnp.max(k_norm2[:, KN_C:KN_C + C_KV, :], axis=-1)) * BOUND_SLACK
    bound = jnp.sqrt(jnp.max(q_norm2[:, QN_C:QN_C + C_HEADS, :])) * jnp.max(k_scale)
    fast = (bound <= MAX_SCORE_BOUND).astype(jnp.int32).reshape(1)
    kern = functools.partial(_attn_c_kernel, n_lat=n_lat, n_ctx=s_tot - n_lat, tk=tk)
    smem = pl.BlockSpec(memory_space=pltpu.SMEM)
    return pl.pallas_call(
        kern,
        grid=(bsz, s_tot // tq),
        in_specs=[smem, smem,
                  pl.BlockSpec((None, QC_W, tq), lambda b, i: (b, 0, i)),
                  pl.BlockSpec((None, QN_ROWS, tq), lambda b, i: (b, 0, i)),
                  pl.BlockSpec((None, s_tot, 2 * KC_W), lambda b, i: (b, 0, 0)),
                  pl.BlockSpec((None, C_KV, V_ROWS, s_tot), lambda b, i: (b, 0, 0, 0))],
        out_specs=pl.BlockSpec((None, tq, QC_W), lambda b, i: (b, i, 0)),
        out_shape=jax.ShapeDtypeStruct((bsz, s_tot, QC_W), BF16),
        scratch_shapes=[pltpu.VMEM((C_KV, 2 * KC_W, GQA_GROUP * tq), BF16),
                        pltpu.VMEM((C_KV, 1, GQA_GROUP * tq), F32),
                        pltpu.VMEM((C_KV, V_ROWS, GQA_GROUP * tq), F32),
                        pltpu.VMEM((QC_W, tq), F32)],
        compiler_params=_params(("parallel", "arbitrary")),
    )(fast, k_scale.reshape(-1), q_t, q_norm2, k, v_t)


def _store_token_tiles(ref, val, pitch=TILE_ROWS, first=0):
    n = val.shape[0]
    for c in range(TILE_ROWS):
        ref[pl.ds(first + c, n, stride=pitch), :] = val[:, c * 128:(c + 1) * 128]


def _load_token_tiles(ref, n, pitch=TILE_ROWS, first=0):
    return jnp.concatenate([ref[pl.ds(first + c, n, stride=pitch), :] for c in range(TILE_ROWS)], axis=1)


def _tile_copy(src_hbm, dst, idx_ref, j, sem, pitch):
    src = pl.multiple_of(idx_ref[0, 0, j] * pitch, pitch)
    return pltpu.make_async_copy(src_hbm.at[pl.ds(src, pitch), :],
                                 dst.at[pl.ds(pl.multiple_of(j * pitch, pitch), pitch), :], sem)


def _start_gather(src_hbm, dst, idx_ref, n, sem, pitch=TILE_ROWS):
    def issue(i, carry):
        for u in range(GATHER_UNROLL):
            _tile_copy(src_hbm, dst, idx_ref, i * GATHER_UNROLL + u, sem, pitch).start()
        return carry

    lax.fori_loop(0, n // GATHER_UNROLL, issue, 0)


def _wait_gather(src_hbm, dst, n, sem, pitch=TILE_ROWS):
    pltpu.make_async_copy(src_hbm.at[pl.ds(0, n * pitch), :], dst, sem).wait()


def _layer_norm(h, g, b):
    mu = jnp.mean(h, axis=-1, keepdims=True)
    hc = h - mu
    var = jnp.mean(hc * hc, axis=-1, keepdims=True)
    return hc * lax.rsqrt(var + LN_EPS) * g + b


def _post_kernel(oa_ref, ob_ref, oc_ref, x_ref, mod_ref, w_ref, g_ref, b_ref, wr_ref, rb_ref,
                 x1_ref, u2_ref, eid_ref, gate_ref, o_scr, ub_scr, *, alpha):
    tm = x_ref.shape[0]
    o_cat = jnp.concatenate([oa_ref[...], ob_ref[...], oc_ref[...]], axis=1)
    o_scr[...] = jnp.dot(o_cat, w_ref[...], preferred_element_type=F32)
    gate1 = 1.0 + mod_ref[2:3, :]
    scale2 = 1.0 + mod_ref[4:5, :]
    shift2 = mod_ref[3:4, :]
    ln_g = g_ref[...]
    ln_b = b_ref[...]

    def row_group(r, carry):
        r0 = pl.multiple_of(r * LN_ROWS, LN_ROWS)
        rows = pl.ds(r0, LN_ROWS)
        x1 = _layer_norm(alpha * x_ref[rows, :] + gate1 * o_scr[rows, :], ln_g, ln_b)
        x1_ref[rows, :] = x1
        u2 = x1 * scale2 + shift2
        ub_scr[rows, :] = u2.astype(BF16)
        for c in range(TILE_ROWS):
            u2_ref[pl.ds(r0 * TILE_ROWS + c, LN_ROWS, stride=TILE_ROWS), :] = u2[:, c * 128:(c + 1) * 128]
        return carry

    lax.fori_loop(0, tm // LN_ROWS, row_group, 0, unroll=LN_UNROLL)
    logits = lax.dot_general(wr_ref[...], ub_scr[...], (((1,), (1,)), ((), ())), preferred_element_type=F32)
    aff_all = jax.nn.sigmoid(logits)
    sel_all = aff_all + rb_ref[...]
    aff = [aff_all[e:e + 1, :] for e in range(N_EXPERTS)]
    sel = [sel_all[e:e + 1, :] for e in range(N_EXPERTS)]
    gsum = []
    for g in range(N_EXPERT_GROUPS):
        a, b, c, d = sel[g * PER_GROUP:(g + 1) * PER_GROUP]
        hi1, lo1, hi2, lo2 = jnp.maximum(a, b), jnp.minimum(a, b), jnp.maximum(c, d), jnp.minimum(c, d)
        gsum.append(jnp.maximum(hi1, hi2) + jnp.maximum(jnp.minimum(hi1, hi2), jnp.maximum(lo1, lo2)))
    best_g = jnp.zeros_like(gsum[0], dtype=jnp.int32)
    best_v = gsum[0]
    for g in range(1, N_EXPERT_GROUPS):
        better = gsum[g] > best_v
        best_g = jnp.where(better, g, best_g)
        best_v = jnp.where(better, gsum[g], best_v)
    s_loc, a_loc = [], []
    for j in range(PER_GROUP):
        sj, aj = sel[j], aff[j]
        for g in range(1, N_EXPERT_GROUPS):
            pick = best_g == g
            sj = jnp.where(pick, sel[g * PER_GROUP + j], sj)
            aj = jnp.where(pick, aff[g * PER_GROUP + j], aj)
        s_loc.append(sj)
        a_loc.append(aj)
    i1 = jnp.zeros_like(best_g)
    v1, g1 = s_loc[0], a_loc[0]
    for j in range(1, PER_GROUP):
        better = s_loc[j] > v1
        i1 = jnp.where(better, j, i1)
        v1 = jnp.where(better, s_loc[j], v1)
        g1 = jnp.where(better, a_loc[j], g1)
    i2 = jnp.full_like(best_g, -1)
    v2 = jnp.full_like(v1, -jnp.inf)
    g2 = jnp.zeros_like(g1)
    for j in range(PER_GROUP):
        better = (i1 != j) & ((s_loc[j] > v2) | (i2 < 0))
        i2 = jnp.where(better, j, i2)
        v2 = jnp.where(better, s_loc[j], v2)
        g2 = jnp.where(better, a_loc[j], g2)
    total = g1 + g2
    lo, hi = jnp.minimum(i1, i2), jnp.maximum(i1, i2)
    pair = jnp.where(lo == 0, hi - 1, jnp.where(lo == 1, hi + 1, PAIRS_PER_GROUP - 1))
    first_is_lo = i1 < i2
    g_lo = jnp.where(first_is_lo, g1, g2) / total
    g_hi = jnp.where(first_is_lo, g2, g1) / total
    eid_ref[...] = jnp.concatenate([best_g * PAIRS_PER_GROUP + pair,
                                    jnp.zeros((MOD_ROWS - 1, tm), jnp.int32)], axis=0)
    gate_ref[...] = jnp.concatenate([g_lo, g_hi, jnp.zeros((MOD_ROWS - TOP_K, tm), F32)], axis=0)


def _post(oa, ob, oc, xa, mod, w_out, ln_g, ln_b, wr_t, rb, n_lat, alpha):
    bsz, s_tot, d = xa.shape
    tm = TOK_TILE
    n_lat_tiles = n_lat // tm
    n_tiles = s_tot // tm
    sd = jax.ShapeDtypeStruct

    def tmajor(width):
        return pl.BlockSpec((None, tm, width), lambda b, i: (b, i, 0))

    def const(shape):
        return pl.BlockSpec(shape, lambda b, i: (0,) * len(shape))

    lanes = pl.BlockSpec((MOD_ROWS, tm), lambda b, i: (0, b * n_tiles + i))
    return pl.pallas_call(
        functools.partial(_post_kernel, alpha=alpha),
        grid=(bsz, n_tiles),
        in_specs=[tmajor(QA_W), tmajor(QB_W), tmajor(QC_W), tmajor(d),
                  pl.BlockSpec((None, N_MOD, d), lambda b, i: (jnp.where(i >= n_lat_tiles, bsz, b), 0, 0)),
                  const((d, d)), const((1, d)), const((1, d)), const((N_EXPERTS, d)), const((N_EXPERTS, 1))],
        out_specs=[tmajor(d), pl.BlockSpec((tm * TILE_ROWS, 128), lambda b, i: (b * n_tiles + i, 0)), lanes, lanes],
        out_shape=[sd((bsz, s_tot, d), F32), sd((bsz * s_tot * TILE_ROWS, 128), F32),
                   sd((MOD_ROWS, bsz * s_tot), jnp.int32), sd((MOD_ROWS, bsz * s_tot), F32)],
        scratch_shapes=[pltpu.VMEM((tm, d), F32), pltpu.VMEM((tm, d), BF16)],
        compiler_params=_params(("parallel", "parallel")),
    )(oa, ob, oc, xa, mod, w_out, ln_g, ln_b, wr_t, rb)


def _ffn_kernel(elo_ref, ehi_ref, nused_ref, tok_ref, tok_next_ref, u_hbm,
                w1lo_ref, w3lo_ref, w2lo_ref, w1hi_ref, w3hi_ref, w2hi_ref, y_ref,
                xbuf, w1lo, w3lo, w2lo, w1hi, w3hi, w2hi, sem):
    blk = y_ref.shape[0] // (TOP_K * TILE_ROWS)
    step = pl.program_id(0)
    n_used = nused_ref[0]
    slot = step % 2
    prev = jnp.maximum(step - 1, 0)

    @pl.when(step < n_used)
    def _():
        @pl.when(step == 0)
        def _():
            _start_gather(u_hbm, xbuf.at[0], tok_ref, blk, sem.at[0])

        @pl.when(step + 1 < n_used)
        def _():
            _start_gather(u_hbm, xbuf.at[1 - slot], tok_next_ref, blk, sem.at[1 - slot])

        @pl.when((step == 0) | (elo_ref[step] != elo_ref[prev]))
        def _():
            w1lo[...] = w1lo_ref[...].astype(BF16)
            w3lo[...] = w3lo_ref[...].astype(BF16)
            w2lo[...] = w2lo_ref[...].astype(BF16)

        @pl.when((step == 0) | (ehi_ref[step] != ehi_ref[prev]))
        def _():
            w1hi[...] = w1hi_ref[...].astype(BF16)
            w3hi[...] = w3hi_ref[...].astype(BF16)
            w2hi[...] = w2hi_ref[...].astype(BF16)

        _wait_gather(u_hbm, xbuf.at[slot], blk, sem.at[slot])
        xb = _load_token_tiles(xbuf.at[slot], blk).astype(BF16)
        for first, (w1b, w3b, w2b) in ((0, (w1lo, w3lo, w2lo)), (TILE_ROWS, (w1hi, w3hi, w2hi))):
            h1 = jnp.dot(xb, w1b[...], preferred_element_type=F32)
            h3 = jnp.dot(xb, w3b[...], preferred_element_type=F32)
            hid = (h1 * jax.nn.sigmoid(h1) * h3).astype(BF16)
            _store_token_tiles(y_ref, jnp.dot(hid, w2b[...], preferred_element_type=F32),
                               pitch=TOP_K * TILE_ROWS, first=first)

    @pl.when(step >= n_used)
    def _():
        y_ref[...] = jnp.zeros_like(y_ref)


def _expert_ffn(block_lo, block_hi, n_used, slot_tok, u2_tiles, w1, w3, w2, layer):
    d = w1.shape[2]
    n_blocks = block_lo.shape[0]
    blk = CLASS_BLOCK
    slot_tok = slot_tok.reshape(n_blocks, 1, blk)

    def weight(shape, which):
        return pl.BlockSpec((None, None) + shape, lambda i, lo, hi, nu: (layer, (lo, hi)[which][i], 0, 0))

    up, down = (d, D_EXPERT), (D_EXPERT, d)
    grid_spec = pltpu.PrefetchScalarGridSpec(
        num_scalar_prefetch=3,
        grid=(n_blocks,),
        in_specs=[pl.BlockSpec((1, 1, blk), lambda i, lo, hi, nu: (i, 0, 0), memory_space=pltpu.SMEM),
                  pl.BlockSpec((1, 1, blk), lambda i, lo, hi, nu: (jnp.minimum(i + 1, n_blocks - 1), 0, 0),
                               memory_space=pltpu.SMEM),
                  pl.BlockSpec(memory_space=pl.ANY),
                  weight(up, 0), weight(up, 0), weight(down, 0), weight(up, 1), weight(up, 1), weight(down, 1)],
        out_specs=pl.BlockSpec((blk * TOP_K * TILE_ROWS, 128), lambda i, lo, hi, nu: (i, 0)),
        scratch_shapes=[pltpu.VMEM((2, blk * TILE_ROWS, 128), F32)]
        + [pltpu.VMEM(s, BF16) for s in (up, up, down, up, up, down)] + [pltpu.SemaphoreType.DMA((2,))],
    )
    return pl.pallas_call(
        _ffn_kernel,
        grid_spec=grid_spec,
        out_shape=jax.ShapeDtypeStruct((n_blocks * blk * TOP_K * TILE_ROWS, 128), F32),
        compiler_params=_params(("arbitrary",)),
    )(block_lo, block_hi, n_used, slot_tok, slot_tok, u2_tiles, w1, w3, w2, w1, w3, w2)


def _comb_kernel(pos_ref, pos_next_ref, y_hbm, x1_ref, gate_ref, mod_ref, g_ref, b_ref, o_ref, ybuf, sem,
                 *, alpha, latent_tiles):
    tm = x1_ref.shape[0]
    pitch = TOP_K * TILE_ROWS
    step = pl.program_id(0)
    slot = step % 2

    @pl.when(step == 0)
    def _():
        _start_gather(y_hbm, ybuf.at[0], pos_ref, tm, sem.at[0], pitch)

    @pl.when(step + 1 < pl.num_programs(0))
    def _():
        _start_gather(y_hbm, ybuf.at[1 - slot], pos_next_ref, tm, sem.at[1 - slot], pitch)

    _wait_gather(y_hbm, ybuf.at[slot], tm, sem.at[slot], pitch)
    y_lo = _load_token_tiles(ybuf.at[slot], tm, pitch, 0)
    y_hi = _load_token_tiles(ybuf.at[slot], tm, pitch, TILE_ROWS)
    f = gate_ref[:, 0:1] * y_lo + gate_ref[:, 1:2] * y_hi

    def finish():
        o_ref[...] = _layer_norm(alpha * x1_ref[...] + (1.0 + mod_ref[5:6, :]) * f, g_ref[...], b_ref[...])

    if latent_tiles is None:
        finish()
    else:
        n_tiles, n_lat_tiles = latent_tiles
        pl.when(step % n_tiles < n_lat_tiles)(finish)


def _combine(pos, y_tiles, x1, gates, mod, ln_g, ln_b, n_lat, alpha, latent_only):
    bsz, s_tot, d = x1.shape
    tm = TOK_TILE
    n_lat_tiles = n_lat // tm
    n_tiles = s_tot // tm
    n_steps = bsz * n_tiles
    out_tokens = n_lat if latent_only else s_tot

    def mod_row(t):
        return jnp.where(t % n_tiles >= n_lat_tiles, bsz, t // n_tiles)

    def out_block(t):
        if not latent_only:
            return t
        return (t // n_tiles) * n_lat_tiles + jnp.minimum(t % n_tiles, n_lat_tiles - 1)

    out = pl.pallas_call(
        functools.partial(_comb_kernel, alpha=alpha, latent_tiles=(n_tiles, n_lat_tiles) if latent_only else None),
        grid=(n_steps,),
        in_specs=[pl.BlockSpec((1, 1, tm), lambda t: (t, 0, 0), memory_space=pltpu.SMEM),
                  pl.BlockSpec((1, 1, tm), lambda t: (jnp.minimum(t + 1, n_steps - 1), 0, 0),
                               memory_space=pltpu.SMEM),
                  pl.BlockSpec(memory_space=pl.ANY),
                  pl.BlockSpec((tm, d), lambda t: (t, 0)),
                  pl.BlockSpec((tm, TOP_K), lambda t: (t, 0)),
                  pl.BlockSpec((None, N_MOD, d), lambda t: (mod_row(t), 0, 0)),
                  pl.BlockSpec((1, d), lambda t: (0, 0)),
                  pl.BlockSpec((1, d), lambda t: (0, 0))],
        out_specs=pl.BlockSpec((tm, d), lambda t: (out_block(t), 0)),
        out_shape=jax.ShapeDtypeStruct((bsz * out_tokens, d), F32),
        scratch_shapes=[pltpu.VMEM((2, TOP_K * tm * TILE_ROWS, 128), F32), pltpu.SemaphoreType.DMA((2,))],
        compiler_params=_params(("arbitrary",)),
    )(pos, pos, y_tiles, x1.reshape(bsz * s_tot, d), gates, mod, ln_g, ln_b)
    return out.reshape(bsz, out_tokens, d)


def _deinterleave(n_heads):
    one = np.concatenate([np.arange(0, HEAD_DIM, 2), np.arange(1, HEAD_DIM, 2)])
    return np.concatenate([h * HEAD_DIM + one for h in range(n_heads)])


def _proj_weight(w_in_l):
    sizes = (QA_W, KA_W, KA_W, QB_W, KB_W, KB_W, QC_W, KC_W, KC_W)
    offs = np.concatenate([[0], np.cumsum(sizes)])
    qa, ka, va, qb, kb, vb, qc, kc, vc = [w_in_l[:, offs[i]:offs[i + 1]] for i in range(len(sizes))]
    scale = HEAD_DIM ** -0.5 * LOG2_E
    qa = qa[:, _deinterleave(A_HEADS)] * scale
    ka = ka[:, _deinterleave(A_KV)]
    qb = qb * scale
    qc = qc[:, _deinterleave(C_HEADS)]
    kc = kc[:, _deinterleave(C_KV)]
    return jnp.concatenate([qa, qb, qc, ka, kb, kc, va, vb, vc], axis=1).T.astype(BF16)


def _rope_tables(n_lat, n_ctx):
    t = np.arange(n_lat)
    inv_freq = ROPE_BASE ** (-np.arange(0, HALF, 2, dtype=np.float32) / HALF)
    ang = jnp.concatenate([jnp.asarray(t // GRID_W, F32)[None, :] * jnp.asarray(inv_freq)[:, None],
                           jnp.asarray(t % GRID_W, F32)[None, :] * jnp.asarray(inv_freq)[:, None]], axis=0)
    cos = jnp.concatenate([jnp.cos(ang), jnp.ones((HALF, n_ctx), F32)], axis=1)
    sin = jnp.concatenate([jnp.sin(ang), jnp.zeros((HALF, n_ctx), F32)], axis=1)
    return cos, sin


def _dispatch(cls, n_tok):
    onehot = (cls[:, None] == jnp.arange(N_CLASSES, dtype=jnp.int32)[None, :]).astype(jnp.int32)
    csum = jnp.cumsum(onehot, axis=0)
    counts = csum[-1]
    padded = (counts + CLASS_BLOCK - 1) // CLASS_BLOCK * CLASS_BLOCK
    pad_ends = jnp.cumsum(padded)
    dest = jnp.sum(onehot * (csum - 1 + (pad_ends - padded)[None, :]), axis=1)
    n_blocks = -(-n_tok // CLASS_BLOCK) + N_CLASSES
    slot_tok = jnp.zeros((n_blocks * CLASS_BLOCK,), jnp.int32).at[dest].set(jnp.arange(n_tok, dtype=jnp.int32))
    block_start = jnp.arange(n_blocks, dtype=jnp.int32) * CLASS_BLOCK
    block_cls = jnp.minimum(jnp.sum((block_start[:, None] >= pad_ends[None, :]).astype(jnp.int32), axis=1),
                            N_CLASSES - 1)
    group, pair = block_cls // PAIRS_PER_GROUP, block_cls % PAIRS_PER_GROUP
    lo = (pair >= 3).astype(jnp.int32) + (pair >= 5).astype(jnp.int32)
    hi = jnp.where(pair < 3, pair + 1, jnp.where(pair < 5, pair - 1, 3))
    n_used = (pad_ends[-1] // CLASS_BLOCK).astype(jnp.int32).reshape(1)
    return slot_tok, group * PER_GROUP + lo, group * PER_GROUP + hi, dest.astype(jnp.int32), n_used


def kernel(x, c, ctx, c_ctx, w_in, w_out, sink, rpb, q_gain, k_gain, w_ada, b_ada,
           ln1_g, ln1_b, ln2_g, ln2_b, w_router, router_bias, w1, w3, w2):
    bsz, n_lat, d = x.shape
    n_ctx = ctx.shape[1]
    s_tot = n_lat + n_ctx
    depth = w_in.shape[0]
    assert d == D_MODEL and n_ctx == TOK_TILE and n_lat % min(TK_C, n_lat) == 0
    assert n_lat % (2 * TOK_TILE) == 0 and bsz + 1 <= MOD_ROWS
    alpha = float((2 * depth) ** 0.25)
    n_tok = bsz * s_tot
    tm = TOK_TILE

    cond = jnp.zeros((MOD_ROWS, d), F32).at[:bsz].set(c).at[bsz].set(c_ctx)
    mods = _ada_table(cond, w_ada, b_ada).reshape(depth, MOD_ROWS, N_MOD, d)
    cos_t, sin_t = _rope_tables(n_lat, n_ctx)
    gain_perm = _deinterleave(1)
    wr_t = w_router.T.astype(BF16)
    rb = router_bias.reshape(N_EXPERTS, 1).astype(F32)

    bias_tiles, starts, variants, n_keys = _nb_bias_tiles(rpb, n_lat, s_tot)

    xa = jnp.concatenate([x, ctx], axis=1)
    for l in range(depth):
        mod = mods[l]
        qa, qb, qc, ka, kb, kc, va, vb, vc, qn, kn = _project(
            xa, mod, _proj_weight(w_in[l]), cos_t, sin_t,
            q_gain[l][gain_perm].reshape(HEAD_DIM, 1) * (HEAD_DIM ** -0.5 * LOG2_E),
            k_gain[l][gain_perm].reshape(HEAD_DIM, 1), n_lat)
        oa = _attn_a(sink[l], qa, qn, ka, kn, va, n_lat)
        ob = _attn_b(qb, qn, kb, kn, vb, bias_tiles[l], jnp.max(jnp.abs(rpb[l])) * LOG2_E,
                     starts, variants, n_keys, n_lat)
        oc = _attn_c(qc, qn, kc, kn, vc, n_lat)
        x1, u2, eid8, gate8 = _post(oa, ob, oc, xa, mod, w_out[l].astype(BF16),
                                    ln1_g[l].reshape(1, d), ln1_b[l].reshape(1, d), wr_t, rb, n_lat, alpha)
        slot_tok, block_lo, block_hi, dest, n_used = _dispatch(eid8[0], n_tok)
        y = _expert_ffn(block_lo, block_hi, n_used, slot_tok, u2, w1, w3, w2, l)
        xa = _combine(dest.reshape(n_tok // tm, 1, tm), y, x1, gate8[:TOP_K].T, mod,
                      ln2_g[l].reshape(1, d), ln2_b[l].reshape(1, d), n_lat, alpha, latent_only=l == depth - 1)
    return xa
```
